```python
import math
import jax, jax.numpy as jnp
from jax import lax
import numpy as np

D_MODEL = 1024
BATCH = 2
SEQ = 8192
DEPTH = 2
DEC_BATCH = 4
DEC_SEQ = 8192
PAST_LEN = 128

MIX_WIDTH = D_MODEL
GROUP_WIDTH = MIX_WIDTH // 4
HEAD_DIM = 64
N_HEADS_GROUP = GROUP_WIDTH // HEAD_DIM

MLA_HEADS = N_HEADS_GROUP
MLA_Q_LORA = 192
MLA_KV_LORA = 128
MLA_NOPE = 64
MLA_ROPE = 32
MLA_V = GROUP_WIDTH // MLA_HEADS
ROPE_THETA = 10000.0

DIFF_HEADS = N_HEADS_GROUP
DIFF_V = GROUP_WIDTH // DIFF_HEADS
DIFF_HALF = DIFF_V // 2

MLSTM_HEADS = N_HEADS_GROUP
MLSTM_DH = GROUP_WIDTH // MLSTM_HEADS
MLSTM_CHUNK = 128
MLSTM_CONV = 3

NA_HEADS = N_HEADS_GROUP
NA_DH = GROUP_WIDTH // NA_HEADS
GRID_W = 64
NA_MAX_ROWS = 8
NA_COLS = 16

QUERY_BLOCK = 128
D_FF = 2816
N_EXPERTS = 8
TOP_K = 2
D_EXPERT = 3584
N_DENSE = (DEPTH + 1) // 2
N_MOE = DEPTH // 2
EPS = 1e-6

IN_SECTIONS = (
    MLA_Q_LORA, MLA_KV_LORA, MLA_ROPE,
    GROUP_WIDTH, GROUP_WIDTH, GROUP_WIDTH,
    2 * GROUP_WIDTH, GROUP_WIDTH, GROUP_WIDTH, 4 * MLSTM_HEADS,
    GROUP_WIDTH, GROUP_WIDTH, GROUP_WIDTH,
)
IN_COLS = sum(IN_SECTIONS)

kernel_name = 'hybrid_parallel_group_bidir_encoder'


def rmsnorm(x, g):
    xf = x.astype(jnp.float32)
    y = xf * lax.rsqrt(jnp.mean(xf * xf, axis=-1, keepdims=True) + EPS)
    return (y * g.astype(jnp.float32)).astype(x.dtype)


def split_heads(t, n_heads):
    b, s, _ = t.shape
    return t.reshape(b, s, n_heads, -1).transpose(0, 2, 1, 3)


def merge_heads(t):
    b, h, s, d = t.shape
    return t.transpose(0, 2, 1, 3).reshape(b, s, h * d)


def sweep_query_blocks(block_fn, qs):
    b, h, s, _ = qs[0].shape
    nb = s // QUERY_BLOCK
    def to_blocks(q):
        return q.reshape(b, h, nb, QUERY_BLOCK, q.shape[-1]).transpose(2, 0, 1, 3, 4)
    starts = jnp.arange(nb, dtype=jnp.int32) * QUERY_BLOCK
    out = lax.map(lambda args: block_fn(*args), (starts,) + tuple(to_blocks(q) for q in qs))
    return out.transpose(1, 2, 0, 3, 4).reshape(b, h, s, out.shape[-1])


def rope(t, pos):
    half = MLA_ROPE // 2
    inv = ROPE_THETA ** (-jnp.arange(half, dtype=jnp.float32) / half)
    ang = pos.astype(jnp.float32)[:, None] * inv[None, :]
    cos, sin = jnp.cos(ang), jnp.sin(ang)
    tf = t.astype(jnp.float32)
    t1, t2 = tf[..., :half], tf[..., half:]
    return jnp.concatenate([t1 * cos - t2 * sin, t2 * cos + t1 * sin], axis=-1).astype(t.dtype)


def mla_mixer(c_q, c_kv, k_rope, q_norm, kv_norm, w_uq, w_ukv):
    s = c_q.shape[1]
    pos = jnp.arange(s)
    q = split_heads(rmsnorm(c_q, q_norm) @ w_uq, MLA_HEADS)
    q_nope, q_rope = q[..., :MLA_NOPE], rope(q[..., MLA_NOPE:], pos)
    kv = split_heads(rmsnorm(c_kv, kv_norm) @ w_ukv, MLA_HEADS)
    k_nope, v = kv[..., :MLA_NOPE], kv[..., MLA_NOPE:]
    k_r = rope(k_rope, pos)
    scale = (MLA_NOPE + MLA_ROPE) ** -0.5
    def block(start, qn, qr):
        sc = jnp.einsum('bhqd,bhkd->bhqk', qn, k_nope) + jnp.einsum('bhqr,bkr->bhqk', qr, k_r)
        p = jax.nn.softmax(sc.astype(jnp.float32) * scale, axis=-1).astype(v.dtype)
        return jnp.einsum('bhqk,bhkd->bhqd', p, v)
    return merge_heads(sweep_query_blocks(block, (q_nope, q_rope)))


def diff_mixer(q, k, v, lam_params, subln, layer_idx):
    s = q.shape[1]
    q, k, v = split_heads(q, DIFF_HEADS), split_heads(k, DIFF_HEADS), split_heads(v, DIFF_HEADS)
    q1, q2 = q[..., :DIFF_HALF], q[..., DIFF_HALF:]
    k1, k2 = k[..., :DIFF_HALF], k[..., DIFF_HALF:]
    lam_init = 0.8 - 0.6 * math.exp(-0.3 * layer_idx)
    lp = lam_params.astype(jnp.float32)
    lam = jnp.exp(jnp.sum(lp[0] * lp[1])) - jnp.exp(jnp.sum(lp[2] * lp[3])) + lam_init
    slopes = 2.0 ** (-8.0 * jnp.arange(1, DIFF_HEADS + 1, dtype=jnp.float32) / DIFF_HEADS)
    kpos = jnp.arange(s, dtype=jnp.float32)
    scale = DIFF_HALF ** -0.5
    def block(start, qa, qb):
        qpos = (start + jnp.arange(QUERY_BLOCK)).astype(jnp.float32)
        bias = -slopes[:, None, None] * jnp.abs(qpos[:, None] - kpos[None, :])
        s1 = jnp.einsum('bhqd,bhkd->bhqk', qa, k1).astype(jnp.float32) * scale + bias
        s2 = jnp.einsum('bhqd,bhkd->bhqk', qb, k2).astype(jnp.float32) * scale + bias
        p = (jax.nn.softmax(s1, axis=-1) - lam * jax.nn.softmax(s2, axis=-1)).astype(v.dtype)
        return jnp.einsum('bhqk,bhkd->bhqd', p, v)
    o = sweep_query_blocks(block, (q1, q2))
    o = rmsnorm(o, subln) * (1.0 - lam_init)
    return merge_heads(o)


def mlstm_scan(q, k, v, log_i, log_f):
    b, h, s, d = q.shape
    L = MLSTM_CHUNK
    nc = s // L
    def chunks(t):
        return jnp.moveaxis(t.reshape((b, h, nc, L) + t.shape[3:]), 2, 0)
    tril = jnp.tril(jnp.ones((L, L), dtype=bool))
    def step(carry, inp):
        C, n, m = carry
        qc, kc, vc, li, lf = inp
        bcum = jnp.cumsum(lf, axis=-1)
        dlog = jnp.where(tril, bcum[..., :, None] - bcum[..., None, :] + li[..., None, :], -jnp.inf)
        inter = bcum + m[..., None]
        m_row = jnp.maximum(inter, jnp.max(dlog, axis=-1))
        w_intra = jnp.exp(dlog - m_row[..., None])
        w_inter = jnp.exp(inter - m_row)
        a = jnp.einsum('bhjd,bhsd->bhjs', qc, kc) * w_intra
        num = w_inter[..., None] * jnp.einsum('bhed,bhjd->bhje', C, qc) + jnp.einsum('bhjs,bhse->bhje', a, vc)
        den = w_inter * jnp.einsum('bhd,bhjd->bhj', n, qc) + jnp.sum(a, axis=-1)
        hc = num / jnp.maximum(jnp.abs(den), jnp.exp(-m_row))[..., None]
        b_last = bcum[..., -1]
        logw_end = b_last[..., None] - bcum + li
        m_new = jnp.maximum(b_last + m, jnp.max(logw_end, axis=-1))
        w_end = jnp.exp(logw_end - m_new[..., None])
        decay = jnp.exp(b_last + m - m_new)
        C_new = decay[..., None, None] * C + jnp.einsum('bhse,bhsd->bhed', w_end[..., None] * vc, kc)
        n_new = decay[..., None] * n + jnp.einsum('bhs,bhsd->bhd', w_end, kc)
        return (C_new, n_new, m_new), hc
    init = (jnp.zeros((b, h, d, d), jnp.float32), jnp.zeros((b, h, d), jnp.float32),
            jnp.zeros((b, h), jnp.float32))
    _, hs = lax.scan(step, init, (chunks(q), chunks(k), chunks(v), chunks(log_i), chunks(log_f)))
    return jnp.moveaxis(hs, 0, 2).reshape(b, h, s, d)


def mlstm_mixer(qk_in, v, o_pre, gates_pre, conv_w, gate_bias, norm_g):
    dtype = v.dtype
    c = qk_in.shape[-1]
    pad = MLSTM_CONV // 2
    qk = lax.conv_general_dilated(qk_in, conv_w[:, None, :].astype(qk_in.dtype), window_strides=(1,),
                                  padding=[(pad, pad)], dimension_numbers=('NWC', 'WIO', 'NWC'),
                                  feature_group_count=c)
    qk = jax.nn.silu(qk)
    q = split_heads(qk[..., :GROUP_WIDTH], MLSTM_HEADS).astype(jnp.float32)
    k = split_heads(qk[..., GROUP_WIDTH:], MLSTM_HEADS).astype(jnp.float32) * (MLSTM_DH ** -0.5)
    vh = split_heads(v, MLSTM_HEADS).astype(jnp.float32)
    g = (gates_pre.astype(jnp.float32) + gate_bias.astype(jnp.float32)).transpose(0, 2, 1)
    i_f, f_f, i_b, f_b = jnp.split(g, 4, axis=1)
    h_fwd = mlstm_scan(q, k, vh, i_f, jax.nn.log_sigmoid(f_f))
    flip = lambda t: jnp.flip(t, axis=2)
    h_bwd = flip(mlstm_scan(flip(q), flip(k), flip(vh), flip(i_b), flip(jax.nn.log_sigmoid(f_b))))
    hn = rmsnorm(h_fwd + h_bwd, norm_g[:, None, :])
    return (merge_heads(hn) * jax.nn.sigmoid(o_pre.astype(jnp.float32))).astype(dtype)


def na_mixer(q, k, v, rpb):
    b, s, _ = q.shape
    rows = s // GRID_W
    wr = min(NA_MAX_ROWS, rows)
    def grid(t):
        return split_heads(t, NA_HEADS).reshape(b, NA_HEADS, rows, GRID_W, NA_DH)
    qg, kg, vg = grid(q), grid(k), grid(v)
    cols = np.arange(GRID_W)
    col_start = np.clip(cols - NA_COLS // 2, 0, GRID_W - NA_COLS)
    col_idx = col_start[:, None] + np.arange(NA_COLS)[None, :]
    col_rel = col_idx - cols[:, None] + (NA_COLS - 1)
    scale = NA_DH ** -0.5
    def row_block(args):
        r, q_row = args
        r0 = jnp.clip(r - wr // 2, 0, rows - wr)
        k_rows = lax.dynamic_slice_in_dim(kg, r0, wr, axis=2)
        v_rows = lax.dynamic_slice_in_dim(vg, r0, wr, axis=2)
        k_win = k_rows[:, :, :, col_idx, :]
        v_win = v_rows[:, :, :, col_idx, :]
        row_rel = r0 + jnp.arange(wr) - r + (NA_MAX_ROWS - 1)
        bias = rpb[:, row_rel[None, :, None], col_rel[:, None, :]]
        sc = jnp.einsum('bhcd,bhwcjd->bhcwj', q_row, k_win).astype(jnp.float32) * scale \
            + bias.astype(jnp.float32)[None]
        p = jax.nn.softmax(sc.reshape(b, NA_HEADS, GRID_W, wr * NA_COLS), axis=-1)
        p = p.reshape(sc.shape).astype(v.dtype)
        return jnp.einsum('bhcwj,bhwcjd->bhcd', p, v_win)
    out = lax.map(row_block, (jnp.arange(rows, dtype=jnp.int32), jnp.moveaxis(qg, 2, 0)))
    out = jnp.moveaxis(out, 0, 2).reshape(b, NA_HEADS, s, NA_DH)
    return merge_heads(out)


def swiglu(x, wg, wu, wd):
    return (jax.nn.silu(x @ wg) * (x @ wu)) @ wd


def moe_ffn(x, router, wg, wu, wd):
    b, s, d = x.shape
    xt = x.reshape(b * s, d)
    logits = (xt @ router).astype(jnp.float32)
    top_val, top_idx = lax.top_k(logits, TOP_K)
    top_w = jax.nn.softmax(top_val, axis=-1)
    combine = jnp.sum(jax.nn.one_hot(top_idx, N_EXPERTS, dtype=jnp.float32) * top_w[..., None], axis=1)
    combine = combine.astype(x.dtype)
    y = jnp.zeros_like(xt)
    for e in range(N_EXPERTS):
        y = y + combine[:, e:e + 1] * swiglu(xt, wg[e], wu[e], wd[e])
    return y.reshape(b, s, d)


def token_mix(h, i, p):
    proj = h @ p['w_in'][i]
    split_at = np.cumsum(IN_SECTIONS)[:-1].tolist()
    (a_cq, a_ckv, a_kr, b_q, b_k, b_v, c_qk, c_v, c_o, c_g, d_q, d_k, d_v) = jnp.split(proj, split_at, axis=-1)
    y_a = mla_mixer(a_cq, a_ckv, a_kr, p['mla_q_norm'][i], p['mla_kv_norm'][i],
                    p['mla_w_uq'][i], p['mla_w_ukv'][i])
    y_b = diff_mixer(b_q, b_k, b_v, p['diff_lambda'][i], p['diff_subln'][i], i)
    y_c = mlstm_mixer(c_qk, c_v, c_o, c_g, p['mlstm_conv'][i], p['mlstm_gate_bias'][i], p['mlstm_norm'][i])
    y_d = na_mixer(d_q, d_k, d_v, p['na_rpb'][i])
    return jnp.concatenate([y_a, y_b, y_c, y_d], axis=-1) @ p['w_out'][i]


def trunk(x, p):
    for i in range(DEPTH):
        x = x + token_mix(rmsnorm(x, p['norm_mix'][i]), i, p)
        h = rmsnorm(x, p['norm_ffn'][i])
        j = i // 2
        if i % 2 == 0:
            x = x + swiglu(h, p['ffn_w_gate'][j], p['ffn_w_up'][j], p['ffn_w_down'][j])
        else:
            x = x + moe_ffn(h, p['moe_router'][j], p['moe_w_gate'][j], p['moe_w_up'][j], p['moe_w_down'][j])
    return rmsnorm(x, p['norm_final'])


def setup_inputs(seed: int = 0) -> dict:
    key = jax.random.key(seed)
    ks = jax.random.split(key, 32)
    f32 = jnp.float32
    nrm = lambda k, shape, scale: jax.random.normal(k, shape, f32) * scale
    gain = lambda k, shape: 1.0 + 0.02 * jax.random.normal(k, shape, f32)
    f_bias = jnp.linspace(3.0, 6.0, MLSTM_HEADS, dtype=f32)
    zh = jnp.zeros((MLSTM_HEADS,), f32)
    gate_bias = nrm(ks[13], (DEPTH, 4 * MLSTM_HEADS), 0.1) + jnp.concatenate([zh, f_bias, zh, f_bias])[None, :]
    return {
        'x_prompt': jax.random.normal(ks[0], (BATCH, SEQ, D_MODEL), f32),
        'x_sample': jax.random.normal(ks[1], (DEC_BATCH, DEC_SEQ, D_MODEL), f32),
        'norm_mix': gain(ks[2], (DEPTH, D_MODEL)),
        'norm_ffn': gain(ks[3], (DEPTH, D_MODEL)),
        'w_in': nrm(ks[4], (DEPTH, D_MODEL, IN_COLS), D_MODEL ** -0.5),
        'w_out': nrm(ks[5], (DEPTH, MIX_WIDTH, D_MODEL), MIX_WIDTH ** -0.5),
        'mla_q_norm': gain(ks[6], (DEPTH, MLA_Q_LORA)),
        'mla_kv_norm': gain(ks[7], (DEPTH, MLA_KV_LORA)),
        'mla_w_uq': nrm(ks[8], (DEPTH, MLA_Q_LORA, MLA_HEADS * (MLA_NOPE + MLA_ROPE)), MLA_Q_LORA ** -0.5),
        'mla_w_ukv': nrm(ks[9], (DEPTH, MLA_KV_LORA, MLA_HEADS * (MLA_NOPE + MLA_V)), MLA_KV_LORA ** -0.5),
        'diff_lambda': nrm(ks[10], (DEPTH, 4, DIFF_HALF), 0.1),
        'diff_subln': gain(ks[11], (DEPTH, DIFF_V)),
        'mlstm_conv': nrm(ks[12], (DEPTH, MLSTM_CONV, 2 * GROUP_WIDTH), MLSTM_CONV ** -0.5),
        'mlstm_gate_bias': gate_bias,
        'mlstm_norm': gain(ks[14], (DEPTH, MLSTM_HEADS, MLSTM_DH)),
        'na_rpb': nrm(ks[15], (DEPTH, NA_HEADS, 2 * NA_MAX_ROWS - 1, 2 * NA_COLS - 1), 0.05),
        'ffn_w_gate': nrm(ks[16], (N_DENSE, D_MODEL, D_FF), D_MODEL ** -0.5),
        'ffn_w_up': nrm(ks[17], (N_DENSE, D_MODEL, D_FF), D_MODEL ** -0.5),
        'ffn_w_down': nrm(ks[18], (N_DENSE, D_FF, D_MODEL), D_FF ** -0.5),
        'moe_router': nrm(ks[19], (N_MOE, D_MODEL, N_EXPERTS), D_MODEL ** -0.5),
        'moe_w_gate': nrm(ks[20], (N_MOE, N_EXPERTS, D_MODEL, D_EXPERT), D_MODEL ** -0.5),
        'moe_w_up': nrm(ks[21], (N_MOE, N_EXPERTS, D_MODEL, D_EXPERT), D_MODEL ** -0.5),
        'moe_w_down': nrm(ks[22], (N_MOE, N_EXPERTS, D_EXPERT, D_MODEL), D_EXPERT ** -0.5),
        'norm_final': gain(ks[23], (D_MODEL,)),
    }


def reference(x_prompt, x_sample, norm_mix, norm_ffn, w_in, w_out, mla_q_norm, mla_kv_norm,
              mla_w_uq, mla_w_ukv, diff_lambda, diff_subln, mlstm_conv, mlstm_gate_bias, mlstm_norm,
              na_rpb, ffn_w_gate, ffn_w_up, ffn_w_down, moe_router, moe_w_gate, moe_w_up, moe_w_down,
              norm_final):
    p = {
        'norm_mix': norm_mix, 'norm_ffn': norm_ffn, 'w_in': w_in, 'w_out': w_out,
        'mla_q_norm': mla_q_norm, 'mla_kv_norm': mla_kv_norm, 'mla_w_uq': mla_w_uq, 'mla_w_ukv': mla_w_ukv,
        'diff_lambda': diff_lambda, 'diff_subln': diff_subln,
        'mlstm_conv': mlstm_conv, 'mlstm_gate_bias': mlstm_gate_bias, 'mlstm_norm': mlstm_norm,
        'na_rpb': na_rpb,
        'ffn_w_gate': ffn_w_gate, 'ffn_w_up': ffn_w_up, 'ffn_w_down': ffn_w_down,
        'moe_router': moe_router, 'moe_w_gate': moe_w_gate, 'moe_w_up': moe_w_up, 'moe_w_down': moe_w_down,
        'norm_final': norm_final,
    }
    y_prompt = trunk(x_prompt, p)
    y_sample = trunk(x_sample, p)
    return (y_prompt, y_sample)
```

```python
import functools
import math

import numpy as np
import jax
import jax.numpy as jnp
from jax import lax
from jax.experimental import pallas as pl
from jax.experimental.pallas import tpu as pltpu

F32 = jnp.float32
BF16 = jnp.bfloat16

V7X_LANES = 128
V7X_VMEM_LIMIT_BYTES = 56 * 1024 * 1024

EPS = 1e-6
LOG2E = 1.4426950408889634
NEG_BIG = -1e30

HEAD_DIM = 64
N_HEADS = 4
GROUP_WIDTH = 256
MLA_Q_LORA = 192
MLA_KV_LORA = 128
MLA_NOPE = 64
MLA_ROPE = 32
ROPE_THETA = 10000.0
DIFF_HALF = 32
MLSTM_CHUNK = 128
GRID_W = 64
NA_ROWS = 8
NA_COLS = 16
N_EXPERTS = 8
ONES_ROWS = 16
VT_ROWS = HEAD_DIM + ONES_ROWS

IN_SECTIONS = (192, 128, 32, 256, 256, 256, 512, 256, 256, 16, 256, 256, 256)
U_BQ, U_BK, U_BV, U_DQ, U_DK, U_DV, U_CQK, U_CV, U_CO, U_ACQ, U_ACKV, U_AKR, U_AKRR = (
    0, 2, 4, 6, 8, 10, 12, 16, 18, 20, 22, 23, 24)
PROJ_COLS = 25 * V7X_LANES


def _params(*sem):
    return pltpu.CompilerParams(dimension_semantics=sem, vmem_limit_bytes=V7X_VMEM_LIMIT_BYTES)


def _rms(x, n):
    return x * lax.rsqrt(jnp.sum(x * x, axis=-1, keepdims=True) * (1.0 / n) + EPS)


def _inproj_kernel(x_ref, g_ref, w_ref, wg_ref, o_ref, og_ref, *, col_chunk):
    x = x_ref[...]
    hn = (_rms(x, x.shape[-1]) * g_ref[...]).astype(BF16)
    for c in range(0, o_ref.shape[1], col_chunk):
        w = min(col_chunk, o_ref.shape[1] - c)
        o_ref[:, c:c + w] = jnp.dot(hn, w_ref[:, c:c + w], preferred_element_type=F32).astype(BF16)
    og_ref[...] = jnp.dot(hn, wg_ref[...], preferred_element_type=F32)


def _inproj(x, g, w_main, w_gates, bm=512):
    n, d = x.shape
    bm = min(bm, n)
    return pl.pallas_call(
        functools.partial(_inproj_kernel, col_chunk=640),
        grid=(n // bm,),
        in_specs=[pl.BlockSpec((bm, d), lambda i: (i, 0)),
                  pl.BlockSpec((1, d), lambda i: (0, 0)),
                  pl.BlockSpec(w_main.shape, lambda i: (0, 0)),
                  pl.BlockSpec(w_gates.shape, lambda i: (0, 0))],
        out_specs=[pl.BlockSpec((bm, PROJ_COLS), lambda i: (i, 0)),
                   pl.BlockSpec((bm, V7X_LANES), lambda i: (i, 0))],
        out_shape=[jax.ShapeDtypeStruct((n, PROJ_COLS), BF16),
                   jax.ShapeDtypeStruct((n, V7X_LANES), F32)],
        compiler_params=_params("parallel"),
        name="inproj",
    )(x, g.reshape(1, d), w_main, w_gates)


def _prep_w_in(w):
    d = w.shape[0]
    offs = np.cumsum((0,) + IN_SECTIONS)
    (a_cq, a_ckv, a_kr, b_q, b_k, b_v, c_qk, c_v, c_o, c_g, d_q, d_k, d_v) = [
        w[:, offs[i]:offs[i + 1]] for i in range(len(IN_SECTIONS))]
    z = lambda k: jnp.zeros((d, k), w.dtype)
    half = MLA_ROPE // 2
    a_kr_rot = jnp.concatenate([a_kr[:, half:], a_kr[:, :half]], axis=1)
    main = jnp.concatenate([b_q, b_k, b_v, d_q, d_k, d_v, c_qk, c_v, c_o,
                            a_cq, z(64), a_ckv,
                            z(64), a_kr, z(32),
                            z(64), a_kr_rot, z(32)], axis=1)
    gates = jnp.concatenate([c_g, z(V7X_LANES - 16)], axis=1)
    return main.astype(BF16), gates.astype(BF16)


def _mla_prep_kernel(cq_ref, ckv_ref, kr_ref, krr_ref, cos_ref, sin_ref, gq_ref, gkv_ref,
                     wq_ref, wqr_ref, wkn_ref, wvt_ref, ones_ref, q_out, k_out, vt_out, *, q_scale):
    cos = cos_ref[...]
    sin = sin_ref[...]
    qn = (_rms(cq_ref[...].astype(F32), MLA_Q_LORA) * gq_ref[...]).astype(BF16)
    qa = jnp.dot(qn, wq_ref[...], preferred_element_type=F32)
    qr = jnp.dot(qn, wqr_ref[...], preferred_element_type=F32)
    kvn = (_rms(ckv_ref[...].astype(F32), MLA_KV_LORA) * gkv_ref[...]).astype(BF16)
    kn = jnp.dot(kvn, wkn_ref[...], preferred_element_type=F32)
    k_rope = kr_ref[...].astype(F32) * cos + krr_ref[...].astype(F32) * sin
    for h in range(N_HEADS):
        sl = slice(h * V7X_LANES, (h + 1) * V7X_LANES)
        q_out[0, h] = ((qa[:, sl] * cos + qr[:, sl] * sin) * q_scale).astype(BF16)
        k_out[0, h] = (kn[:, sl] + k_rope).astype(BF16)
        vt = lax.dot_general(wvt_ref[h], kvn, (((1,), (1,)), ((), ())), preferred_element_type=F32)
        vt_out[0, h] = (vt + ones_ref[...]).astype(BF16)


def _mla_prep(proj, b, s, cos_t, sin_t, gq, gkv, wq, wqr, wkn, wvt, bm=512):
    bm = min(bm, s)
    nt = s // bm
    row = lambda bi, si: bi * nt + si
    ones_col = jnp.concatenate([jnp.zeros((HEAD_DIM, 1), F32), jnp.ones((ONES_ROWS, 1), F32)], axis=0)
    full = lambda a: pl.BlockSpec(a.shape, lambda bi, si: (0,) * a.ndim)
    return pl.pallas_call(
        functools.partial(_mla_prep_kernel, q_scale=LOG2E * (MLA_NOPE + MLA_ROPE) ** -0.5),
        grid=(b, nt),
        in_specs=[pl.BlockSpec((bm, 2 * V7X_LANES), lambda bi, si: (row(bi, si), U_ACQ // 2)),
                  pl.BlockSpec((bm, V7X_LANES), lambda bi, si: (row(bi, si), U_ACKV)),
                  pl.BlockSpec((bm, V7X_LANES), lambda bi, si: (row(bi, si), U_AKR)),
                  pl.BlockSpec((bm, V7X_LANES), lambda bi, si: (row(bi, si), U_AKRR)),
                  pl.BlockSpec((bm, V7X_LANES), lambda bi, si: (si, 0)),
                  pl.BlockSpec((bm, V7X_LANES), lambda bi, si: (si, 0)),
                  full(gq), full(gkv), full(wq), full(wqr), full(wkn), full(wvt), full(ones_col)],
        out_specs=[pl.BlockSpec((1, N_HEADS, bm, V7X_LANES), lambda bi, si: (bi, 0, si, 0)),
                   pl.BlockSpec((1, N_HEADS, bm, V7X_LANES), lambda bi, si: (bi, 0, si, 0)),
                   pl.BlockSpec((1, N_HEADS, VT_ROWS, bm), lambda bi, si: (bi, 0, 0, si))],
        out_shape=[jax.ShapeDtypeStruct((b, N_HEADS, s, V7X_LANES), BF16),
                   jax.ShapeDtypeStruct((b, N_HEADS, s, V7X_LANES), BF16),
                   jax.ShapeDtypeStruct((b, N_HEADS, VT_ROWS, s), BF16)],
        compiler_params=_params("parallel", "parallel"),
        name="mla_prep",
    )(proj, proj, proj, proj, cos_t, sin_t, gq, gkv, wq, wqr, wkn, wvt, ones_col)


def _prep_mla_weights(q_norm, kv_norm, w_uq, w_ukv):
    half = MLA_ROPE // 2
    dq = MLA_NOPE + MLA_ROPE
    wq_h = w_uq.reshape(MLA_Q_LORA, N_HEADS, dq)
    zq = lambda k: jnp.zeros((MLA_Q_LORA, N_HEADS, k), w_uq.dtype)
    rope_cols = wq_h[:, :, MLA_NOPE:]
    rope_rot = jnp.concatenate([rope_cols[:, :, half:], rope_cols[:, :, :half]], axis=2)
    wq = jnp.concatenate([wq_h, zq(V7X_LANES - dq)], axis=2).reshape(MLA_Q_LORA, N_HEADS * V7X_LANES)
    wqr = jnp.concatenate([zq(MLA_NOPE), rope_rot, zq(V7X_LANES - dq)], axis=2).reshape(
        MLA_Q_LORA, N_HEADS * V7X_LANES)
    pad_rows = jnp.zeros((2 * V7X_LANES - MLA_Q_LORA, N_HEADS * V7X_LANES), w_uq.dtype)
    wq = jnp.concatenate([wq, pad_rows], axis=0).astype(BF16)
    wqr = jnp.concatenate([wqr, pad_rows], axis=0).astype(BF16)
    wkv_h = w_ukv.reshape(MLA_KV_LORA, N_HEADS, MLA_NOPE + HEAD_DIM)
    wkn = jnp.concatenate([wkv_h[:, :, :MLA_NOPE],
                           jnp.zeros((MLA_KV_LORA, N_HEADS, V7X_LANES - MLA_NOPE), w_ukv.dtype)],
                          axis=2).reshape(MLA_KV_LORA, N_HEADS * V7X_LANES).astype(BF16)
    wvt = jnp.transpose(wkv_h[:, :, MLA_NOPE:], (1, 2, 0))
    wvt = jnp.concatenate([wvt, jnp.zeros((N_HEADS, ONES_ROWS, MLA_KV_LORA), w_ukv.dtype)],
                          axis=1).astype(BF16)
    gq = jnp.concatenate([q_norm, jnp.zeros((2 * V7X_LANES - MLA_Q_LORA,), q_norm.dtype)]).reshape(1, -1)
    gkv = kv_norm.reshape(1, -1)
    return gq, gkv, wq, wqr, wkn, wvt


def _rope_tables(s):
    half = MLA_ROPE // 2
    inv = ROPE_THETA ** (-jnp.arange(half, dtype=F32) / half)
    ang = jnp.arange(s).astype(F32)[:, None] * inv[None, :]
    cos, sin = jnp.cos(ang), jnp.sin(ang)
    ones = jnp.ones((s, MLA_NOPE), F32)
    z = lambda k: jnp.zeros((s, k), F32)
    pad = V7X_LANES - MLA_NOPE - MLA_ROPE
    cos_t = jnp.concatenate([ones, cos, cos, z(pad)], axis=1)
    sin_t = jnp.concatenate([z(MLA_NOPE), -sin, sin, z(pad)], axis=1)
    return cos_t, sin_t


def _flash_step(k_blk, q, vt_blk, m, acc, bias=None):
    st = lax.dot_general(k_blk, q, (((1,), (1,)), ((), ())), preferred_element_type=F32)
    if bias is not None:
        st = st + bias
    m_new = jnp.maximum(m, jnp.max(st, axis=0, keepdims=True))
    alpha = jnp.exp2(m - m_new)
    p = jnp.exp2((st - m_new).astype(BF16))
    acc = acc * alpha + jnp.dot(vt_blk, p, preferred_element_type=F32)
    return m_new, acc


def _mla_flash_kernel(q_ref, k_ref, vt_ref, o_ref, *, bk):
    q = q_ref[0, 0]
    bq = q.shape[0]
    nk = k_ref.shape[2] // bk

    def body(i, carry):
        m, acc = carry
        k0 = pl.multiple_of(i * bk, bk)
        return _flash_step(k_ref[0, 0, pl.ds(k0, bk), :], q, vt_ref[0, 0, :, pl.ds(k0, bk)], m, acc)

    m0 = jnp.full((1, bq), NEG_BIG, F32)
    acc0 = jnp.zeros((VT_ROWS, bq), F32)
    _, acc = lax.fori_loop(0, nk, body, (m0, acc0))
    o_ref[0] = (acc[:HEAD_DIM] / acc[HEAD_DIM:HEAD_DIM + 1]).astype(o_ref.dtype)


def _mla_flash(q, k, vt, bq=256, bk=512):
    b, h, s, dq = q.shape
    bq, bk = min(bq, s), min(bk, s)
    return pl.pallas_call(
        functools.partial(_mla_flash_kernel, bk=bk),
        grid=(b, h, s // bq),
        in_specs=[pl.BlockSpec((1, 1, bq, dq), lambda bi, hi, qi: (bi, hi, qi, 0)),
                  pl.BlockSpec((1, 1, s, dq), lambda bi, hi, qi: (bi, hi, 0, 0)),
                  pl.BlockSpec((1, 1, VT_ROWS, s), lambda bi, hi, qi: (bi, hi, 0, 0))],
        out_specs=pl.BlockSpec((1, HEAD_DIM, bq), lambda bi, hi, qi: (bi, hi, qi)),
        out_shape=jax.ShapeDtypeStruct((b, h * HEAD_DIM, s), BF16),
        compiler_params=_params("parallel", "parallel", "arbitrary"),
        name="mla_flash",
    )(q, k, vt)


def _diff_flash_kernel(sc_ref, q_ref, k_ref, vt_ref, g_ref, o_ref, *, bk, q_scale, out_scale):
    hi = pl.program_id(1)
    qi = pl.program_id(2)
    slope = sc_ref[hi]
    lam = sc_ref[N_HEADS]
    qf = q_ref[0, 0].astype(F32) * q_scale
    bq = qf.shape[0]
    lane = lax.broadcasted_iota(jnp.int32, qf.shape, 1)
    q1 = jnp.where(lane < DIFF_HALF, qf, 0.0).astype(BF16)
    q2 = jnp.where(lane >= DIFF_HALF, qf, 0.0).astype(BF16)
    nk = k_ref.shape[2] // bk
    rel = (lax.broadcasted_iota(jnp.int32, (bk, bq), 0)
           - lax.broadcasted_iota(jnp.int32, (bk, bq), 1)).astype(F32)
    q0 = (qi * bq).astype(F32)

    def body(i, carry):
        m1, a1, m2, a2 = carry
        k0 = pl.multiple_of(i * bk, bk)
        k_blk = k_ref[0, 0, pl.ds(k0, bk), :]
        vt_blk = vt_ref[0, 0, :, pl.ds(k0, bk)]
        bias = jnp.abs(rel + (k0.astype(F32) - q0)) * slope
        m1, a1 = _flash_step(k_blk, q1, vt_blk, m1, a1, bias)
        m2, a2 = _flash_step(k_blk, q2, vt_blk, m2, a2, bias)
        return m1, a1, m2, a2

    m0 = jnp.full((1, bq), NEG_BIG, F32)
    acc0 = jnp.zeros((VT_ROWS, bq), F32)
    _, a1, _, a2 = lax.fori_loop(0, nk, body, (m0, acc0, m0, acc0))
    o = a1[:HEAD_DIM] / a1[HEAD_DIM:HEAD_DIM + 1] - lam * (a2[:HEAD_DIM] / a2[HEAD_DIM:HEAD_DIM + 1])
    ms = jnp.sum(o * o, axis=0, keepdims=True) * (1.0 / HEAD_DIM)
    o_ref[0] = (o * lax.rsqrt(ms + EPS) * g_ref[...] * out_scale).astype(o_ref.dtype)


def _diff_flash(scalars, q, k, vt, subln, layer_idx, bq=256, bk=512):
    b, h, s, d = q.shape
    bq, bk = min(bq, s), min(bk, s)
    lam_init = 0.8 - 0.6 * math.exp(-0.3 * layer_idx)
    return pl.pallas_call(
        functools.partial(_diff_flash_kernel, bk=bk, q_scale=LOG2E * DIFF_HALF ** -0.5,
                          out_scale=1.0 - lam_init),
        grid=(b, h, s // bq),
        in_specs=[pl.BlockSpec(memory_space=pltpu.SMEM),
                  pl.BlockSpec((1, 1, bq, d), lambda bi, hi, qi: (bi, hi, qi, 0)),
                  pl.BlockSpec((1, 1, s, d), lambda bi, hi, qi: (bi, hi, 0, 0)),
                  pl.BlockSpec((1, 1, VT_ROWS, s), lambda bi, hi, qi: (bi, hi, 0, 0)),
                  pl.BlockSpec((HEAD_DIM, 1), lambda bi, hi, qi: (0, 0))],
        out_specs=pl.BlockSpec((1, HEAD_DIM, bq), lambda bi, hi, qi: (bi, hi, qi)),
        out_shape=jax.ShapeDtypeStruct((b, h * HEAD_DIM, s), BF16),
        compiler_params=_params("parallel", "parallel", "arbitrary"),
        name="diff_flash",
    )(scalars, q, k, vt, subln.reshape(HEAD_DIM, 1))


def _conv_silu_kernel(x_ref, prev_ref, next_ref, w_ref, o_ref, *, n_tiles):
    si = pl.program_id(1)
    x = x_ref[...]
    bm = x.shape[0]
    r = lax.broadcasted_iota(jnp.int32, (bm, bm), 0)
    c = lax.broadcasted_iota(jnp.int32, (bm, bm), 1)
    shift_dn = jnp.where(r == c + 1, 1.0, 0.0).astype(BF16)
    shift_up = jnp.where(r + 1 == c, 1.0, 0.0).astype(BF16)
    x_prev = jnp.dot(shift_dn, x, preferred_element_type=F32)
    x_next = jnp.dot(shift_up, x, preferred_element_type=F32)
    row = lax.broadcasted_iota(jnp.int32, x.shape, 0)
    halo_prev = jnp.where(si > 0, prev_ref[7:8, :].astype(F32), 0.0)
    halo_next = jnp.where(si < n_tiles - 1, next_ref[0:1, :].astype(F32), 0.0)
    x_prev = jnp.where(row == 0, halo_prev, x_prev)
    x_next = jnp.where(row == bm - 1, halo_next, x_next)
    w = w_ref[...]
    y = x_prev * w[0:1] + x.astype(F32) * w[1:2] + x_next * w[2:3]
    y = y * jax.nn.sigmoid(y)
    col = lax.broadcasted_iota(jnp.int32, x.shape, 1)
    o_ref[...] = jnp.where(col >= GROUP_WIDTH, y * (HEAD_DIM ** -0.5), y).astype(o_ref.dtype)


def _conv_silu(proj, b, s, conv_w, bm=256):
    bm = min(bm, s)
    nt = s // bm
    c = 2 * GROUP_WIDTH
    cb = U_CQK * V7X_LANES // c
    hb = bm // 8
    n8 = b * s // 8
    return pl.pallas_call(
        functools.partial(_conv_silu_kernel, n_tiles=nt),
        grid=(b, nt),
        in_specs=[pl.BlockSpec((bm, c), lambda bi, si: (bi * nt + si, cb)),
                  pl.BlockSpec((8, c), lambda bi, si: (jnp.maximum((bi * nt + si) * hb - 1, 0), cb)),
                  pl.BlockSpec((8, c), lambda bi, si: (jnp.minimum((bi * nt + si + 1) * hb, n8 - 1), cb)),
                  pl.BlockSpec((8, c), lambda bi, si: (0, 0))],
        out_specs=pl.BlockSpec((bm, c), lambda bi, si: (bi * nt + si, 0)),
        out_shape=jax.ShapeDtypeStruct((b * s, c), BF16),
        compiler_params=_params("parallel", "parallel"),
        name="mlstm_conv",
    )(proj, proj, proj, jnp.concatenate([conv_w, jnp.zeros((5, c), conv_w.dtype)], axis=0))


def _log_sigmoid(x):
    return jnp.minimum(x, 0.0) - jnp.log(1.0 + jnp.exp(-jnp.abs(x)))


def _mlstm_chunk(qc, kc, vt1, gcol, grow, caug, m, *, backward):
    L = qc.shape[0]
    d = HEAD_DIM
    li_c, lf_c = gcol[:, 0:1], _log_sigmoid(gcol[:, 1:2])
    li_r, lf_r = grow[0:1, :], _log_sigmoid(grow[1:2, :])
    s_i = lax.broadcasted_iota(jnp.int32, (L, L), 0)
    j_i = lax.broadcasted_iota(jnp.int32, (L, L), 1)
    if backward:
        a_mask, b_mat, valid = j_i <= s_i, s_i >= j_i, s_i >= j_i
        bcum_row_idx, last_lane = L - 1, 0
    else:
        a_mask, b_mat, valid = j_i >= s_i, s_i <= j_i, s_i <= j_i
        bcum_row_idx, last_lane = 0, L - 1
    a = jnp.where(a_mask, lf_r, 0.0)
    a_hi = a.astype(BF16)
    a_lo = (a - a_hi.astype(F32)).astype(BF16)
    ones_b = jnp.where(b_mat, 1.0, 0.0).astype(BF16)
    e = jnp.dot(a_hi, ones_b, preferred_element_type=F32) + jnp.dot(a_lo, ones_b, preferred_element_type=F32)
    bcum_r = e[bcum_row_idx:bcum_row_idx + 1, :]
    dlog = jnp.where(valid, e + (li_c - lf_c), NEG_BIG)
    inter = bcum_r + m
    m_row = jnp.maximum(inter, jnp.max(dlog, axis=0, keepdims=True))
    w_intra = jnp.exp(dlog - m_row)
    w_inter = jnp.exp(inter - m_row)
    st = lax.dot_general(kc, qc, (((1,), (1,)), ((), ())), preferred_element_type=F32)
    at = (st * w_intra).astype(BF16)
    pv = jnp.dot(vt1, at, preferred_element_type=F32)
    cq = lax.dot_general(caug.astype(BF16), qc, (((1,), (1,)), ((), ())), preferred_element_type=F32)
    num = w_inter * cq[:d] + pv[:d]
    den = w_inter * cq[d:d + 1] + pv[d:d + 1]
    h = num / jnp.maximum(jnp.abs(den), jnp.exp(-m_row))
    b_last = bcum_r[:, last_lane:last_lane + 1]
    logw_end = b_last - bcum_r + li_r
    m_new = jnp.maximum(b_last + m, jnp.max(logw_end, axis=1, keepdims=True))
    w_end = jnp.exp(logw_end - m_new)
    decay = jnp.exp(b_last + m - m_new)
    u = jnp.dot((vt1.astype(F32) * w_end).astype(BF16), kc, preferred_element_type=F32)
    caug = decay * caug + u[:d + 8]
    return h, caug, m_new


def _mlstm_kernel(q_ref, k_ref, vt_ref, gc_ref, gr_ref, op_ref, ng_ref, o_ref, hf_sc, hb_sc):
    L = MLSTM_CHUNK
    d = HEAD_DIM
    s = q_ref.shape[2]
    nc = s // L
    ones = jnp.ones((ONES_ROWS, L), BF16)

    def chunk(c, caug, m, backward):
        t0 = pl.multiple_of(c * L, L)
        vt1 = jnp.concatenate([vt_ref[0, :, pl.ds(t0, L)], ones], axis=0)
        gcol = gc_ref[0, 0, pl.ds(t0, L), :]
        grow = gr_ref[0, 0, :, pl.ds(t0, L)]
        if backward:
            gcol, grow = gcol[:, 2:4], grow[2:4, :]
        else:
            gcol, grow = gcol[:, 0:2], grow[0:2, :]
        return _mlstm_chunk(q_ref[0, 0, pl.ds(t0, L), :], k_ref[0, 0, pl.ds(t0, L), :], vt1, gcol, grow,
                            caug, m, backward=backward), t0

    def body(i, carry):
        cf, mf, cb, mb = carry
        (hf, cf, mf), tf = chunk(i, cf, mf, False)
        hf_sc[:, pl.ds(tf, L)] = hf
        (hb, cb, mb), tb = chunk(nc - 1 - i, cb, mb, True)
        hb_sc[:, pl.ds(tb, L)] = hb
        return cf, mf, cb, mb

    c0 = jnp.zeros((d + 8, d), F32)
    m0 = jnp.zeros((1, 1), F32)
    lax.fori_loop(0, nc, body, (c0, m0, c0, m0))
    hsum = hf_sc[...] + hb_sc[...]
    ms = jnp.sum(hsum * hsum, axis=0, keepdims=True) * (1.0 / d)
    hn = hsum * lax.rsqrt(ms + EPS) * ng_ref[0]
    o_ref[0] = (hn * jax.nn.sigmoid(op_ref[0].astype(F32))).astype(o_ref.dtype)


def _mlstm(q, k, vt, gcol, grow, opre_t, norm_g):
    b, h, s, d = q.shape
    return pl.pallas_call(
        _mlstm_kernel,
        grid=(b, h),
        in_specs=[pl.BlockSpec((1, 1, s, d), lambda bi, hi: (bi, hi, 0, 0)),
                  pl.BlockSpec((1, 1, s, d), lambda bi, hi: (bi, hi, 0, 0)),
                  pl.BlockSpec((1, d, s), lambda bi, hi: (bi, hi, 0)),
                  pl.BlockSpec((1, 1, s, 4), lambda bi, hi: (bi, hi, 0, 0)),
                  pl.BlockSpec((1, 1, 4, s), lambda bi, hi: (bi, hi, 0, 0)),
                  pl.BlockSpec((1, d, s), lambda bi, hi: (bi, hi, 0)),
                  pl.BlockSpec((1, d, 1), lambda bi, hi: (hi, 0, 0))],
        out_specs=pl.BlockSpec((1, d, s), lambda bi, hi: (bi, hi, 0)),
        out_shape=jax.ShapeDtypeStruct((b, h * d, s), BF16),
        scratch_shapes=[pltpu.VMEM((d, s), F32), pltpu.VMEM((d, s), F32)],
        compiler_params=_params("parallel", "parallel"),
        name="mlstm_scan",
    )(q, k, vt, gcol, grow, opre_t, norm_g.reshape(h, d, 1))


def _na_kernel(q_ref, k_ref, v_ref, bias_ref, o_ref, *, rows_per_step, n_rows):
    blk = pl.program_id(1)
    win = NA_ROWS * GRID_W
    for i in range(rows_per_step):
        r = blk * rows_per_step + i
        r0 = jnp.clip(r - NA_ROWS // 2, 0, n_rows - NA_ROWS)
        dsel = r - r0
        k0 = pl.multiple_of(r0 * GRID_W, GRID_W)
        for h in range(N_HEADS):
            qh = q_ref[0, h, i * GRID_W:(i + 1) * GRID_W, :]
            kw = k_ref[0, h, pl.ds(k0, win), :]
            vw = v_ref[0, h, pl.ds(k0, win), :]
            sc = lax.dot_general(qh, kw, (((1,), (1,)), ((), ())), preferred_element_type=F32)
            sc = sc * (HEAD_DIM ** -0.5) + bias_ref[h, dsel]
            mx = jnp.max(sc, axis=-1, keepdims=True)
            p = jnp.exp(sc - mx)
            p = (p / jnp.sum(p, axis=-1, keepdims=True)).astype(BF16)
            o_ref[0, h, i * GRID_W:(i + 1) * GRID_W, :] = jnp.dot(
                p, vw, preferred_element_type=F32).astype(o_ref.dtype)


def _na(q, k, v, bias, rows_per_step=8):
    b, h, s, d = q.shape
    n_rows = s // GRID_W
    rows_per_step = min(rows_per_step, n_rows)
    bm = rows_per_step * GRID_W
    return pl.pallas_call(
        functools.partial(_na_kernel, rows_per_step=rows_per_step, n_rows=n_rows),
        grid=(b, n_rows // rows_per_step),
        in_specs=[pl.BlockSpec((1, h, bm, d), lambda bi, ri: (bi, 0, ri, 0)),
                  pl.BlockSpec((1, h, s, d), lambda bi, ri: (bi, 0, 0, 0)),
                  pl.BlockSpec((1, h, s, d), lambda bi, ri: (bi, 0, 0, 0)),
                  pl.BlockSpec(bias.shape, lambda bi, ri: (0, 0, 0, 0))],
        out_specs=pl.BlockSpec((1, h, bm, d), lambda bi, ri: (bi, 0, ri, 0)),
        out_shape=jax.ShapeDtypeStruct((b, h, s, d), BF16),
        compiler_params=_params("parallel", "arbitrary"),
        name="na_attn",
    )(q, k, v, bias)


def _na_bias_table(rpb, n_rows):
    wr = min(NA_ROWS, n_rows)
    cols = np.arange(GRID_W)
    col_start = np.clip(cols - NA_COLS // 2, 0, GRID_W - NA_COLS)
    ck = np.arange(GRID_W)[None, :]
    valid = (ck >= col_start[:, None]) & (ck < col_start[:, None] + NA_COLS)
    cidx = np.clip(ck - cols[:, None] + NA_COLS - 1, 0, 2 * NA_COLS - 2)
    ridx = np.arange(wr)[None, :] - np.arange(wr)[:, None] + NA_ROWS - 1
    t = rpb[:, ridx[:, None, :, None], cidx[None, :, None, :]]
    t = jnp.where(valid[None, None, :, None, :], t, NEG_BIG)
    return t.reshape(rpb.shape[0], wr, GRID_W, wr * GRID_W).astype(F32)


def _outproj_kernel(ya_ref, yb_ref, yc_ref, yd_ref, x_ref, w_ref, o_ref):
    acc = x_ref[...]
    for g, y_ref in enumerate((ya_ref, yb_ref, yc_ref, yd_ref)):
        acc = acc + lax.dot_general(y_ref[0], w_ref[g], (((0,), (0,)), ((), ())),
                                    preferred_element_type=F32)
    o_ref[...] = acc


def _outproj(ya, yb, yc, yd, x, w, b, s, bm=512):
    bm = min(bm, s)
    nt = s // bm
    d = x.shape[1]
    yspec = pl.BlockSpec((1, GROUP_WIDTH, bm), lambda bi, si: (bi, 0, si))
    return pl.pallas_call(
        _outproj_kernel,
        grid=(b, nt),
        in_specs=[yspec, yspec, yspec, yspec,
                  pl.BlockSpec((bm, d), lambda bi, si: (bi * nt + si, 0)),
                  pl.BlockSpec(w.shape, lambda bi, si: (0, 0, 0))],
        out_specs=pl.BlockSpec((bm, d), lambda bi, si: (bi * nt + si, 0)),
        out_shape=jax.ShapeDtypeStruct(x.shape, F32),
        compiler_params=_params("parallel", "parallel"),
        name="outproj",
    )(ya, yb, yc, yd, x, w)


def _ffn_kernel(x_ref, g_ref, wg_ref, wu_ref, wd_ref, o_ref, hn_sc, acc_sc):
    f = pl.program_id(1)

    @pl.when(f == 0)
    def _():
        x = x_ref[...]
        hn_sc[...] = (_rms(x, x.shape[-1]) * g_ref[...]).astype(BF16)
        acc_sc[...] = jnp.zeros_like(acc_sc)

    hn = hn_sc[...]
    gate = jnp.dot(hn, wg_ref[...], preferred_element_type=F32)
    up = jnp.dot(hn, wu_ref[...], preferred_element_type=F32)
    act = (gate * jax.nn.sigmoid(gate) * up).astype(BF16)
    acc_sc[...] += jnp.dot(act, wd_ref[...], preferred_element_type=F32)

    @pl.when(f == pl.num_programs(1) - 1)
    def _():
        o_ref[...] = x_ref[...] + acc_sc[...]


def _ffn(x, g, wg, wu, wd, bm=512, bf=1408):
    n, d = x.shape
    ff = wg.shape[1]
    bm = min(bm, n)
    return pl.pallas_call(
        _ffn_kernel,
        grid=(n // bm, ff // bf),
        in_specs=[pl.BlockSpec((bm, d), lambda i, f: (i, 0)),
                  pl.BlockSpec((1, d), lambda i, f: (0, 0)),
                  pl.BlockSpec((d, bf), lambda i, f: (0, f)),
                  pl.BlockSpec((d, bf), lambda i, f: (0, f)),
                  pl.BlockSpec((bf, d), lambda i, f: (f, 0))],
        out_specs=pl.BlockSpec((bm, d), lambda i, f: (i, 0)),
        out_shape=jax.ShapeDtypeStruct(x.shape, F32),
        scratch_shapes=[pltpu.VMEM((bm, d), BF16), pltpu.VMEM((bm, d), F32)],
        compiler_params=_params("parallel", "arbitrary"),
        name="ffn",
    )(x, g.reshape(1, d), wg, wu, wd)


def _router_kernel(x_ref, g_ref, wr_ref, hn_ref, cw_ref):
    x = x_ref[...]
    hn = _rms(x, x.shape[-1]) * g_ref[...]
    hn_ref[...] = hn.astype(BF16)
    logits = jnp.dot(hn, wr_ref[...], preferred_element_type=F32, precision=lax.Precision.HIGHEST)
    lane = lax.broadcasted_iota(jnp.int32, logits.shape, 1)
    logits = jnp.where(lane < N_EXPERTS, logits, NEG_BIG)
    m1 = jnp.max(logits, axis=-1, keepdims=True)
    i1 = jnp.min(jnp.where(logits == m1, lane, V7X_LANES), axis=-1, keepdims=True)
    rest = jnp.where(lane == i1, NEG_BIG, logits)
    m2 = jnp.max(rest, axis=-1, keepdims=True)
    i2 = jnp.min(jnp.where(rest == m2, lane, V7X_LANES), axis=-1, keepdims=True)
    e2 = jnp.exp(m2 - m1)
    w1 = 1.0 / (1.0 + e2)
    w2 = e2 / (1.0 + e2)
    cw_ref[...] = jnp.where(lane == i1, w1, 0.0) + jnp.where(lane == i2, w2, 0.0)


def _router(x, g, w_router, bm=512):
    n, d = x.shape
    bm = min(bm, n)
    wr = jnp.concatenate([w_router, jnp.zeros((d, V7X_LANES - N_EXPERTS), w_router.dtype)], axis=1)
    return pl.pallas_call(
        _router_kernel,
        grid=(n // bm,),
        in_specs=[pl.BlockSpec((bm, d), lambda i: (i, 0)),
                  pl.BlockSpec((1, d), lambda i: (0, 0)),
                  pl.BlockSpec((d, V7X_LANES), lambda i: (0, 0))],
        out_specs=[pl.BlockSpec((bm, d), lambda i: (i, 0)),
                   pl.BlockSpec((bm, V7X_LANES), lambda i: (i, 0))],
        out_shape=[jax.ShapeDtypeStruct((n, d), BF16),
                   jax.ShapeDtypeStruct((n, V7X_LANES), F32)],
        compiler_params=_params("parallel"),
        name="moe_router",
    )(x, g.reshape(1, d), wr)


def _moe_kernel(x_ref, hn_ref, cw_ref, wg_ref, wu_ref, wd_ref, o_ref, acc_sc):
    e = pl.program_id(1)
    f = pl.program_id(2)

    @pl.when((e == 0) & (f == 0))
    def _():
        acc_sc[...] = jnp.zeros_like(acc_sc)

    hn = hn_ref[...]
    gate = jnp.dot(hn, wg_ref[0], preferred_element_type=F32)
    up = jnp.dot(hn, wu_ref[0], preferred_element_type=F32)
    act = (gate * jax.nn.sigmoid(gate) * up).astype(BF16)
    cw = cw_ref[...]
    lane = lax.broadcasted_iota(jnp.int32, cw.shape, 1)
    ce = jnp.sum(jnp.where(lane == e, cw, 0.0), axis=-1, keepdims=True)
    acc_sc[...] += ce * jnp.dot(act, wd_ref[0], preferred_element_type=F32)

    @pl.when((e == pl.num_programs(1) - 1) & (f == pl.num_programs(2) - 1))
    def _():
        o_ref[...] = x_ref[...] + acc_sc[...]


def _moe(x, hn, cw, wg, wu, wd, bm=512, bf=896):
    n, d = x.shape
    ne, _, ff = wg.shape
    bm = min(bm, n)
    return pl.pallas_call(
        _moe_kernel,
        grid=(n // bm, ne, ff // bf),
        in_specs=[pl.BlockSpec((bm, d), lambda i, e, f: (i, 0)),
                  pl.BlockSpec((bm, d), lambda i, e, f: (i, 0)),
                  pl.BlockSpec((bm, V7X_LANES), lambda i, e, f: (i, 0)),
                  pl.BlockSpec((1, d, bf), lambda i, e, f: (e, 0, f)),
                  pl.BlockSpec((1, d, bf), lambda i, e, f: (e, 0, f)),
                  pl.BlockSpec((1, bf, d), lambda i, e, f: (e, f, 0))],
        out_specs=pl.BlockSpec((bm, d), lambda i, e, f: (i, 0)),
        out_shape=jax.ShapeDtypeStruct(x.shape, F32),
        scratch_shapes=[pltpu.VMEM((bm, d), F32)],
        compiler_params=_params("parallel", "arbitrary", "arbitrary"),
        name="moe_ffn",
    )(x, hn, cw, wg, wu, wd)


def _final_norm_kernel(x_ref, g_ref, o_ref):
    x = x_ref[...]
    o_ref[...] = _rms(x, x.shape[-1]) * g_ref[...]


def _final_norm(x, g, bm=1024):
    n, d = x.shape
    bm = min(bm, n)
    return pl.pallas_call(
        _final_norm_kernel,
        grid=(n // bm,),
        in_specs=[pl.BlockSpec((bm, d), lambda i: (i, 0)), pl.BlockSpec((1, d), lambda i: (0, 0))],
        out_specs=pl.BlockSpec((bm, d), lambda i: (i, 0)),
        out_shape=jax.ShapeDtypeStruct(x.shape, F32),
        compiler_params=_params("parallel"),
        name="final_norm",
    )(x, g.reshape(1, d))


def _heads(t, b, s):
    return t.reshape(b, s, N_HEADS, HEAD_DIM).transpose(0, 2, 1, 3)


def _chan_major(t, b, s):
    return t.reshape(b, s, t.shape[-1]).transpose(0, 2, 1)


def _with_ones(vt):
    b, _, s = vt.shape
    vt = vt.reshape(b, N_HEADS, HEAD_DIM, s)
    return jnp.concatenate([vt, jnp.ones((b, N_HEADS, ONES_ROWS, s), vt.dtype)], axis=2)


def _cols(proj, unit, n_units):
    return proj[:, unit * V7X_LANES:(unit + n_units) * V7X_LANES]


def _token_mix(x, i, b, s, p):
    w_main, w_gates = _prep_w_in(p['w_in'][i])
    proj, gates = _inproj(x, p['norm_mix'][i], w_main, w_gates)

    cos_t, sin_t = _rope_tables(s)
    q_a, k_a, vt_a = _mla_prep(proj, b, s, cos_t, sin_t, *_prep_mla_weights(
        p['mla_q_norm'][i], p['mla_kv_norm'][i], p['mla_w_uq'][i], p['mla_w_ukv'][i]))
    y_a = _mla_flash(q_a, k_a, vt_a)

    lp = p['diff_lambda'][i].astype(F32)
    lam_init = 0.8 - 0.6 * math.exp(-0.3 * i)
    lam = jnp.exp(jnp.sum(lp[0] * lp[1])) - jnp.exp(jnp.sum(lp[2] * lp[3])) + lam_init
    slopes = 2.0 ** (-8.0 * jnp.arange(1, N_HEADS + 1, dtype=F32) / N_HEADS)
    scalars = jnp.concatenate([-slopes * LOG2E, lam[None]]).astype(F32)
    y_b = _diff_flash(scalars, _heads(_cols(proj, U_BQ, 2), b, s), _heads(_cols(proj, U_BK, 2), b, s),
                      _with_ones(_chan_major(_cols(proj, U_BV, 2), b, s)), p['diff_subln'][i], i)

    qk = _conv_silu(proj, b, s, p['mlstm_conv'][i])
    g = (gates[:, :4 * N_HEADS] + p['mlstm_gate_bias'][i][None, :]).reshape(b, s, 4, N_HEADS)
    y_c = _mlstm(_heads(qk[:, :GROUP_WIDTH], b, s), _heads(qk[:, GROUP_WIDTH:], b, s),
                 _chan_major(_cols(proj, U_CV, 2), b, s),
                 g.transpose(0, 3, 1, 2), g.transpose(0, 3, 2, 1),
                 _chan_major(_cols(proj, U_CO, 2), b, s), p['mlstm_norm'][i])

    y_d = _na(_heads(_cols(proj, U_DQ, 2), b, s), _heads(_cols(proj, U_DK, 2), b, s),
              _heads(_cols(proj, U_DV, 2), b, s), _na_bias_table(p['na_rpb'][i], s // GRID_W))
    y_d = y_d.transpose(0, 1, 3, 2).reshape(b, GROUP_WIDTH, s)

    w_out = p['w_out'][i].reshape(4, GROUP_WIDTH, -1).astype(BF16)
    return _outproj(y_a, y_b, y_c, y_d, x, w_out, b, s)


def _trunk(x, p, depth):
    b, s, d = x.shape
    x = x.reshape(b * s, d)
    for i in range(depth):
        x = _token_mix(x, i, b, s, p)
        j = i // 2
        if i % 2 == 0:
            x = _ffn(x, p['norm_ffn'][i], p['ffn_w_gate'][j].astype(BF16), p['ffn_w_up'][j].astype(BF16),
                     p['ffn_w_down'][j].astype(BF16))
        else:
            hn, cw = _router(x, p['norm_ffn'][i], p['moe_router'][j])
            x = _moe(x, hn, cw, p['moe_w_gate'][j].astype(BF16), p['moe_w_up'][j].astype(BF16),
                     p['moe_w_down'][j].astype(BF16))
    return _final_norm(x, p['norm_final']).reshape(b, s, d)


def kernel(x_prompt, x_sample, norm_mix, norm_ffn, w_in, w_out, mla_q_norm, mla_kv_norm, mla_w_uq, mla_w_ukv,
           diff_lambda, diff_subln, mlstm_conv, mlstm_gate_bias, mlstm_norm, na_rpb, ffn_w_gate, ffn_w_up,
           ffn_w_down, moe_router, moe_w_gate, moe_w_up, moe_w_down, norm_final):
    p = dict(norm_mix=norm_mix, norm_ffn=norm_ffn, w_in=w_in, w_out=w_out, mla_q_norm=mla_q_norm,
             mla_kv_norm=mla_kv_norm, mla_w_uq=mla_w_uq, mla_w_ukv=mla_w_ukv, diff_lambda=diff_lambda,
             diff_subln=diff_subln, mlstm_conv=mlstm_conv, mlstm_gate_bias=mlstm_gate_bias,
             mlstm_norm=mlstm_norm, na_rpb=na_rpb, ffn_w_gate=ffn_w_gate, ffn_w_up=ffn_w_up,
             ffn_w_down=ffn_w_down, moe_router=moe_router, moe_w_gate=moe_w_gate, moe_w_up=moe_w_up,
             moe_w_down=moe_w_down, norm_final=norm_final)
    depth = norm_mix.shape[0]
    nb = x_prompt.shape[0]
    y = _trunk(jnp.concatenate([x_prompt, x_sample], axis=0), p, depth)
    return (y[:nb], y[nb:])
```

```python
import functools
import math

import numpy as np
import jax
import jax.numpy as jnp
from jax import lax
from jax.experimental import pallas as pl
from jax.experimental.pallas import tpu as pltpu

F32 = jnp.float32
BF16 = jnp.bfloat16

V7X_LANES = 128
V7X_VMEM_LIMIT_BYTES = 56 * 1024 * 1024

EPS = 1e-6
LOG2E = 1.4426950408889634
NEG_BIG = -1e30

HEAD_DIM = 64
N_HEADS = 4
GROUP_WIDTH = 256
MLA_Q_LORA = 192
MLA_KV_LORA = 128
MLA_NOPE = 64
MLA_ROPE = 32
ROPE_THETA = 10000.0
DIFF_HALF = 32
MLSTM_CHUNK = 128
GRID_W = 64
NA_ROWS = 8
NA_COLS = 16
N_EXPERTS = 8
ONES_ROWS = 16
VT_ROWS = HEAD_DIM + ONES_ROWS
MLA_REF_LANE = MLA_NOPE + MLA_ROPE
DIFF_REF_LANE = HEAD_DIM

IN_SECTIONS = (192, 128, 32, 256, 256, 256, 512, 256, 256, 16, 256, 256, 256)
U_BQ, U_BK, U_BV, U_DQ, U_DK, U_DV, U_CQK, U_CV, U_CO, U_ACQ, U_ACKV, U_AKR, U_AKRR = (
    0, 2, 4, 6, 8, 10, 12, 16, 18, 20, 22, 23, 24)
PROJ_COLS = 25 * V7X_LANES


def _params(*sem):
    return pltpu.CompilerParams(dimension_semantics=sem, vmem_limit_bytes=V7X_VMEM_LIMIT_BYTES)


def _rms(x, n):
    return x * lax.rsqrt(jnp.sum(x * x, axis=-1, keepdims=True) * (1.0 / n) + EPS)


def _inproj_kernel(x_ref, g_ref, w_ref, wg_ref, o_ref, og_ref, *, col_chunk):
    x = x_ref[...]
    hn = (_rms(x, x.shape[-1]) * g_ref[...]).astype(BF16)
    for c in range(0, o_ref.shape[1], col_chunk):
        w = min(col_chunk, o_ref.shape[1] - c)
        o_ref[:, c:c + w] = jnp.dot(hn, w_ref[:, c:c + w], preferred_element_type=F32).astype(BF16)
    og_ref[...] = jnp.dot(hn, wg_ref[...], preferred_element_type=F32)


def _inproj(x, g, w_main, w_gates, bm=512):
    n, d = x.shape
    bm = min(bm, n)
    return pl.pallas_call(
        functools.partial(_inproj_kernel, col_chunk=640),
        grid=(n // bm,),
        in_specs=[pl.BlockSpec((bm, d), lambda i: (i, 0)),
                  pl.BlockSpec((1, d), lambda i: (0, 0)),
                  pl.BlockSpec(w_main.shape, lambda i: (0, 0)),
                  pl.BlockSpec(w_gates.shape, lambda i: (0, 0))],
        out_specs=[pl.BlockSpec((bm, PROJ_COLS), lambda i: (i, 0)),
                   pl.BlockSpec((bm, V7X_LANES), lambda i: (i, 0))],
        out_shape=[jax.ShapeDtypeStruct((n, PROJ_COLS), BF16),
                   jax.ShapeDtypeStruct((n, V7X_LANES), F32)],
        compiler_params=_params("parallel"),
        name="inproj",
    )(x, g.reshape(1, d), w_main, w_gates)


def _prep_w_in(w):
    d = w.shape[0]
    offs = np.cumsum((0,) + IN_SECTIONS)
    (a_cq, a_ckv, a_kr, b_q, b_k, b_v, c_qk, c_v, c_o, c_g, d_q, d_k, d_v) = [
        w[:, offs[i]:offs[i + 1]] for i in range(len(IN_SECTIONS))]
    z = lambda k: jnp.zeros((d, k), w.dtype)
    half = MLA_ROPE // 2
    a_kr_rot = jnp.concatenate([a_kr[:, half:], a_kr[:, :half]], axis=1)
    main = jnp.concatenate([b_q, b_k, b_v, d_q, d_k, d_v, c_qk, c_v, c_o,
                            a_cq, z(64), a_ckv,
                            z(64), a_kr, z(32),
                            z(64), a_kr_rot, z(32)], axis=1)
    gates = jnp.concatenate([c_g, z(V7X_LANES - 16)], axis=1)
    return main.astype(BF16), gates.astype(BF16)


def _mla_prep_kernel(cq_ref, ckv_ref, kr_ref, krr_ref, cos_ref, sin_ref, gq_ref, gkv_ref,
                     wq_ref, wqr_ref, wkn_ref, wvt_ref, ones_ref, q_out, k_out, vt_out, *, q_scale):
    cos = cos_ref[...]
    sin = sin_ref[...]
    qn = (_rms(cq_ref[...].astype(F32), MLA_Q_LORA) * gq_ref[...]).astype(BF16)
    qa = jnp.dot(qn, wq_ref[...], preferred_element_type=F32)
    qr = jnp.dot(qn, wqr_ref[...], preferred_element_type=F32)
    kvn = (_rms(ckv_ref[...].astype(F32), MLA_KV_LORA) * gkv_ref[...]).astype(BF16)
    kn = jnp.dot(kvn, wkn_ref[...], preferred_element_type=F32)
    k_rope = kr_ref[...].astype(F32) * cos + krr_ref[...].astype(F32) * sin
    k_rope = jnp.where(lax.broadcasted_iota(jnp.int32, k_rope.shape, 1) == MLA_REF_LANE, 1.0, k_rope)
    for h in range(N_HEADS):
        sl = slice(h * V7X_LANES, (h + 1) * V7X_LANES)
        q_out[0, h] = ((qa[:, sl] * cos + qr[:, sl] * sin) * q_scale).astype(BF16)
        k_out[0, h] = (kn[:, sl] + k_rope).astype(BF16)
        vt = lax.dot_general(wvt_ref[h], kvn, (((1,), (1,)), ((), ())), preferred_element_type=F32)
        vt_out[0, h] = (vt + ones_ref[...]).astype(BF16)


def _mla_prep(proj, b, s, cos_t, sin_t, gq, gkv, wq, wqr, wkn, wvt, bm=512):
    bm = min(bm, s)
    nt = s // bm
    row = lambda bi, si: bi * nt + si
    ones_col = jnp.concatenate([jnp.zeros((HEAD_DIM, 1), F32), jnp.ones((ONES_ROWS, 1), F32)], axis=0)
    full = lambda a: pl.BlockSpec(a.shape, lambda bi, si: (0,) * a.ndim)
    return pl.pallas_call(
        functools.partial(_mla_prep_kernel, q_scale=LOG2E * (MLA_NOPE + MLA_ROPE) ** -0.5),
        grid=(b, nt),
        in_specs=[pl.BlockSpec((bm, 2 * V7X_LANES), lambda bi, si: (row(bi, si), U_ACQ // 2)),
                  pl.BlockSpec((bm, V7X_LANES), lambda bi, si: (row(bi, si), U_ACKV)),
                  pl.BlockSpec((bm, V7X_LANES), lambda bi, si: (row(bi, si), U_AKR)),
                  pl.BlockSpec((bm, V7X_LANES), lambda bi, si: (row(bi, si), U_AKRR)),
                  pl.BlockSpec((bm, V7X_LANES), lambda bi, si: (si, 0)),
                  pl.BlockSpec((bm, V7X_LANES), lambda bi, si: (si, 0)),
                  full(gq), full(gkv), full(wq), full(wqr), full(wkn), full(wvt), full(ones_col)],
        out_specs=[pl.BlockSpec((1, N_HEADS, bm, V7X_LANES), lambda bi, si: (bi, 0, si, 0)),
                   pl.BlockSpec((1, N_HEADS, bm, V7X_LANES), lambda bi, si: (bi, 0, si, 0)),
                   pl.BlockSpec((1, N_HEADS, VT_ROWS, bm), lambda bi, si: (bi, 0, 0, si))],
        out_shape=[jax.ShapeDtypeStruct((b, N_HEADS, s, V7X_LANES), BF16),
                   jax.ShapeDtypeStruct((b, N_HEADS, s, V7X_LANES), BF16),
                   jax.ShapeDtypeStruct((b, N_HEADS, VT_ROWS, s), BF16)],
        compiler_params=_params("parallel", "parallel"),
        name="mla_prep",
    )(proj, proj, proj, proj, cos_t, sin_t, gq, gkv, wq, wqr, wkn, wvt, ones_col)


def _prep_mla_weights(q_norm, kv_norm, w_uq, w_ukv):
    half = MLA_ROPE // 2
    dq = MLA_NOPE + MLA_ROPE
    wq_h = w_uq.reshape(MLA_Q_LORA, N_HEADS, dq)
    zq = lambda k: jnp.zeros((MLA_Q_LORA, N_HEADS, k), w_uq.dtype)
    rope_cols = wq_h[:, :, MLA_NOPE:]
    rope_rot = jnp.concatenate([rope_cols[:, :, half:], rope_cols[:, :, :half]], axis=2)
    wq = jnp.concatenate([wq_h, zq(V7X_LANES - dq)], axis=2).reshape(MLA_Q_LORA, N_HEADS * V7X_LANES)
    wqr = jnp.concatenate([zq(MLA_NOPE), rope_rot, zq(V7X_LANES - dq)], axis=2).reshape(
        MLA_Q_LORA, N_HEADS * V7X_LANES)
    pad_rows = jnp.zeros((2 * V7X_LANES - MLA_Q_LORA, N_HEADS * V7X_LANES), w_uq.dtype)
    wq = jnp.concatenate([wq, pad_rows], axis=0).astype(BF16)
    wqr = jnp.concatenate([wqr, pad_rows], axis=0).astype(BF16)
    wkv_h = w_ukv.reshape(MLA_KV_LORA, N_HEADS, MLA_NOPE + HEAD_DIM)
    wkn = jnp.concatenate([wkv_h[:, :, :MLA_NOPE],
                           jnp.zeros((MLA_KV_LORA, N_HEADS, V7X_LANES - MLA_NOPE), w_ukv.dtype)],
                          axis=2).reshape(MLA_KV_LORA, N_HEADS * V7X_LANES).astype(BF16)
    wvt = jnp.transpose(wkv_h[:, :, MLA_NOPE:], (1, 2, 0))
    wvt = jnp.concatenate([wvt, jnp.zeros((N_HEADS, ONES_ROWS, MLA_KV_LORA), w_ukv.dtype)],
                          axis=1).astype(BF16)
    gq = jnp.concatenate([q_norm, jnp.zeros((2 * V7X_LANES - MLA_Q_LORA,), q_norm.dtype)]).reshape(1, -1)
    gkv = kv_norm.reshape(1, -1)
    return gq, gkv, wq, wqr, wkn, wvt


def _rope_tables(s):
    half = MLA_ROPE // 2
    inv = ROPE_THETA ** (-jnp.arange(half, dtype=F32) / half)
    ang = jnp.arange(s).astype(F32)[:, None] * inv[None, :]
    cos, sin = jnp.cos(ang), jnp.sin(ang)
    ones = jnp.ones((s, MLA_NOPE), F32)
    z = lambda k: jnp.zeros((s, k), F32)
    pad = V7X_LANES - MLA_NOPE - MLA_ROPE
    cos_t = jnp.concatenate([ones, cos, cos, z(pad)], axis=1)
    sin_t = jnp.concatenate([z(MLA_NOPE), -sin, sin, z(pad)], axis=1)
    return cos_t, sin_t


_NT = (((1,), (1,)), ((), ()))


def _diag_ref_max(q, k_ref, q_start, bias=None):
    cols = []
    for j in range(q.shape[0] // V7X_LANES):
        k_diag = k_ref[0, 0, pl.ds(pl.multiple_of(q_start + j * V7X_LANES, V7X_LANES), V7X_LANES), :]
        sc = lax.dot_general(q[j * V7X_LANES:(j + 1) * V7X_LANES], k_diag, _NT, preferred_element_type=F32)
        if bias is not None:
            sc = sc + bias
        cols.append(jnp.max(sc, axis=1, keepdims=True))
    return jnp.concatenate(cols, axis=0)


def _with_ref_column(q, m_col, ref_lane):
    lane = lax.broadcasted_iota(jnp.int32, q.shape, 1)
    return jnp.where(lane == ref_lane, -m_col, q.astype(F32)).astype(BF16)


def _not_finite(acc):
    return jnp.max(jnp.where(jnp.isfinite(acc), 0.0, 1.0)) > 0.0


def _online_block(k_blk, q, vt_blk, m, acc, bias=None):
    st = lax.dot_general(k_blk, q, _NT, preferred_element_type=F32)
    if bias is not None:
        st = st + bias
    m_new = jnp.maximum(m, jnp.max(st, axis=0, keepdims=True))
    p = jnp.exp2(st - m_new).astype(BF16)
    return m_new, acc * jnp.exp2(m - m_new) + jnp.dot(vt_blk, p, preferred_element_type=F32)


def _mla_flash_kernel(q_ref, k_ref, vt_ref, o_ref, *, bk, unroll):
    qi = pl.program_id(2)
    q = q_ref[0, 0]
    bq = q.shape[0]
    s_len = k_ref.shape[2]
    q_aug = _with_ref_column(q, _diag_ref_max(q, k_ref, qi * bq), MLA_REF_LANE)

    def body(i, acc):
        for u in range(unroll):
            k0 = pl.multiple_of((i * unroll + u) * bk, bk)
            st = lax.dot_general(k_ref[0, 0, pl.ds(k0, bk), :], q_aug, _NT, preferred_element_type=F32)
            acc = acc + jnp.dot(vt_ref[0, 0, :, pl.ds(k0, bk)], jnp.exp2(st).astype(BF16),
                                preferred_element_type=F32)
        return acc

    acc0 = jnp.zeros((VT_ROWS, bq), F32)
    acc = lax.fori_loop(0, s_len // (bk * unroll), body, acc0)
    o_ref[0] = (acc[:HEAD_DIM] / acc[HEAD_DIM:HEAD_DIM + 1]).astype(o_ref.dtype)

    @pl.when(_not_finite(acc))
    def _():
        def exact(i, carry):
            k0 = pl.multiple_of(i * bk, bk)
            return _online_block(k_ref[0, 0, pl.ds(k0, bk), :], q, vt_ref[0, 0, :, pl.ds(k0, bk)], *carry)

        _, acc_x = lax.fori_loop(0, s_len // bk, exact, (jnp.full((1, bq), NEG_BIG, F32), acc0))
        o_ref[0] = (acc_x[:HEAD_DIM] / acc_x[HEAD_DIM:HEAD_DIM + 1]).astype(o_ref.dtype)


def _mla_flash(q, k, vt, bq=512, bk=1024, unroll=2):
    b, h, s, dq = q.shape
    bq, bk = min(bq, s), min(bk, s)
    unroll = min(unroll, s // bk)
    return pl.pallas_call(
        functools.partial(_mla_flash_kernel, bk=bk, unroll=unroll),
        grid=(b, h, s // bq),
        in_specs=[pl.BlockSpec((1, 1, bq, dq), lambda bi, hi, qi: (bi, hi, qi, 0)),
                  pl.BlockSpec((1, 1, s, dq), lambda bi, hi, qi: (bi, hi, 0, 0)),
                  pl.BlockSpec((1, 1, VT_ROWS, s), lambda bi, hi, qi: (bi, hi, 0, 0))],
        out_specs=pl.BlockSpec((1, HEAD_DIM, bq), lambda bi, hi, qi: (bi, hi, qi)),
        out_shape=jax.ShapeDtypeStruct((b, h * HEAD_DIM, s), BF16),
        compiler_params=_params("parallel", "parallel", "arbitrary"),
        name="mla_flash",
    )(q, k, vt)


def _diff_flash_kernel(sc_ref, q_ref, k_ref, vt_ref, g_ref, o_ref, *, bk, unroll, q_scale, out_scale):
    hi = pl.program_id(1)
    qi = pl.program_id(2)
    slope = sc_ref[hi]
    lam = sc_ref[N_HEADS]
    qf = q_ref[0, 0].astype(F32) * q_scale
    bq = qf.shape[0]
    s_len = k_ref.shape[2]
    lane = lax.broadcasted_iota(jnp.int32, qf.shape, 1)
    q1 = jnp.where(lane < DIFF_HALF, qf, 0.0).astype(BF16)
    q2 = jnp.where(lane >= DIFF_HALF, qf, 0.0).astype(BF16)
    diag = (lax.broadcasted_iota(jnp.int32, (V7X_LANES, V7X_LANES), 0)
            - lax.broadcasted_iota(jnp.int32, (V7X_LANES, V7X_LANES), 1)).astype(F32)
    diag_bias = jnp.abs(diag) * slope
    q1a = _with_ref_column(q1, _diag_ref_max(q1, k_ref, qi * bq, diag_bias), DIFF_REF_LANE)
    q2a = _with_ref_column(q2, _diag_ref_max(q2, k_ref, qi * bq, diag_bias), DIFF_REF_LANE)
    rel = (lax.broadcasted_iota(jnp.int32, (bk, bq), 0)
           - lax.broadcasted_iota(jnp.int32, (bk, bq), 1)).astype(F32)
    q0 = (qi * bq).astype(F32)

    def block_inputs(i):
        k0 = pl.multiple_of(i * bk, bk)
        bias = jnp.abs(rel + (k0.astype(F32) - q0)) * slope
        return k_ref[0, 0, pl.ds(k0, bk), :], vt_ref[0, 0, :, pl.ds(k0, bk)], bias

    def body(i, carry):
        a1, a2 = carry
        for u in range(unroll):
            k_blk, vt_blk, bias = block_inputs(i * unroll + u)
            s1 = lax.dot_general(k_blk, q1a, _NT, preferred_element_type=F32) + bias
            a1 = a1 + jnp.dot(vt_blk, jnp.exp2(s1).astype(BF16), preferred_element_type=F32)
            s2 = lax.dot_general(k_blk, q2a, _NT, preferred_element_type=F32) + bias
            a2 = a2 + jnp.dot(vt_blk, jnp.exp2(s2).astype(BF16), preferred_element_type=F32)
        return a1, a2

    def finish(a1, a2):
        o = a1[:HEAD_DIM] / a1[HEAD_DIM:HEAD_DIM + 1] - lam * (a2[:HEAD_DIM] / a2[HEAD_DIM:HEAD_DIM + 1])
        ms = jnp.sum(o * o, axis=0, keepdims=True) * (1.0 / HEAD_DIM)
        o_ref[0] = (o * lax.rsqrt(ms + EPS) * g_ref[...] * out_scale).astype(o_ref.dtype)

    acc0 = jnp.zeros((VT_ROWS, bq), F32)
    a1, a2 = lax.fori_loop(0, s_len // (bk * unroll), body, (acc0, acc0))
    finish(a1, a2)

    @pl.when(_not_finite(a1) | _not_finite(a2))
    def _():
        def exact(i, carry):
            m1, x1, m2, x2 = carry
            k_blk, vt_blk, bias = block_inputs(i)
            m1, x1 = _online_block(k_blk, q1, vt_blk, m1, x1, bias)
            m2, x2 = _online_block(k_blk, q2, vt_blk, m2, x2, bias)
            return m1, x1, m2, x2

        m0 = jnp.full((1, bq), NEG_BIG, F32)
        _, x1, _, x2 = lax.fori_loop(0, s_len // bk, exact, (m0, acc0, m0, acc0))
        finish(x1, x2)


def _diff_flash(scalars, q, k, vt, subln, layer_idx, bq=512, bk=1024, unroll=2):
    b, h, s, d = q.shape
    bq, bk = min(bq, s), min(bk, s)
    unroll = min(unroll, s // bk)
    lam_init = 0.8 - 0.6 * math.exp(-0.3 * layer_idx)
    return pl.pallas_call(
        functools.partial(_diff_flash_kernel, bk=bk, unroll=unroll, q_scale=LOG2E * DIFF_HALF ** -0.5,
                          out_scale=1.0 - lam_init),
        grid=(b, h, s // bq),
        in_specs=[pl.BlockSpec(memory_space=pltpu.SMEM),
                  pl.BlockSpec((1, 1, bq, d), lambda bi, hi, qi: (bi, hi, qi, 0)),
                  pl.BlockSpec((1, 1, s, d), lambda bi, hi, qi: (bi, hi, 0, 0)),
                  pl.BlockSpec((1, 1, VT_ROWS, s), lambda bi, hi, qi: (bi, hi, 0, 0)),
                  pl.BlockSpec((HEAD_DIM, 1), lambda bi, hi, qi: (0, 0))],
        out_specs=pl.BlockSpec((1, HEAD_DIM, bq), lambda bi, hi, qi: (bi, hi, qi)),
        out_shape=jax.ShapeDtypeStruct((b, h * HEAD_DIM, s), BF16),
        compiler_params=_params("parallel", "parallel", "arbitrary"),
        name="diff_flash",
    )(scalars, q, k, vt, subln.reshape(HEAD_DIM, 1))


def _conv_silu_kernel(x_ref, prev_ref, next_ref, w_ref, o_ref, *, n_tiles):
    si = pl.program_id(1)
    x = x_ref[...]
    bm = x.shape[0]
    r = lax.broadcasted_iota(jnp.int32, (bm, bm), 0)
    c = lax.broadcasted_iota(jnp.int32, (bm, bm), 1)
    shift_dn = jnp.where(r == c + 1, 1.0, 0.0).astype(BF16)
    shift_up = jnp.where(r + 1 == c, 1.0, 0.0).astype(BF16)
    x_prev = jnp.dot(shift_dn, x, preferred_element_type=F32)
    x_next = jnp.dot(shift_up, x, preferred_element_type=F32)
    row = lax.broadcasted_iota(jnp.int32, x.shape, 0)
    halo_prev = jnp.where(si > 0, prev_ref[7:8, :].astype(F32), 0.0)
    halo_next = jnp.where(si < n_tiles - 1, next_ref[0:1, :].astype(F32), 0.0)
    x_prev = jnp.where(row == 0, halo_prev, x_prev)
    x_next = jnp.where(row == bm - 1, halo_next, x_next)
    w = w_ref[...]
    y = x_prev * w[0:1] + x.astype(F32) * w[1:2] + x_next * w[2:3]
    y = y * jax.nn.sigmoid(y)
    col = lax.broadcasted_iota(jnp.int32, x.shape, 1)
    o_ref[...] = jnp.where(col >= GROUP_WIDTH, y * (HEAD_DIM ** -0.5), y).astype(o_ref.dtype)


def _conv_silu(proj, b, s, conv_w, bm=256):
    bm = min(bm, s)
    nt = s // bm
    c = 2 * GROUP_WIDTH
    cb = U_CQK * V7X_LANES // c
    hb = bm // 8
    n8 = b * s // 8
    return pl.pallas_call(
        functools.partial(_conv_silu_kernel, n_tiles=nt),
        grid=(b, nt),
        in_specs=[pl.BlockSpec((bm, c), lambda bi, si: (bi * nt + si, cb)),
                  pl.BlockSpec((8, c), lambda bi, si: (jnp.maximum((bi * nt + si) * hb - 1, 0), cb)),
                  pl.BlockSpec((8, c), lambda bi, si: (jnp.minimum((bi * nt + si + 1) * hb, n8 - 1), cb)),
                  pl.BlockSpec((8, c), lambda bi, si: (0, 0))],
        out_specs=pl.BlockSpec((bm, c), lambda bi, si: (bi * nt + si, 0)),
        out_shape=jax.ShapeDtypeStruct((b * s, c), BF16),
        compiler_params=_params("parallel", "parallel"),
        name="mlstm_conv",
    )(proj, proj, proj, jnp.concatenate([conv_w, jnp.zeros((5, c), conv_w.dtype)], axis=0))


def _log_sigmoid(x):
    return jnp.minimum(x, 0.0) - jnp.log(1.0 + jnp.exp(-jnp.abs(x)))


def _mlstm_chunk(qc, kc, vt1, gcol, grow, caug, m, *, backward):
    L = qc.shape[0]
    d = HEAD_DIM
    li_c, lf_c = gcol[:, 0:1], _log_sigmoid(gcol[:, 1:2])
    li_r, lf_r = grow[0:1, :], _log_sigmoid(grow[1:2, :])
    s_i = lax.broadcasted_iota(jnp.int32, (L, L), 0)
    j_i = lax.broadcasted_iota(jnp.int32, (L, L), 1)
    if backward:
        a_mask, b_mat, valid = j_i <= s_i, s_i >= j_i, s_i >= j_i
        bcum_row_idx, last_lane = L - 1, 0
    else:
        a_mask, b_mat, valid = j_i >= s_i, s_i <= j_i, s_i <= j_i
        bcum_row_idx, last_lane = 0, L - 1
    a = jnp.where(a_mask, lf_r, 0.0)
    a_hi = a.astype(BF16)
    a_lo = (a - a_hi.astype(F32)).astype(BF16)
    ones_b = jnp.where(b_mat, 1.0, 0.0).astype(BF16)
    e = jnp.dot(a_hi, ones_b, preferred_element_type=F32) + jnp.dot(a_lo, ones_b, preferred_element_type=F32)
    bcum_r = e[bcum_row_idx:bcum_row_idx + 1, :]
    dlog = jnp.where(valid, e + (li_c - lf_c), NEG_BIG)
    inter = bcum_r + m
    m_row = jnp.maximum(inter, jnp.max(dlog, axis=0, keepdims=True))
    w_intra = jnp.exp(dlog - m_row)
    w_inter = jnp.exp(inter - m_row)
    st = lax.dot_general(kc, qc, (((1,), (1,)), ((), ())), preferred_element_type=F32)
    at = (st * w_intra).astype(BF16)
    pv = jnp.dot(vt1, at, preferred_element_type=F32)
    cq = lax.dot_general(caug.astype(BF16), qc, (((1,), (1,)), ((), ())), preferred_element_type=F32)
    num = w_inter * cq[:d] + pv[:d]
    den = w_inter * cq[d:d + 1] + pv[d:d + 1]
    h = num / jnp.maximum(jnp.abs(den), jnp.exp(-m_row))
    b_last = bcum_r[:, last_lane:last_lane + 1]
    logw_end = b_last - bcum_r + li_r
    m_new = jnp.maximum(b_last + m, jnp.max(logw_end, axis=1, keepdims=True))
    w_end = jnp.exp(logw_end - m_new)
    decay = jnp.exp(b_last + m - m_new)
    u = jnp.dot((vt1.astype(F32) * w_end).astype(BF16), kc, preferred_element_type=F32)
    caug = decay * caug + u[:d + 8]
    return h, caug, m_new


def _mlstm_kernel(q_ref, k_ref, vt_ref, gc_ref, gr_ref, op_ref, ng_ref, o_ref, hf_sc, hb_sc):
    L = MLSTM_CHUNK
    d = HEAD_DIM
    s = q_ref.shape[2]
    nc = s // L
    ones = jnp.ones((ONES_ROWS, L), BF16)

    def chunk(c, caug, m, backward):
        t0 = pl.multiple_of(c * L, L)
        vt1 = jnp.concatenate([vt_ref[0, :, pl.ds(t0, L)], ones], axis=0)
        gcol = gc_ref[0, 0, pl.ds(t0, L), :]
        grow = gr_ref[0, 0, :, pl.ds(t0, L)]
        if backward:
            gcol, grow = gcol[:, 2:4], grow[2:4, :]
        else:
            gcol, grow = gcol[:, 0:2], grow[0:2, :]
        return _mlstm_chunk(q_ref[0, 0, pl.ds(t0, L), :], k_ref[0, 0, pl.ds(t0, L), :], vt1, gcol, grow,
                            caug, m, backward=backward), t0

    def body(i, carry):
        cf, mf, cb, mb = carry
        (hf, cf, mf), tf = chunk(i, cf, mf, False)
        hf_sc[:, pl.ds(tf, L)] = hf
        (hb, cb, mb), tb = chunk(nc - 1 - i, cb, mb, True)
        hb_sc[:, pl.ds(tb, L)] = hb
        return cf, mf, cb, mb

    c0 = jnp.zeros((d + 8, d), F32)
    m0 = jnp.zeros((1, 1), F32)
    lax.fori_loop(0, nc, body, (c0, m0, c0, m0))
    hsum = hf_sc[...] + hb_sc[...]
    ms = jnp.sum(hsum * hsum, axis=0, keepdims=True) * (1.0 / d)
    hn = hsum * lax.rsqrt(ms + EPS) * ng_ref[0]
    o_ref[0] = (hn * jax.nn.sigmoid(op_ref[0].astype(F32))).astype(o_ref.dtype)


def _mlstm(q, k, vt, gcol, grow, opre_t, norm_g):
    b, h, s, d = q.shape
    return pl.pallas_call(
        _mlstm_kernel,
        grid=(b, h),
        in_specs=[pl.BlockSpec((1, 1, s, d), lambda bi, hi: (bi, hi, 0, 0)),
                  pl.BlockSpec((1, 1, s, d), lambda bi, hi: (bi, hi, 0, 0)),
                  pl.BlockSpec((1, d, s), lambda bi, hi: (bi, hi, 0)),
                  pl.BlockSpec((1, 1, s, 4), lambda bi, hi: (bi, hi, 0, 0)),
                  pl.BlockSpec((1, 1, 4, s), lambda bi, hi: (bi, hi, 0, 0)),
                  pl.BlockSpec((1, d, s), lambda bi, hi: (bi, hi, 0)),
                  pl.BlockSpec((1, d, 1), lambda bi, hi: (hi, 0, 0))],
        out_specs=pl.BlockSpec((1, d, s), lambda bi, hi: (bi, hi, 0)),
        out_shape=jax.ShapeDtypeStruct((b, h * d, s), BF16),
        scratch_shapes=[pltpu.VMEM((d, s), F32), pltpu.VMEM((d, s), F32)],
        compiler_params=_params("parallel", "parallel"),
        name="mlstm_scan",
    )(q, k, vt, gcol, grow, opre_t, norm_g.reshape(h, d, 1))


def _na_kernel(q_ref, k_ref, v_ref, bias_ref, o_ref, *, rows_per_step, n_rows):
    blk = pl.program_id(1)
    win = NA_ROWS * GRID_W
    for i in range(rows_per_step):
        r = blk * rows_per_step + i
        r0 = jnp.clip(r - NA_ROWS // 2, 0, n_rows - NA_ROWS)
        dsel = r - r0
        k0 = pl.multiple_of(r0 * GRID_W, GRID_W)
        for h in range(N_HEADS):
            qh = q_ref[0, h, i * GRID_W:(i + 1) * GRID_W, :]
            kw = k_ref[0, h, pl.ds(k0, win), :]
            vw = v_ref[0, h, pl.ds(k0, win), :]
            sc = lax.dot_general(qh, kw, (((1,), (1,)), ((), ())), preferred_element_type=F32)
            sc = sc * (HEAD_DIM ** -0.5) + bias_ref[h, dsel]
            mx = jnp.max(sc, axis=-1, keepdims=True)
            p = jnp.exp(sc - mx)
            p = (p / jnp.sum(p, axis=-1, keepdims=True)).astype(BF16)
            o_ref[0, h, i * GRID_W:(i + 1) * GRID_W, :] = jnp.dot(
                p, vw, preferred_element_type=F32).astype(o_ref.dtype)


def _na(q, k, v, bias, rows_per_step=8):
    b, h, s, d = q.shape
    n_rows = s // GRID_W
    rows_per_step = min(rows_per_step, n_rows)
    bm = rows_per_step * GRID_W
    return pl.pallas_call(
        functools.partial(_na_kernel, rows_per_step=rows_per_step, n_rows=n_rows),
        grid=(b, n_rows // rows_per_step),
        in_specs=[pl.BlockSpec((1, h, bm, d), lambda bi, ri: (bi, 0, ri, 0)),
                  pl.BlockSpec((1, h, s, d), lambda bi, ri: (bi, 0, 0, 0)),
                  pl.BlockSpec((1, h, s, d), lambda bi, ri: (bi, 0, 0, 0)),
                  pl.BlockSpec(bias.shape, lambda bi, ri: (0, 0, 0, 0))],
        out_specs=pl.BlockSpec((1, h, bm, d), lambda bi, ri: (bi, 0, ri, 0)),
        out_shape=jax.ShapeDtypeStruct((b, h, s, d), BF16),
        compiler_params=_params("parallel", "arbitrary"),
        name="na_attn",
    )(q, k, v, bias)


def _na_bias_table(rpb, n_rows):
    wr = min(NA_ROWS, n_rows)
    cols = np.arange(GRID_W)
    col_start = np.clip(cols - NA_COLS // 2, 0, GRID_W - NA_COLS)
    ck = np.arange(GRID_W)[None, :]
    valid = (ck >= col_start[:, None]) & (ck < col_start[:, None] + NA_COLS)
    crel = ck - cols[:, None] + NA_COLS - 1
    rrel = np.arange(wr)[None, :] - np.arange(wr)[:, None] + NA_ROWS - 1
    c_sel = ((crel[None] == np.arange(2 * NA_COLS - 1)[:, None, None]) & valid[None]).astype(np.float32)
    r_sel = (rrel[None] == np.arange(2 * NA_ROWS - 1)[:, None, None]).astype(np.float32)
    t = jnp.einsum('hab,adw,bqk->hdqwk', rpb.astype(F32), r_sel, c_sel, precision=lax.Precision.HIGHEST)
    t = t + np.where(valid, 0.0, NEG_BIG).astype(np.float32)[None, None, :, None, :]
    return t.reshape(rpb.shape[0], wr, GRID_W, wr * GRID_W)


def _outproj_kernel(ya_ref, yb_ref, yc_ref, yd_ref, x_ref, w_ref, o_ref):
    acc = x_ref[...]
    for g, y_ref in enumerate((ya_ref, yb_ref, yc_ref, yd_ref)):
        acc = acc + lax.dot_general(y_ref[0], w_ref[g], (((0,), (0,)), ((), ())),
                                    preferred_element_type=F32)
    o_ref[...] = acc


def _outproj(ya, yb, yc, yd, x, w, b, s, bm=512):
    bm = min(bm, s)
    nt = s // bm
    d = x.shape[1]
    yspec = pl.BlockSpec((1, GROUP_WIDTH, bm), lambda bi, si: (bi, 0, si))
    return pl.pallas_call(
        _outproj_kernel,
        grid=(b, nt),
        in_specs=[yspec, yspec, yspec, yspec,
                  pl.BlockSpec((bm, d), lambda bi, si: (bi * nt + si, 0)),
                  pl.BlockSpec(w.shape, lambda bi, si: (0, 0, 0))],
        out_specs=pl.BlockSpec((bm, d), lambda bi, si: (bi * nt + si, 0)),
        out_shape=jax.ShapeDtypeStruct(x.shape, F32),
        compiler_params=_params("parallel", "parallel"),
        name="outproj",
    )(ya, yb, yc, yd, x, w)


def _ffn_kernel(x_ref, g_ref, wg_ref, wu_ref, wd_ref, o_ref, hn_sc, acc_sc):
    f = pl.program_id(1)

    @pl.when(f == 0)
    def _():
        x = x_ref[...]
        hn_sc[...] = (_rms(x, x.shape[-1]) * g_ref[...]).astype(BF16)
        acc_sc[...] = jnp.zeros_like(acc_sc)

    hn = hn_sc[...]
    gate = jnp.dot(hn, wg_ref[...], preferred_element_type=F32)
    up = jnp.dot(hn, wu_ref[...], preferred_element_type=F32)
    act = (gate * jax.nn.sigmoid(gate) * up).astype(BF16)
    acc_sc[...] += jnp.dot(act, wd_ref[...], preferred_element_type=F32)

    @pl.when(f == pl.num_programs(1) - 1)
    def _():
        o_ref[...] = x_ref[...] + acc_sc[...]


def _ffn(x, g, wg, wu, wd, bm=512, bf=1408):
    n, d = x.shape
    ff = wg.shape[1]
    bm = min(bm, n)
    return pl.pallas_call(
        _ffn_kernel,
        grid=(n // bm, ff // bf),
        in_specs=[pl.BlockSpec((bm, d), lambda i, f: (i, 0)),
                  pl.BlockSpec((1, d), lambda i, f: (0, 0)),
                  pl.BlockSpec((d, bf), lambda i, f: (0, f)),
                  pl.BlockSpec((d, bf), lambda i, f: (0, f)),
                  pl.BlockSpec((bf, d), lambda i, f: (f, 0))],
        out_specs=pl.BlockSpec((bm, d), lambda i, f: (i, 0)),
        out_shape=jax.ShapeDtypeStruct(x.shape, F32),
        scratch_shapes=[pltpu.VMEM((bm, d), BF16), pltpu.VMEM((bm, d), F32)],
        compiler_params=_params("parallel", "arbitrary"),
        name="ffn",
    )(x, g.reshape(1, d), wg, wu, wd)


def _router_kernel(x_ref, g_ref, wr_ref, hn_ref, cw_ref):
    x = x_ref[...]
    hn = _rms(x, x.shape[-1]) * g_ref[...]
    hn_ref[...] = hn.astype(BF16)
    logits = jnp.dot(hn, wr_ref[...], preferred_element_type=F32, precision=lax.Precision.HIGHEST)
    lane = lax.broadcasted_iota(jnp.int32, logits.shape, 1)
    logits = jnp.where(lane < N_EXPERTS, logits, NEG_BIG)
    m1 = jnp.max(logits, axis=-1, keepdims=True)
    i1 = jnp.min(jnp.where(logits == m1, lane, V7X_LANES), axis=-1, keepdims=True)
    rest = jnp.where(lane == i1, NEG_BIG, logits)
    m2 = jnp.max(rest, axis=-1, keepdims=True)
    i2 = jnp.min(jnp.where(rest == m2, lane, V7X_LANES), axis=-1, keepdims=True)
    e2 = jnp.exp(m2 - m1)
    w1 = 1.0 / (1.0 + e2)
    w2 = e2 / (1.0 + e2)
    cw_ref[...] = jnp.where(lane == i1, w1, 0.0) + jnp.where(lane == i2, w2, 0.0)


def _router(x, g, w_router, bm=512):
    n, d = x.shape
    bm = min(bm, n)
    wr = jnp.concatenate([w_router, jnp.zeros((d, V7X_LANES - N_EXPERTS), w_router.dtype)], axis=1)
    return pl.pallas_call(
        _router_kernel,
        grid=(n // bm,),
        in_specs=[pl.BlockSpec((bm, d), lambda i: (i, 0)),
                  pl.BlockSpec((1, d), lambda i: (0, 0)),
                  pl.BlockSpec((d, V7X_LANES), lambda i: (0, 0))],
        out_specs=[pl.BlockSpec((bm, d), lambda i: (i, 0)),
                   pl.BlockSpec((bm, V7X_LANES), lambda i: (i, 0))],
        out_shape=[jax.ShapeDtypeStruct((n, d), BF16),
                   jax.ShapeDtypeStruct((n, V7X_LANES), F32)],
        compiler_params=_params("parallel"),
        name="moe_router",
    )(x, g.reshape(1, d), wr)


def _moe_kernel(x_ref, hn_ref, cw_ref, wg_ref, wu_ref, wd_ref, o_ref, acc_sc):
    e = pl.program_id(1)
    f = pl.program_id(2)

    @pl.when((e == 0) & (f == 0))
    def _():
        acc_sc[...] = jnp.zeros_like(acc_sc)

    hn = hn_ref[...]
    gate = jnp.dot(hn, wg_ref[0], preferred_element_type=F32)
    up = jnp.dot(hn, wu_ref[0], preferred_element_type=F32)
    act = (gate * jax.nn.sigmoid(gate) * up).astype(BF16)
    cw = cw_ref[...]
    lane = lax.broadcasted_iota(jnp.int32, cw.shape, 1)
    ce = jnp.sum(jnp.where(lane == e, cw, 0.0), axis=-1, keepdims=True)
    acc_sc[...] += ce * jnp.dot(act, wd_ref[0], preferred_element_type=F32)

    @pl.when((e == pl.num_programs(1) - 1) & (f == pl.num_programs(2) - 1))
    def _():
        o_ref[...] = x_ref[...] + acc_sc[...]


def _moe(x, hn, cw, wg, wu, wd, bm=512, bf=896):
    n, d = x.shape
    ne, _, ff = wg.shape
    bm = min(bm, n)
    return pl.pallas_call(
        _moe_kernel,
        grid=(n // bm, ne, ff // bf),
        in_specs=[pl.BlockSpec((bm, d), lambda i, e, f: (i, 0)),
                  pl.BlockSpec((bm, d), lambda i, e, f: (i, 0)),
                  pl.BlockSpec((bm, V7X_LANES), lambda i, e, f: (i, 0)),
                  pl.BlockSpec((1, d, bf), lambda i, e, f: (e, 0, f)),
                  pl.BlockSpec((1, d, bf), lambda i, e, f: (e, 0, f)),
                  pl.BlockSpec((1, bf, d), lambda i, e, f: (e, f, 0))],
        out_specs=pl.BlockSpec((bm, d), lambda i, e, f: (i, 0)),
        out_shape=jax.ShapeDtypeStruct(x.shape, F32),
        scratch_shapes=[pltpu.VMEM((bm, d), F32)],
        compiler_params=_params("parallel", "arbitrary", "arbitrary"),
        name="moe_ffn",
    )(x, hn, cw, wg, wu, wd)


def _final_norm_kernel(x_ref, g_ref, o_ref):
    x = x_ref[...]
    o_ref[...] = _rms(x, x.shape[-1]) * g_ref[...]


def _final_norm(x, g, bm=1024):
    n, d = x.shape
    bm = min(bm, n)
    return pl.pallas_call(
        _final_norm_kernel,
        grid=(n // bm,),
        in_specs=[pl.BlockSpec((bm, d), lambda i: (i, 0)), pl.BlockSpec((1, d), lambda i: (0, 0))],
        out_specs=pl.BlockSpec((bm, d), lambda i: (i, 0)),
        out_shape=jax.ShapeDtypeStruct(x.shape, F32),
        compiler_params=_params("parallel"),
        name="final_norm",
    )(x, g.reshape(1, d))


def _heads(t, b, s):
    return t.reshape(b, s, N_HEADS, HEAD_DIM).transpose(0, 2, 1, 3)


def _chan_major(t, b, s):
    return t.reshape(b, s, t.shape[-1]).transpose(0, 2, 1)


def _with_ones(vt):
    b, _, s = vt.shape
    vt = vt.reshape(b, N_HEADS, HEAD_DIM, s)
    return jnp.concatenate([vt, jnp.ones((b, N_HEADS, ONES_ROWS, s), vt.dtype)], axis=2)


def _cols(proj, unit, n_units):
    return proj[:, unit * V7X_LANES:(unit + n_units) * V7X_LANES]


def _token_mix(x, i, b, s, p):
    w_main, w_gates = _prep_w_in(p['w_in'][i])
    proj, gates = _inproj(x, p['norm_mix'][i], w_main, w_gates)

    cos_t, sin_t = _rope_tables(s)
    q_a, k_a, vt_a = _mla_prep(proj, b, s, cos_t, sin_t, *_prep_mla_weights(
        p['mla_q_norm'][i], p['mla_kv_norm'][i], p['mla_w_uq'][i], p['mla_w_ukv'][i]))
    y_a = _mla_flash(q_a, k_a, vt_a)

    lp = p['diff_lambda'][i].astype(F32)
    lam_init = 0.8 - 0.6 * math.exp(-0.3 * i)
    lam = jnp.exp(jnp.sum(lp[0] * lp[1])) - jnp.exp(jnp.sum(lp[2] * lp[3])) + lam_init
    slopes = 2.0 ** (-8.0 * jnp.arange(1, N_HEADS + 1, dtype=F32) / N_HEADS)
    scalars = jnp.concatenate([-slopes * LOG2E, lam[None]]).astype(F32)
    q_b = _heads(_cols(proj, U_BQ, 2), b, s)
    k_b = _heads(_cols(proj, U_BK, 2), b, s)
    pad = jnp.zeros(q_b.shape[:-1] + (V7X_LANES - HEAD_DIM - 1,), BF16)
    q_b = jnp.concatenate([q_b, jnp.zeros_like(q_b)], axis=-1)
    k_b = jnp.concatenate([k_b, jnp.ones(k_b.shape[:-1] + (1,), BF16), pad], axis=-1)
    y_b = _diff_flash(scalars, q_b, k_b, _with_ones(_chan_major(_cols(proj, U_BV, 2), b, s)),
                      p['diff_subln'][i], i)

    qk = _conv_silu(proj, b, s, p['mlstm_conv'][i])
    g = (gates[:, :4 * N_HEADS] + p['mlstm_gate_bias'][i][None, :]).reshape(b, s, 4, N_HEADS)
    y_c = _mlstm(_heads(qk[:, :GROUP_WIDTH], b, s), _heads(qk[:, GROUP_WIDTH:], b, s),
                 _chan_major(_cols(proj, U_CV, 2), b, s),
                 g.transpose(0, 3, 1, 2), g.transpose(0, 3, 2, 1),
                 _chan_major(_cols(proj, U_CO, 2), b, s), p['mlstm_norm'][i])

    y_d = _na(_heads(_cols(proj, U_DQ, 2), b, s), _heads(_cols(proj, U_DK, 2), b, s),
              _heads(_cols(proj, U_DV, 2), b, s), _na_bias_table(p['na_rpb'][i], s // GRID_W))
    y_d = y_d.transpose(0, 1, 3, 2).reshape(b, GROUP_WIDTH, s)

    w_out = p['w_out'][i].reshape(4, GROUP_WIDTH, -1).astype(BF16)
    return _outproj(y_a, y_b, y_c, y_d, x, w_out, b, s)


def _trunk(x, p, depth):
    b, s, d = x.shape
    x = x.reshape(b * s, d)
    for i in range(depth):
        x = _token_mix(x, i, b, s, p)
        j = i // 2
        if i % 2 == 0:
            x = _ffn(x, p['norm_ffn'][i], p['ffn_w_gate'][j].astype(BF16), p['ffn_w_up'][j].astype(BF16),
                     p['ffn_w_down'][j].astype(BF16))
        else:
            hn, cw = _router(x, p['norm_ffn'][i], p['moe_router'][j])
            x = _moe(x, hn, cw, p['moe_w_gate'][j].astype(BF16), p['moe_w_up'][j].astype(BF16),
                     p['moe_w_down'][j].astype(BF16))
    return _final_norm(x, p['norm_final']).reshape(b, s, d)


def kernel(x_prompt, x_sample, norm_mix, norm_ffn, w_in, w_out, mla_q_norm, mla_kv_norm, mla_w_uq, mla_w_ukv,
           diff_lambda, diff_subln, mlstm_conv, mlstm_gate_bias, mlstm_norm, na_rpb, ffn_w_gate, ffn_w_up,
           ffn_w_down, moe_router, moe_w_gate, moe_w_up, moe_w_down, norm_final):
    p = dict(norm_mix=norm_mix, norm_ffn=norm_ffn, w_in=w_in, w_out=w_out, mla_q_norm=mla_q_norm,
             mla_kv_norm=mla_kv_norm, mla_w_uq=mla_w_uq, mla_w_ukv=mla_w_ukv, diff_lambda=diff_lambda,
             diff_subln=diff_subln, mlstm_conv=mlstm_conv, mlstm_gate_bias=mlstm_gate_bias,
             mlstm_norm=mlstm_norm, na_rpb=na_rpb, ffn_w_gate=ffn_w_gate, ffn_w_up=ffn_w_up,
             ffn_w_down=ffn_w_down, moe_router=moe_router, moe_w_gate=moe_w_gate, moe_w_up=moe_w_up,
             moe_w_down=moe_w_down, norm_final=norm_final)
    depth = norm_mix.shape[0]
    nb = x_prompt.shape[0]
    y = _trunk(jnp.concatenate([x_prompt, x_sample], axis=0), p, depth)
    return (y[:nb], y[nb:])
```

```python
import functools
import math

import numpy as np
import jax
import jax.numpy as jnp
from jax import lax
from jax.experimental import pallas as pl
from jax.experimental.pallas import tpu as pltpu

F32 = jnp.float32
BF16 = jnp.bfloat16

V7X_LANES = 128
V7X_VMEM_LIMIT_BYTES = 56 * 1024 * 1024

EPS = 1e-6
LOG2E = 1.4426950408889634
NEG_BIG = -1e30

HEAD_DIM = 64
N_HEADS = 4
GROUP_WIDTH = 256
MLA_Q_LORA = 192
MLA_KV_LORA = 128
MLA_NOPE = 64
MLA_ROPE = 32
ROPE_THETA = 10000.0
DIFF_HALF = 32
MLSTM_CHUNK = 128
GRID_W = 64
NA_ROWS = 8
NA_COLS = 16
N_EXPERTS = 8
ONES_ROWS = 16
VT_ROWS = HEAD_DIM + ONES_ROWS
MLA_REF_LANE = MLA_NOPE + MLA_ROPE
DIFF_REF_LANE = HEAD_DIM

IN_SECTIONS = (192, 128, 32, 256, 256, 256, 512, 256, 256, 16, 256, 256, 256)
U_BQ, U_BK, U_BV, U_DQ, U_DK, U_DV, U_CQK, U_CV, U_CO, U_ACQ, U_ACKV, U_AKR, U_AKRR = (
    0, 2, 4, 6, 8, 10, 12, 16, 18, 20, 22, 23, 24)
PROJ_COLS = 25 * V7X_LANES


def _params(*sem):
    return pltpu.CompilerParams(dimension_semantics=sem, vmem_limit_bytes=V7X_VMEM_LIMIT_BYTES)


def _rms(x, n):
    return x * lax.rsqrt(jnp.sum(x * x, axis=-1, keepdims=True) * (1.0 / n) + EPS)


def _inproj_kernel(x_ref, g_ref, w_ref, wg_ref, o_ref, og_ref, *, col_chunk):
    x = x_ref[...]
    hn = (_rms(x, x.shape[-1]) * g_ref[...]).astype(BF16)
    for c in range(0, o_ref.shape[1], col_chunk):
        w = min(col_chunk, o_ref.shape[1] - c)
        o_ref[:, c:c + w] = jnp.dot(hn, w_ref[:, c:c + w], preferred_element_type=F32).astype(BF16)
    og_ref[...] = jnp.dot(hn, wg_ref[...], preferred_element_type=F32)


def _inproj(x, g, w_main, w_gates, bm=512):
    n, d = x.shape
    bm = min(bm, n)
    return pl.pallas_call(
        functools.partial(_inproj_kernel, col_chunk=640),
        grid=(n // bm,),
        in_specs=[pl.BlockSpec((bm, d), lambda i: (i, 0)),
                  pl.BlockSpec((1, d), lambda i: (0, 0)),
                  pl.BlockSpec(w_main.shape, lambda i: (0, 0)),
                  pl.BlockSpec(w_gates.shape, lambda i: (0, 0))],
        out_specs=[pl.BlockSpec((bm, PROJ_COLS), lambda i: (i, 0)),
                   pl.BlockSpec((bm, V7X_LANES), lambda i: (i, 0))],
        out_shape=[jax.ShapeDtypeStruct((n, PROJ_COLS), BF16),
                   jax.ShapeDtypeStruct((n, V7X_LANES), F32)],
        compiler_params=_params("parallel"),
        name="inproj",
    )(x, g.reshape(1, d), w_main, w_gates)


def _prep_w_in(w):
    d = w.shape[0]
    offs = np.cumsum((0,) + IN_SECTIONS)
    (a_cq, a_ckv, a_kr, b_q, b_k, b_v, c_qk, c_v, c_o, c_g, d_q, d_k, d_v) = [
        w[:, offs[i]:offs[i + 1]] for i in range(len(IN_SECTIONS))]
    z = lambda k: jnp.zeros((d, k), w.dtype)
    half = MLA_ROPE // 2
    a_kr_rot = jnp.concatenate([a_kr[:, half:], a_kr[:, :half]], axis=1)
    main = jnp.concatenate([b_q, b_k, b_v, d_q, d_k, d_v, c_qk, c_v, c_o,
                            a_cq, z(64), a_ckv,
                            z(64), a_kr, z(32),
                            z(64), a_kr_rot, z(32)], axis=1)
    gates = jnp.concatenate([c_g, z(V7X_LANES - 16)], axis=1)
    return main.astype(BF16), gates.astype(BF16)


def _mla_prep_kernel(cq_ref, ckv_ref, kr_ref, krr_ref, cos_ref, sin_ref, gq_ref, gkv_ref,
                     wq_ref, wqr_ref, wkn_ref, wvt_ref, ones_ref, q_out, k_out, vt_out, *, q_scale):
    cos = cos_ref[...]
    sin = sin_ref[...]
    qn = (_rms(cq_ref[...].astype(F32), MLA_Q_LORA) * gq_ref[...]).astype(BF16)
    qa = jnp.dot(qn, wq_ref[...], preferred_element_type=F32)
    qr = jnp.dot(qn, wqr_ref[...], preferred_element_type=F32)
    kvn = (_rms(ckv_ref[...].astype(F32), MLA_KV_LORA) * gkv_ref[...]).astype(BF16)
    kn = jnp.dot(kvn, wkn_ref[...], preferred_element_type=F32)
    k_rope = kr_ref[...].astype(F32) * cos + krr_ref[...].astype(F32) * sin
    k_rope = jnp.where(lax.broadcasted_iota(jnp.int32, k_rope.shape, 1) == MLA_REF_LANE, 1.0, k_rope)
    for h in range(N_HEADS):
        sl = slice(h * V7X_LANES, (h + 1) * V7X_LANES)
        q_out[0, h] = ((qa[:, sl] * cos + qr[:, sl] * sin) * q_scale).astype(BF16)
        k_out[0, h] = (kn[:, sl] + k_rope).astype(BF16)
        vt = lax.dot_general(wvt_ref[h], kvn, (((1,), (1,)), ((), ())), preferred_element_type=F32)
        vt_out[0, h] = (vt + ones_ref[...]).astype(BF16)


def _mla_prep(proj, b, s, cos_t, sin_t, gq, gkv, wq, wqr, wkn, wvt, bm=512):
    bm = min(bm, s)
    nt = s // bm
    row = lambda bi, si: bi * nt + si
    ones_col = jnp.concatenate([jnp.zeros((HEAD_DIM, 1), F32), jnp.ones((ONES_ROWS, 1), F32)], axis=0)
    full = lambda a: pl.BlockSpec(a.shape, lambda bi, si: (0,) * a.ndim)
    return pl.pallas_call(
        functools.partial(_mla_prep_kernel, q_scale=LOG2E * (MLA_NOPE + MLA_ROPE) ** -0.5),
        grid=(b, nt),
        in_specs=[pl.BlockSpec((bm, 2 * V7X_LANES), lambda bi, si: (row(bi, si), U_ACQ // 2)),
                  pl.BlockSpec((bm, V7X_LANES), lambda bi, si: (row(bi, si), U_ACKV)),
                  pl.BlockSpec((bm, V7X_LANES), lambda bi, si: (row(bi, si), U_AKR)),
                  pl.BlockSpec((bm, V7X_LANES), lambda bi, si: (row(bi, si), U_AKRR)),
                  pl.BlockSpec((bm, V7X_LANES), lambda bi, si: (si, 0)),
                  pl.BlockSpec((bm, V7X_LANES), lambda bi, si: (si, 0)),
                  full(gq), full(gkv), full(wq), full(wqr), full(wkn), full(wvt), full(ones_col)],
        out_specs=[pl.BlockSpec((1, N_HEADS, bm, V7X_LANES), lambda bi, si: (bi, 0, si, 0)),
                   pl.BlockSpec((1, N_HEADS, bm, V7X_LANES), lambda bi, si: (bi, 0, si, 0)),
                   pl.BlockSpec((1, N_HEADS, VT_ROWS, bm), lambda bi, si: (bi, 0, 0, si))],
        out_shape=[jax.ShapeDtypeStruct((b, N_HEADS, s, V7X_LANES), BF16),
                   jax.ShapeDtypeStruct((b, N_HEADS, s, V7X_LANES), BF16),
                   jax.ShapeDtypeStruct((b, N_HEADS, VT_ROWS, s), BF16)],
        compiler_params=_params("parallel", "parallel"),
        name="mla_prep",
    )(proj, proj, proj, proj, cos_t, sin_t, gq, gkv, wq, wqr, wkn, wvt, ones_col)


def _prep_mla_weights(q_norm, kv_norm, w_uq, w_ukv):
    half = MLA_ROPE // 2
    dq = MLA_NOPE + MLA_ROPE
    wq_h = w_uq.reshape(MLA_Q_LORA, N_HEADS, dq)
    zq = lambda k: jnp.zeros((MLA_Q_LORA, N_HEADS, k), w_uq.dtype)
    rope_cols = wq_h[:, :, MLA_NOPE:]
    rope_rot = jnp.concatenate([rope_cols[:, :, half:], rope_cols[:, :, :half]], axis=2)
    wq = jnp.concatenate([wq_h, zq(V7X_LANES - dq)], axis=2).reshape(MLA_Q_LORA, N_HEADS * V7X_LANES)
    wqr = jnp.concatenate([zq(MLA_NOPE), rope_rot, zq(V7X_LANES - dq)], axis=2).reshape(
        MLA_Q_LORA, N_HEADS * V7X_LANES)
    pad_rows = jnp.zeros((2 * V7X_LANES - MLA_Q_LORA, N_HEADS * V7X_LANES), w_uq.dtype)
    wq = jnp.concatenate([wq, pad_rows], axis=0).astype(BF16)
    wqr = jnp.concatenate([wqr, pad_rows], axis=0).astype(BF16)
    wkv_h = w_ukv.reshape(MLA_KV_LORA, N_HEADS, MLA_NOPE + HEAD_DIM)
    wkn = jnp.concatenate([wkv_h[:, :, :MLA_NOPE],
                           jnp.zeros((MLA_KV_LORA, N_HEADS, V7X_LANES - MLA_NOPE), w_ukv.dtype)],
                          axis=2).reshape(MLA_KV_LORA, N_HEADS * V7X_LANES).astype(BF16)
    wvt = jnp.transpose(wkv_h[:, :, MLA_NOPE:], (1, 2, 0))
    wvt = jnp.concatenate([wvt, jnp.zeros((N_HEADS, ONES_ROWS, MLA_KV_LORA), w_ukv.dtype)],
                          axis=1).astype(BF16)
    gq = jnp.concatenate([q_norm, jnp.zeros((2 * V7X_LANES - MLA_Q_LORA,), q_norm.dtype)]).reshape(1, -1)
    gkv = kv_norm.reshape(1, -1)
    return gq, gkv, wq, wqr, wkn, wvt


def _rope_tables(s):
    half = MLA_ROPE // 2
    inv = ROPE_THETA ** (-jnp.arange(half, dtype=F32) / half)
    ang = jnp.arange(s).astype(F32)[:, None] * inv[None, :]
    cos, sin = jnp.cos(ang), jnp.sin(ang)
    ones = jnp.ones((s, MLA_NOPE), F32)
    z = lambda k: jnp.zeros((s, k), F32)
    pad = V7X_LANES - MLA_NOPE - MLA_ROPE
    cos_t = jnp.concatenate([ones, cos, cos, z(pad)], axis=1)
    sin_t = jnp.concatenate([z(MLA_NOPE), -sin, sin, z(pad)], axis=1)
    return cos_t, sin_t


_NT = (((1,), (1,)), ((), ()))


def _diag_ref_max(q, k_ref, q_start, bias=None):
    cols = []
    for j in range(q.shape[0] // V7X_LANES):
        k_diag = k_ref[0, 0, pl.ds(pl.multiple_of(q_start + j * V7X_LANES, V7X_LANES), V7X_LANES), :]
        sc = lax.dot_general(q[j * V7X_LANES:(j + 1) * V7X_LANES], k_diag, _NT, preferred_element_type=F32)
        if bias is not None:
            sc = sc + bias
        cols.append(jnp.max(sc, axis=1, keepdims=True))
    return jnp.concatenate(cols, axis=0)


def _with_ref_column(q, m_col, ref_lane):
    lane = lax.broadcasted_iota(jnp.int32, q.shape, 1)
    return jnp.where(lane == ref_lane, -m_col, q.astype(F32)).astype(BF16)


def _not_finite(acc):
    return jnp.max(jnp.where(jnp.isfinite(acc), 0.0, 1.0)) > 0.0


def _online_block(k_blk, q, vt_blk, m, acc, bias=None):
    st = lax.dot_general(k_blk, q, _NT, preferred_element_type=F32)
    if bias is not None:
        st = st + bias
    m_new = jnp.maximum(m, jnp.max(st, axis=0, keepdims=True))
    p = jnp.exp2(st - m_new).astype(BF16)
    return m_new, acc * jnp.exp2(m - m_new) + jnp.dot(vt_blk, p, preferred_element_type=F32)


def _mla_flash_kernel(q_ref, k_ref, vt_ref, o_ref, *, bk, unroll):
    qi = pl.program_id(2)
    q = q_ref[0, 0]
    bq = q.shape[0]
    s_len = k_ref.shape[2]
    q_aug = _with_ref_column(q, _diag_ref_max(q, k_ref, qi * bq), MLA_REF_LANE)

    def body(i, acc):
        for u in range(unroll):
            k0 = pl.multiple_of((i * unroll + u) * bk, bk)
            st = lax.dot_general(k_ref[0, 0, pl.ds(k0, bk), :], q_aug, _NT, preferred_element_type=F32)
            acc = acc + jnp.dot(vt_ref[0, 0, :, pl.ds(k0, bk)], jnp.exp2(st).astype(BF16),
                                preferred_element_type=F32)
        return acc

    acc0 = jnp.zeros((VT_ROWS, bq), F32)
    acc = lax.fori_loop(0, s_len // (bk * unroll), body, acc0)
    o_ref[0] = (acc[:HEAD_DIM] / acc[HEAD_DIM:HEAD_DIM + 1]).astype(o_ref.dtype)

    @pl.when(_not_finite(acc))
    def _():
        def exact(i, carry):
            k0 = pl.multiple_of(i * bk, bk)
            return _online_block(k_ref[0, 0, pl.ds(k0, bk), :], q, vt_ref[0, 0, :, pl.ds(k0, bk)], *carry)

        _, acc_x = lax.fori_loop(0, s_len // bk, exact, (jnp.full((1, bq), NEG_BIG, F32), acc0))
        o_ref[0] = (acc_x[:HEAD_DIM] / acc_x[HEAD_DIM:HEAD_DIM + 1]).astype(o_ref.dtype)


def _mla_flash(q, k, vt, bq=512, bk=1024, unroll=2):
    b, h, s, dq = q.shape
    bq, bk = min(bq, s), min(bk, s)
    unroll = min(unroll, s // bk)
    return pl.pallas_call(
        functools.partial(_mla_flash_kernel, bk=bk, unroll=unroll),
        grid=(b, h, s // bq),
        in_specs=[pl.BlockSpec((1, 1, bq, dq), lambda bi, hi, qi: (bi, hi, qi, 0)),
                  pl.BlockSpec((1, 1, s, dq), lambda bi, hi, qi: (bi, hi, 0, 0)),
                  pl.BlockSpec((1, 1, VT_ROWS, s), lambda bi, hi, qi: (bi, hi, 0, 0))],
        out_specs=pl.BlockSpec((1, HEAD_DIM, bq), lambda bi, hi, qi: (bi, hi, qi)),
        out_shape=jax.ShapeDtypeStruct((b, h * HEAD_DIM, s), BF16),
        compiler_params=_params("parallel", "parallel", "arbitrary"),
        name="mla_flash",
    )(q, k, vt)


def _diff_flash_kernel(sc_ref, q_ref, k_ref, vt_ref, g_ref, o_ref, *, bk, unroll, q_scale, out_scale):
    hi = pl.program_id(1)
    qi = pl.program_id(2)
    slope = sc_ref[hi]
    lam = sc_ref[N_HEADS]
    qf = q_ref[0, 0].astype(F32) * q_scale
    bq = qf.shape[0]
    s_len = k_ref.shape[2]
    lane = lax.broadcasted_iota(jnp.int32, qf.shape, 1)
    q1 = jnp.where(lane < DIFF_HALF, qf, 0.0).astype(BF16)
    q2 = jnp.where(lane >= DIFF_HALF, qf, 0.0).astype(BF16)
    diag = (lax.broadcasted_iota(jnp.int32, (V7X_LANES, V7X_LANES), 0)
            - lax.broadcasted_iota(jnp.int32, (V7X_LANES, V7X_LANES), 1)).astype(F32)
    diag_bias = jnp.abs(diag) * slope
    q1a = _with_ref_column(q1, _diag_ref_max(q1, k_ref, qi * bq, diag_bias), DIFF_REF_LANE)
    q2a = _with_ref_column(q2, _diag_ref_max(q2, k_ref, qi * bq, diag_bias), DIFF_REF_LANE)
    rel = (lax.broadcasted_iota(jnp.int32, (bk, bq), 0)
           - lax.broadcasted_iota(jnp.int32, (bk, bq), 1)).astype(F32)
    q0 = (qi * bq).astype(F32)

    def block_inputs(i):
        k0 = pl.multiple_of(i * bk, bk)
        bias = jnp.abs(rel + (k0.astype(F32) - q0)) * slope
        return k_ref[0, 0, pl.ds(k0, bk), :], vt_ref[0, 0, :, pl.ds(k0, bk)], bias

    def body(i, carry):
        a1, a2 = carry
        for u in range(unroll):
            k_blk, vt_blk, bias = block_inputs(i * unroll + u)
            s1 = lax.dot_general(k_blk, q1a, _NT, preferred_element_type=F32) + bias
            a1 = a1 + jnp.dot(vt_blk, jnp.exp2(s1).astype(BF16), preferred_element_type=F32)
            s2 = lax.dot_general(k_blk, q2a, _NT, preferred_element_type=F32) + bias
            a2 = a2 + jnp.dot(vt_blk, jnp.exp2(s2).astype(BF16), preferred_element_type=F32)
        return a1, a2

    def finish(a1, a2):
        o = a1[:HEAD_DIM] / a1[HEAD_DIM:HEAD_DIM + 1] - lam * (a2[:HEAD_DIM] / a2[HEAD_DIM:HEAD_DIM + 1])
        ms = jnp.sum(o * o, axis=0, keepdims=True) * (1.0 / HEAD_DIM)
        o_ref[0] = (o * lax.rsqrt(ms + EPS) * g_ref[...] * out_scale).astype(o_ref.dtype)

    acc0 = jnp.zeros((VT_ROWS, bq), F32)
    a1, a2 = lax.fori_loop(0, s_len // (bk * unroll), body, (acc0, acc0))
    finish(a1, a2)

    @pl.when(_not_finite(a1) | _not_finite(a2))
    def _():
        def exact(i, carry):
            m1, x1, m2, x2 = carry
            k_blk, vt_blk, bias = block_inputs(i)
            m1, x1 = _online_block(k_blk, q1, vt_blk, m1, x1, bias)
            m2, x2 = _online_block(k_blk, q2, vt_blk, m2, x2, bias)
            return m1, x1, m2, x2

        m0 = jnp.full((1, bq), NEG_BIG, F32)
        _, x1, _, x2 = lax.fori_loop(0, s_len // bk, exact, (m0, acc0, m0, acc0))
        finish(x1, x2)


def _diff_flash(scalars, q, k, vt, subln, layer_idx, bq=512, bk=1024, unroll=2):
    b, h, s, d = q.shape
    bq, bk = min(bq, s), min(bk, s)
    unroll = min(unroll, s // bk)
    lam_init = 0.8 - 0.6 * math.exp(-0.3 * layer_idx)
    return pl.pallas_call(
        functools.partial(_diff_flash_kernel, bk=bk, unroll=unroll, q_scale=LOG2E * DIFF_HALF ** -0.5,
                          out_scale=1.0 - lam_init),
        grid=(b, h, s // bq),
        in_specs=[pl.BlockSpec(memory_space=pltpu.SMEM),
                  pl.BlockSpec((1, 1, bq, d), lambda bi, hi, qi: (bi, hi, qi, 0)),
                  pl.BlockSpec((1, 1, s, d), lambda bi, hi, qi: (bi, hi, 0, 0)),
                  pl.BlockSpec((1, 1, VT_ROWS, s), lambda bi, hi, qi: (bi, hi, 0, 0)),
                  pl.BlockSpec((HEAD_DIM, 1), lambda bi, hi, qi: (0, 0))],
        out_specs=pl.BlockSpec((1, HEAD_DIM, bq), lambda bi, hi, qi: (bi, hi, qi)),
        out_shape=jax.ShapeDtypeStruct((b, h * HEAD_DIM, s), BF16),
        compiler_params=_params("parallel", "parallel", "arbitrary"),
        name="diff_flash",
    )(scalars, q, k, vt, subln.reshape(HEAD_DIM, 1))


def _conv_silu_kernel(x_ref, prev_ref, next_ref, w_ref, o_ref, *, n_tiles):
    si = pl.program_id(1)
    x = x_ref[...]
    bm = x.shape[0]
    r = lax.broadcasted_iota(jnp.int32, (bm, bm), 0)
    c = lax.broadcasted_iota(jnp.int32, (bm, bm), 1)
    shift_dn = jnp.where(r == c + 1, 1.0, 0.0).astype(BF16)
    shift_up = jnp.where(r + 1 == c, 1.0, 0.0).astype(BF16)
    x_prev = jnp.dot(shift_dn, x, preferred_element_type=F32)
    x_next = jnp.dot(shift_up, x, preferred_element_type=F32)
    row = lax.broadcasted_iota(jnp.int32, x.shape, 0)
    halo_prev = jnp.where(si > 0, prev_ref[7:8, :].astype(F32), 0.0)
    halo_next = jnp.where(si < n_tiles - 1, next_ref[0:1, :].astype(F32), 0.0)
    x_prev = jnp.where(row == 0, halo_prev, x_prev)
    x_next = jnp.where(row == bm - 1, halo_next, x_next)
    w = w_ref[...]
    y = x_prev * w[0:1] + x.astype(F32) * w[1:2] + x_next * w[2:3]
    y = y * jax.nn.sigmoid(y)
    col = lax.broadcasted_iota(jnp.int32, x.shape, 1)
    o_ref[...] = jnp.where(col >= GROUP_WIDTH, y * (HEAD_DIM ** -0.5), y).astype(o_ref.dtype)


def _conv_silu(proj, b, s, conv_w, bm=256):
    bm = min(bm, s)
    nt = s // bm
    c = 2 * GROUP_WIDTH
    cb = U_CQK * V7X_LANES // c
    hb = bm // 8
    n8 = b * s // 8
    return pl.pallas_call(
        functools.partial(_conv_silu_kernel, n_tiles=nt),
        grid=(b, nt),
        in_specs=[pl.BlockSpec((bm, c), lambda bi, si: (bi * nt + si, cb)),
                  pl.BlockSpec((8, c), lambda bi, si: (jnp.maximum((bi * nt + si) * hb - 1, 0), cb)),
                  pl.BlockSpec((8, c), lambda bi, si: (jnp.minimum((bi * nt + si + 1) * hb, n8 - 1), cb)),
                  pl.BlockSpec((8, c), lambda bi, si: (0, 0))],
        out_specs=pl.BlockSpec((bm, c), lambda bi, si: (bi * nt + si, 0)),
        out_shape=jax.ShapeDtypeStruct((b * s, c), BF16),
        compiler_params=_params("parallel", "parallel"),
        name="mlstm_conv",
    )(proj, proj, proj, jnp.concatenate([conv_w, jnp.zeros((5, c), conv_w.dtype)], axis=0))


def _log_sigmoid(x):
    return jnp.minimum(x, 0.0) - jnp.log(1.0 + jnp.exp(-jnp.abs(x)))


def _mlstm_chunk(qc, kc, vt1, gcol, grow, caug, m, *, backward):
    L = qc.shape[0]
    d = HEAD_DIM
    li_c, lf_c = gcol[:, 0:1], _log_sigmoid(gcol[:, 1:2])
    li_r, lf_r = grow[0:1, :], _log_sigmoid(grow[1:2, :])
    s_i = lax.broadcasted_iota(jnp.int32, (L, L), 0)
    j_i = lax.broadcasted_iota(jnp.int32, (L, L), 1)
    if backward:
        a_mask, b_mat, valid = j_i <= s_i, s_i >= j_i, s_i >= j_i
        bcum_row_idx, last_lane = L - 1, 0
    else:
        a_mask, b_mat, valid = j_i >= s_i, s_i <= j_i, s_i <= j_i
        bcum_row_idx, last_lane = 0, L - 1
    a = jnp.where(a_mask, lf_r, 0.0)
    a_hi = a.astype(BF16)
    a_lo = (a - a_hi.astype(F32)).astype(BF16)
    ones_b = jnp.where(b_mat, 1.0, 0.0).astype(BF16)
    e = jnp.dot(a_hi, ones_b, preferred_element_type=F32) + jnp.dot(a_lo, ones_b, preferred_element_type=F32)
    bcum_r = e[bcum_row_idx:bcum_row_idx + 1, :]
    dlog = jnp.where(valid, e + (li_c - lf_c), NEG_BIG)
    inter = bcum_r + m
    m_row = jnp.maximum(inter, jnp.max(dlog, axis=0, keepdims=True))
    w_intra = jnp.exp(dlog - m_row)
    w_inter = jnp.exp(inter - m_row)
    st = lax.dot_general(kc, qc, (((1,), (1,)), ((), ())), preferred_element_type=F32)
    at = (st * w_intra).astype(BF16)
    pv = jnp.dot(vt1, at, preferred_element_type=F32)
    cq = lax.dot_general(caug.astype(BF16), qc, (((1,), (1,)), ((), ())), preferred_element_type=F32)
    num = w_inter * cq[:d] + pv[:d]
    den = w_inter * cq[d:d + 1] + pv[d:d + 1]
    h = num / jnp.maximum(jnp.abs(den), jnp.exp(-m_row))
    b_last = bcum_r[:, last_lane:last_lane + 1]
    logw_end = b_last - bcum_r + li_r
    m_new = jnp.maximum(b_last + m, jnp.max(logw_end, axis=1, keepdims=True))
    w_end = jnp.exp(logw_end - m_new)
    decay = jnp.exp(b_last + m - m_new)
    u = jnp.dot((vt1.astype(F32) * w_end).astype(BF16), kc, preferred_element_type=F32)
    caug = decay * caug + u[:d + 8]
    return h, caug, m_new


def _mlstm_kernel(q_ref, k_ref, vt_ref, gc_ref, gr_ref, op_ref, ng_ref, o_ref, hf_sc, hb_sc):
    L = MLSTM_CHUNK
    d = HEAD_DIM
    s = q_ref.shape[2]
    nc = s // L
    ones = jnp.ones((ONES_ROWS, L), BF16)

    def chunk(c, caug, m, backward):
        t0 = pl.multiple_of(c * L, L)
        vt1 = jnp.concatenate([vt_ref[0, :, pl.ds(t0, L)], ones], axis=0)
        gcol = gc_ref[0, 0, pl.ds(t0, L), :]
        grow = gr_ref[0, 0, :, pl.ds(t0, L)]
        if backward:
            gcol, grow = gcol[:, 2:4], grow[2:4, :]
        else:
            gcol, grow = gcol[:, 0:2], grow[0:2, :]
        return _mlstm_chunk(q_ref[0, 0, pl.ds(t0, L), :], k_ref[0, 0, pl.ds(t0, L), :], vt1, gcol, grow,
                            caug, m, backward=backward), t0

    def body(i, carry):
        cf, mf, cb, mb = carry
        (hf, cf, mf), tf = chunk(i, cf, mf, False)
        hf_sc[:, pl.ds(tf, L)] = hf
        (hb, cb, mb), tb = chunk(nc - 1 - i, cb, mb, True)
        hb_sc[:, pl.ds(tb, L)] = hb
        return cf, mf, cb, mb

    c0 = jnp.zeros((d + 8, d), F32)
    m0 = jnp.zeros((1, 1), F32)
    lax.fori_loop(0, nc, body, (c0, m0, c0, m0))
    hsum = hf_sc[...] + hb_sc[...]
    ms = jnp.sum(hsum * hsum, axis=0, keepdims=True) * (1.0 / d)
    hn = hsum * lax.rsqrt(ms + EPS) * ng_ref[0]
    o_ref[0] = (hn * jax.nn.sigmoid(op_ref[0].astype(F32))).astype(o_ref.dtype)


def _mlstm(q, k, vt, gcol, grow, opre_t, norm_g):
    b, h, s, d = q.shape
    return pl.pallas_call(
        _mlstm_kernel,
        grid=(b, h),
        in_specs=[pl.BlockSpec((1, 1, s, d), lambda bi, hi: (bi, hi, 0, 0)),
                  pl.BlockSpec((1, 1, s, d), lambda bi, hi: (bi, hi, 0, 0)),
                  pl.BlockSpec((1, d, s), lambda bi, hi: (bi, hi, 0)),
                  pl.BlockSpec((1, 1, s, 4), lambda bi, hi: (bi, hi, 0, 0)),
                  pl.BlockSpec((1, 1, 4, s), lambda bi, hi: (bi, hi, 0, 0)),
                  pl.BlockSpec((1, d, s), lambda bi, hi: (bi, hi, 0)),
                  pl.BlockSpec((1, d, 1), lambda bi, hi: (hi, 0, 0))],
        out_specs=pl.BlockSpec((1, d, s), lambda bi, hi: (bi, hi, 0)),
        out_shape=jax.ShapeDtypeStruct((b, h * d, s), BF16),
        scratch_shapes=[pltpu.VMEM((d, s), F32), pltpu.VMEM((d, s), F32)],
        compiler_params=_params("parallel", "parallel"),
        name="mlstm_scan",
    )(q, k, vt, gcol, grow, opre_t, norm_g.reshape(h, d, 1))


def _na_kernel(q_ref, k_ref, v_ref, bias_ref, o_ref, *, rows_per_step, n_rows):
    blk = pl.program_id(1)
    win = NA_ROWS * GRID_W
    for i in range(rows_per_step):
        r = blk * rows_per_step + i
        r0 = jnp.clip(r - NA_ROWS // 2, 0, n_rows - NA_ROWS)
        dsel = r - r0
        k0 = pl.multiple_of(r0 * GRID_W, GRID_W)
        for h in range(N_HEADS):
            qh = q_ref[0, h, i * GRID_W:(i + 1) * GRID_W, :]
            kw = k_ref[0, h, pl.ds(k0, win), :]
            vw = v_ref[0, h, pl.ds(k0, win), :]
            sc = lax.dot_general(qh, kw, (((1,), (1,)), ((), ())), preferred_element_type=F32)
            sc = sc * (HEAD_DIM ** -0.5) + bias_ref[h, dsel]
            mx = jnp.max(sc, axis=-1, keepdims=True)
            p = jnp.exp(sc - mx)
            p = (p / jnp.sum(p, axis=-1, keepdims=True)).astype(BF16)
            o_ref[0, h, i * GRID_W:(i + 1) * GRID_W, :] = jnp.dot(
                p, vw, preferred_element_type=F32).astype(o_ref.dtype)


def _na(q, k, v, bias, rows_per_step=8):
    b, h, s, d = q.shape
    n_rows = s // GRID_W
    rows_per_step = min(rows_per_step, n_rows)
    bm = rows_per_step * GRID_W
    return pl.pallas_call(
        functools.partial(_na_kernel, rows_per_step=rows_per_step, n_rows=n_rows),
        grid=(b, n_rows // rows_per_step),
        in_specs=[pl.BlockSpec((1, h, bm, d), lambda bi, ri: (bi, 0, ri, 0)),
                  pl.BlockSpec((1, h, s, d), lambda bi, ri: (bi, 0, 0, 0)),
                  pl.BlockSpec((1, h, s, d), lambda bi, ri: (bi, 0, 0, 0)),
                  pl.BlockSpec(bias.shape, lambda bi, ri: (0, 0, 0, 0))],
        out_specs=pl.BlockSpec((1, h, bm, d), lambda bi, ri: (bi, 0, ri, 0)),
        out_shape=jax.ShapeDtypeStruct((b, h, s, d), BF16),
        compiler_params=_params("parallel", "arbitrary"),
        name="na_attn",
    )(q, k, v, bias)


def _na_bias_table(rpb, n_rows):
    wr = min(NA_ROWS, n_rows)
    cols = np.arange(GRID_W)
    col_start = np.clip(cols - NA_COLS // 2, 0, GRID_W - NA_COLS)
    ck = np.arange(GRID_W)[None, :]
    valid = (ck >= col_start[:, None]) & (ck < col_start[:, None] + NA_COLS)
    crel = ck - cols[:, None] + NA_COLS - 1
    rrel = np.arange(wr)[None, :] - np.arange(wr)[:, None] + NA_ROWS - 1
    c_sel = ((crel[None] == np.arange(2 * NA_COLS - 1)[:, None, None]) & valid[None]).astype(np.float32)
    r_sel = (rrel[None] == np.arange(2 * NA_ROWS - 1)[:, None, None]).astype(np.float32)
    t = jnp.einsum('hab,adw,bqk->hdqwk', rpb.astype(F32), r_sel, c_sel, precision=lax.Precision.HIGHEST)
    t = t + np.where(valid, 0.0, NEG_BIG).astype(np.float32)[None, None, :, None, :]
    return t.reshape(rpb.shape[0], wr, GRID_W, wr * GRID_W)


def _outproj_kernel(ya_ref, yb_ref, yc_ref, yd_ref, x_ref, w_ref, o_ref):
    acc = x_ref[...]
    for g, y_ref in enumerate((ya_ref, yb_ref, yc_ref, yd_ref)):
        acc = acc + lax.dot_general(y_ref[0], w_ref[g], (((0,), (0,)), ((), ())),
                                    preferred_element_type=F32)
    o_ref[...] = acc


def _outproj(ya, yb, yc, yd, x, w, b, s, bm=512):
    bm = min(bm, s)
    nt = s // bm
    d = x.shape[1]
    yspec = pl.BlockSpec((1, GROUP_WIDTH, bm), lambda bi, si: (bi, 0, si))
    return pl.pallas_call(
        _outproj_kernel,
        grid=(b, nt),
        in_specs=[yspec, yspec, yspec, yspec,
                  pl.BlockSpec((bm, d), lambda bi, si: (bi * nt + si, 0)),
                  pl.BlockSpec(w.shape, lambda bi, si: (0, 0, 0))],
        out_specs=pl.BlockSpec((bm, d), lambda bi, si: (bi * nt + si, 0)),
        out_shape=jax.ShapeDtypeStruct(x.shape, F32),
        compiler_params=_params("parallel", "parallel"),
        name="outproj",
    )(ya, yb, yc, yd, x, w)


def _ffn_kernel(x_ref, g_ref, wg_ref, wu_ref, wd_ref, o_ref, hn_sc, acc_sc):
    f = pl.program_id(1)

    @pl.when(f == 0)
    def _():
        x = x_ref[...]
        hn_sc[...] = (_rms(x, x.shape[-1]) * g_ref[...]).astype(BF16)
        acc_sc[...] = jnp.zeros_like(acc_sc)

    hn = hn_sc[...]
    gate = jnp.dot(hn, wg_ref[...], preferred_element_type=F32)
    up = jnp.dot(hn, wu_ref[...], preferred_element_type=F32)
    act = (gate * jax.nn.sigmoid(gate) * up).astype(BF16)
    acc_sc[...] += jnp.dot(act, wd_ref[...], preferred_element_type=F32)

    @pl.when(f == pl.num_programs(1) - 1)
    def _():
        o_ref[...] = x_ref[...] + acc_sc[...]


def _ffn(x, g, wg, wu, wd, bm=512, bf=1408):
    n, d = x.shape
    ff = wg.shape[1]
    bm = min(bm, n)
    return pl.pallas_call(
        _ffn_kernel,
        grid=(n // bm, ff // bf),
        in_specs=[pl.BlockSpec((bm, d), lambda i, f: (i, 0)),
                  pl.BlockSpec((1, d), lambda i, f: (0, 0)),
                  pl.BlockSpec((d, bf), lambda i, f: (0, f)),
                  pl.BlockSpec((d, bf), lambda i, f: (0, f)),
                  pl.BlockSpec((bf, d), lambda i, f: (f, 0))],
        out_specs=pl.BlockSpec((bm, d), lambda i, f: (i, 0)),
        out_shape=jax.ShapeDtypeStruct(x.shape, F32),
        scratch_shapes=[pltpu.VMEM((bm, d), BF16), pltpu.VMEM((bm, d), F32)],
        compiler_params=_params("parallel", "arbitrary"),
        name="ffn",
    )(x, g.reshape(1, d), wg, wu, wd)


R_E1, R_E2, R_W1, R_W2, R_RANK1, R_RANK2 = range(6)


def _lane_pack(lane, cols):
    out = jnp.zeros(lane.shape, F32)
    for idx, col in cols:
        out = out + jnp.where(lane == idx, col, 0.0)
    return out


def _router_kernel(x_ref, g_ref, wr_ref, hn_ref, route_ref, counts_ref, carry_sc):
    @pl.when(pl.program_id(0) == 0)
    def _():
        carry_sc[...] = jnp.zeros_like(carry_sc)

    x = x_ref[...]
    bm = x.shape[0]
    hn = _rms(x, x.shape[-1]) * g_ref[...]
    hn_ref[...] = hn
    logits = jnp.dot(hn, wr_ref[...], preferred_element_type=F32, precision=lax.Precision.HIGHEST)
    lane = lax.broadcasted_iota(jnp.int32, logits.shape, 1)
    logits = jnp.where(lane < N_EXPERTS, logits, NEG_BIG)
    m1 = jnp.max(logits, axis=-1, keepdims=True)
    i1 = jnp.min(jnp.where(logits == m1, lane, V7X_LANES), axis=-1, keepdims=True)
    rest = jnp.where(lane == i1, NEG_BIG, logits)
    m2 = jnp.max(rest, axis=-1, keepdims=True)
    i2 = jnp.min(jnp.where(rest == m2, lane, V7X_LANES), axis=-1, keepdims=True)
    e2 = jnp.exp(m2 - m1)
    w1 = 1.0 / (1.0 + e2)
    w2 = e2 / (1.0 + e2)
    hot1 = jnp.where(lane == i1, 1.0, 0.0)
    hot2 = jnp.where(lane == i2, 1.0, 0.0)
    r_i = lax.broadcasted_iota(jnp.int32, (bm, bm), 0)
    c_i = lax.broadcasted_iota(jnp.int32, (bm, bm), 1)
    below = jnp.where(c_i < r_i, 1.0, 0.0).astype(BF16)
    before1 = jnp.dot(below, hot1.astype(BF16), preferred_element_type=F32)
    before2 = jnp.dot(below, hot2.astype(BF16), preferred_element_type=F32)
    cnt1 = jnp.sum(hot1, axis=0, keepdims=True)
    cnt2 = jnp.sum(hot2, axis=0, keepdims=True)
    carry = carry_sc[...]
    rank1 = jnp.sum(hot1 * (before1 + carry), axis=-1, keepdims=True)
    rank2 = jnp.sum(hot2 * (before2 + carry + cnt1), axis=-1, keepdims=True)
    carry = carry + cnt1 + cnt2
    carry_sc[...] = carry
    counts_ref[...] = carry
    route_ref[...] = _lane_pack(lane, ((R_E1, i1.astype(F32)), (R_E2, i2.astype(F32)), (R_W1, w1), (R_W2, w2),
                                       (R_RANK1, rank1), (R_RANK2, rank2)))


def _router(x, g, w_router, bm=512):
    n, d = x.shape
    bm = min(bm, n)
    wr = jnp.concatenate([w_router, jnp.zeros((d, V7X_LANES - N_EXPERTS), w_router.dtype)], axis=1)
    return pl.pallas_call(
        _router_kernel,
        grid=(n // bm,),
        in_specs=[pl.BlockSpec((bm, d), lambda i: (i, 0)),
                  pl.BlockSpec((1, d), lambda i: (0, 0)),
                  pl.BlockSpec((d, V7X_LANES), lambda i: (0, 0))],
        out_specs=[pl.BlockSpec((bm, d), lambda i: (i, 0)),
                   pl.BlockSpec((bm, V7X_LANES), lambda i: (i, 0)),
                   pl.BlockSpec((1, V7X_LANES), lambda i: (0, 0))],
        out_shape=[jax.ShapeDtypeStruct((n, d), F32),
                   jax.ShapeDtypeStruct((n, V7X_LANES), F32),
                   jax.ShapeDtypeStruct((1, V7X_LANES), F32)],
        scratch_shapes=[pltpu.VMEM((1, V7X_LANES), F32)],
        compiler_params=_params("arbitrary"),
        name="moe_router",
    )(x, g.reshape(1, d), wr)


def _route_plan(route, counts, tm):
    n = route.shape[0]
    counts = counts[0, :N_EXPERTS].astype(jnp.int32)
    padded = ((counts + tm - 1) // tm) * tm
    g_end = jnp.cumsum(padded)
    g_start = (g_end - padded).astype(F32)
    experts = jnp.arange(N_EXPERTS, dtype=F32)[None, :]
    start1 = jnp.sum(jnp.where(route[:, R_E1:R_E1 + 1] == experts, g_start[None, :], 0.0), axis=1)
    start2 = jnp.sum(jnp.where(route[:, R_E2:R_E2 + 1] == experts, g_start[None, :], 0.0), axis=1)
    pos = jnp.stack([start1 + route[:, R_RANK1], start2 + route[:, R_RANK2]], axis=1).astype(jnp.int32)
    n_tiles = 2 * n // tm + N_EXPERTS
    tile_start = jnp.arange(n_tiles, dtype=jnp.int32) * tm
    tile_expert = jnp.minimum(jnp.sum(tile_start[:, None] >= g_end[None, :], axis=1), N_EXPERTS - 1)
    n_used = (g_end[-1:] // tm).astype(jnp.int32)
    return pos.reshape(-1), tile_expert.astype(jnp.int32), n_used, n_tiles


def _row_copies(pos_ref, r, src_at, dst_at, sem):
    copies = []
    for c in range(2):
        p = pos_ref[2 * r + c]
        copies.append(pltpu.make_async_copy(src_at(r, c, p), dst_at(r, c, p), sem))
    return copies


def _move_rows(pos_ref, n_rows, src_at, dst_at, sem):
    def start(r, carry):
        for cp in _row_copies(pos_ref, r, src_at, dst_at, sem):
            cp.start()
        return carry

    def wait(r, carry):
        for cp in _row_copies(pos_ref, r, src_at, dst_at, sem):
            cp.wait()
        return carry

    lax.fori_loop(0, n_rows, start, 0, unroll=8)
    lax.fori_loop(0, n_rows, wait, 0, unroll=8)


def _dispatch_kernel(pos_ref, hn_ref, xg_in_ref, xg_ref, sem):
    del xg_in_ref
    _move_rows(pos_ref, hn_ref.shape[0],
               lambda r, c, p: hn_ref.at[pl.ds(r, 1)],
               lambda r, c, p: xg_ref.at[pl.ds(p, 1)], sem)


def _dispatch(pos, hn, n_rows, bm=512):
    n, d = hn.shape
    bm = min(bm, n)
    return pl.pallas_call(
        _dispatch_kernel,
        grid=(n // bm,),
        in_specs=[pl.BlockSpec((2 * bm,), lambda i: (i,), memory_space=pltpu.SMEM),
                  pl.BlockSpec((bm, d), lambda i: (i, 0)),
                  pl.BlockSpec(memory_space=pl.ANY)],
        out_specs=pl.BlockSpec(memory_space=pl.ANY),
        out_shape=jax.ShapeDtypeStruct((n_rows, d), F32),
        scratch_shapes=[pltpu.SemaphoreType.DMA(())],
        input_output_aliases={2: 0},
        compiler_params=_params("arbitrary"),
        name="moe_dispatch",
    )(pos, hn, jnp.zeros((n_rows, d), F32))


def _expert_ffn_kernel(te_ref, nu_ref, xg_ref, wg_ref, wu_ref, wd_ref, o_ref, hn_sc, acc_sc):
    del te_ref
    t = pl.program_id(0)
    f = pl.program_id(1)
    last = pl.num_programs(1) - 1
    used = t < nu_ref[0]

    @pl.when(used & (f == 0))
    def _():
        hn_sc[...] = xg_ref[...].astype(BF16)
        acc_sc[...] = jnp.zeros_like(acc_sc)

    @pl.when(used)
    def _():
        hn = hn_sc[...]
        gate = jnp.dot(hn, wg_ref[0], preferred_element_type=F32)
        up = jnp.dot(hn, wu_ref[0], preferred_element_type=F32)
        act = (gate * jax.nn.sigmoid(gate) * up).astype(BF16)
        acc_sc[...] += jnp.dot(act, wd_ref[0], preferred_element_type=F32)

    @pl.when(used & (f == last))
    def _():
        o_ref[...] = acc_sc[...]

    @pl.when(jnp.logical_not(used) & (f == last))
    def _():
        o_ref[...] = jnp.zeros_like(o_ref)


def _expert_ffn(tile_expert, n_used, xg, wg, wu, wd, tm, bf=896):
    rows, d = xg.shape
    ff = wg.shape[2]
    grid_spec = pltpu.PrefetchScalarGridSpec(
        num_scalar_prefetch=2,
        grid=(rows // tm, ff // bf),
        in_specs=[pl.BlockSpec((tm, d), lambda t, f, te, nu: (t, 0)),
                  pl.BlockSpec((1, d, bf), lambda t, f, te, nu: (te[t], 0, f)),
                  pl.BlockSpec((1, d, bf), lambda t, f, te, nu: (te[t], 0, f)),
                  pl.BlockSpec((1, bf, d), lambda t, f, te, nu: (te[t], f, 0))],
        out_specs=pl.BlockSpec((tm, d), lambda t, f, te, nu: (t, 0)),
        scratch_shapes=[pltpu.VMEM((tm, d), BF16), pltpu.VMEM((tm, d), F32)])
    return pl.pallas_call(
        _expert_ffn_kernel,
        grid_spec=grid_spec,
        out_shape=jax.ShapeDtypeStruct((rows, d), F32),
        compiler_params=_params("arbitrary", "arbitrary"),
        name="moe_ffn",
    )(tile_expert, n_used, xg, wg, wu, wd)


def _combine_kernel(pos_ref, x_ref, route_ref, g_ref, yg_ref, o_ref, buf, sem, *, final_norm):
    _move_rows(pos_ref, x_ref.shape[0],
               lambda r, c, p: yg_ref.at[pl.ds(p, 1)],
               lambda r, c, p: buf.at[c, pl.ds(r, 1)], sem)
    route = route_ref[...]
    y = x_ref[...] + route[:, R_W1:R_W1 + 1] * buf[0] + route[:, R_W2:R_W2 + 1] * buf[1]
    if final_norm:
        y = _rms(y, y.shape[-1]) * g_ref[...]
    o_ref[...] = y


def _combine(pos, x, route, yg, g_final, final_norm, bm=512):
    n, d = x.shape
    bm = min(bm, n)
    return pl.pallas_call(
        functools.partial(_combine_kernel, final_norm=final_norm),
        grid=(n // bm,),
        in_specs=[pl.BlockSpec((2 * bm,), lambda i: (i,), memory_space=pltpu.SMEM),
                  pl.BlockSpec((bm, d), lambda i: (i, 0)),
                  pl.BlockSpec((bm, V7X_LANES), lambda i: (i, 0)),
                  pl.BlockSpec((1, d), lambda i: (0, 0)),
                  pl.BlockSpec(memory_space=pl.ANY)],
        out_specs=pl.BlockSpec((bm, d), lambda i: (i, 0)),
        out_shape=jax.ShapeDtypeStruct(x.shape, F32),
        scratch_shapes=[pltpu.VMEM((2, bm, d), F32), pltpu.SemaphoreType.DMA(())],
        compiler_params=_params("arbitrary"),
        name="moe_combine",
    )(pos, x, route, g_final.reshape(1, d), yg)


def _moe(x, g, w_router, wg, wu, wd, g_final, final_norm, tm=512):
    hn, route, counts = _router(x, g, w_router)
    pos, tile_expert, n_used, n_tiles = _route_plan(route, counts, tm)
    xg = _dispatch(pos, hn, n_tiles * tm)
    yg = _expert_ffn(tile_expert, n_used, xg, wg, wu, wd, tm)
    return _combine(pos, x, route, yg, g_final, final_norm)


def _final_norm_kernel(x_ref, g_ref, o_ref):
    x = x_ref[...]
    o_ref[...] = _rms(x, x.shape[-1]) * g_ref[...]


def _final_norm(x, g, bm=1024):
    n, d = x.shape
    bm = min(bm, n)
    return pl.pallas_call(
        _final_norm_kernel,
        grid=(n // bm,),
        in_specs=[pl.BlockSpec((bm, d), lambda i: (i, 0)), pl.BlockSpec((1, d), lambda i: (0, 0))],
        out_specs=pl.BlockSpec((bm, d), lambda i: (i, 0)),
        out_shape=jax.ShapeDtypeStruct(x.shape, F32),
        compiler_params=_params("parallel"),
        name="final_norm",
    )(x, g.reshape(1, d))


def _heads(t, b, s):
    return t.reshape(b, s, N_HEADS, HEAD_DIM).transpose(0, 2, 1, 3)


def _chan_major(t, b, s):
    return t.reshape(b, s, t.shape[-1]).transpose(0, 2, 1)


def _with_ones(vt):
    b, _, s = vt.shape
    vt = vt.reshape(b, N_HEADS, HEAD_DIM, s)
    return jnp.concatenate([vt, jnp.ones((b, N_HEADS, ONES_ROWS, s), vt.dtype)], axis=2)


def _cols(proj, unit, n_units):
    return proj[:, unit * V7X_LANES:(unit + n_units) * V7X_LANES]


def _token_mix(x, i, b, s, p):
    w_main, w_gates = _prep_w_in(p['w_in'][i])
    proj, gates = _inproj(x, p['norm_mix'][i], w_main, w_gates)

    cos_t, sin_t = _rope_tables(s)
    q_a, k_a, vt_a = _mla_prep(proj, b, s, cos_t, sin_t, *_prep_mla_weights(
        p['mla_q_norm'][i], p['mla_kv_norm'][i], p['mla_w_uq'][i], p['mla_w_ukv'][i]))
    y_a = _mla_flash(q_a, k_a, vt_a)

    lp = p['diff_lambda'][i].astype(F32)
    lam_init = 0.8 - 0.6 * math.exp(-0.3 * i)
    lam = jnp.exp(jnp.sum(lp[0] * lp[1])) - jnp.exp(jnp.sum(lp[2] * lp[3])) + lam_init
    slopes = 2.0 ** (-8.0 * jnp.arange(1, N_HEADS + 1, dtype=F32) / N_HEADS)
    scalars = jnp.concatenate([-slopes * LOG2E, lam[None]]).astype(F32)
    q_b = _heads(_cols(proj, U_BQ, 2), b, s)
    k_b = _heads(_cols(proj, U_BK, 2), b, s)
    pad = jnp.zeros(q_b.shape[:-1] + (V7X_LANES - HEAD_DIM - 1,), BF16)
    q_b = jnp.concatenate([q_b, jnp.zeros_like(q_b)], axis=-1)
    k_b = jnp.concatenate([k_b, jnp.ones(k_b.shape[:-1] + (1,), BF16), pad], axis=-1)
    y_b = _diff_flash(scalars, q_b, k_b, _with_ones(_chan_major(_cols(proj, U_BV, 2), b, s)),
                      p['diff_subln'][i], i)

    qk = _conv_silu(proj, b, s, p['mlstm_conv'][i])
    g = (gates[:, :4 * N_HEADS] + p['mlstm_gate_bias'][i][None, :]).reshape(b, s, 4, N_HEADS)
    y_c = _mlstm(_heads(qk[:, :GROUP_WIDTH], b, s), _heads(qk[:, GROUP_WIDTH:], b, s),
                 _chan_major(_cols(proj, U_CV, 2), b, s),
                 g.transpose(0, 3, 1, 2), g.transpose(0, 3, 2, 1),
                 _chan_major(_cols(proj, U_CO, 2), b, s), p['mlstm_norm'][i])

    y_d = _na(_heads(_cols(proj, U_DQ, 2), b, s), _heads(_cols(proj, U_DK, 2), b, s),
              _heads(_cols(proj, U_DV, 2), b, s), _na_bias_table(p['na_rpb'][i], s // GRID_W))
    y_d = y_d.transpose(0, 1, 3, 2).reshape(b, GROUP_WIDTH, s)

    w_out = p['w_out'][i].reshape(4, GROUP_WIDTH, -1).astype(BF16)
    return _outproj(y_a, y_b, y_c, y_d, x, w_out, b, s)


def _trunk(x, p, depth):
    b, s, d = x.shape
    x = x.reshape(b * s, d)
    for i in range(depth):
        x = _token_mix(x, i, b, s, p)
        j = i // 2
        if i % 2 == 0:
            x = _ffn(x, p['norm_ffn'][i], p['ffn_w_gate'][j].astype(BF16), p['ffn_w_up'][j].astype(BF16),
                     p['ffn_w_down'][j].astype(BF16))
        else:
            x = _moe(x, p['norm_ffn'][i], p['moe_router'][j], p['moe_w_gate'][j].astype(BF16),
                     p['moe_w_up'][j].astype(BF16), p['moe_w_down'][j].astype(BF16),
                     p['norm_final'], final_norm=(i == depth - 1))
    if depth % 2 == 1:
        x = _final_norm(x, p['norm_final'])
    return x.reshape(b, s, d)


def kernel(x_prompt, x_sample, norm_mix, norm_ffn, w_in, w_out, mla_q_norm, mla_kv_norm, mla_w_uq, mla_w_ukv,
           diff_lambda, diff_subln, mlstm_conv, mlstm_gate_bias, mlstm_norm, na_rpb, ffn_w_gate, ffn_w_up,
           ffn_w_down, moe_router, moe_w_gate, moe_w_up, moe_w_down, norm_final):
    p = dict(norm_mix=norm_mix, norm_ffn=norm_ffn, w_in=w_in, w_out=w_out, mla_q_norm=mla_q_norm,
             mla_kv_norm=mla_kv_norm, mla_w_uq=mla_w_uq, mla_w_ukv=mla_w_ukv, diff_lambda=diff_lambda,
             diff_subln=diff_subln, mlstm_conv=mlstm_conv, mlstm_gate_bias=mlstm_gate_bias,
             mlstm_norm=mlstm_norm, na_rpb=na_rpb, ffn_w_gate=ffn_w_gate, ffn_w_up=ffn_w_up,
             ffn_w_down=ffn_w_down, moe_router=moe_router, moe_w_gate=moe_w_gate, moe_w_up=moe_w_up,
             moe_w_down=moe_w_down, norm_final=norm_final)
    depth = norm_mix.shape[0]
    nb = x_prompt.shape[0]
    y = _trunk(jnp.concatenate([x_prompt, x_sample], axis=0), p, depth)
    return (y[:nb], y[nb:])
```

```python
import functools
import math

import numpy as np
import jax
import jax.numpy as jnp
from jax import lax
from jax.experimental import pallas as pl
from jax.experimental.pallas import tpu as pltpu

F32 = jnp.float32
BF16 = jnp.bfloat16

V7X_LANES = 128
V7X_VMEM_LIMIT_BYTES = 56 * 1024 * 1024

EPS = 1e-6
LOG2E = 1.4426950408889634
NEG_BIG = -1e30

HEAD_DIM = 64
N_HEADS = 4
GROUP_WIDTH = 256
MLA_Q_LORA = 192
MLA_KV_LORA = 128
MLA_NOPE = 64
MLA_ROPE = 32
ROPE_THETA = 10000.0
DIFF_HALF = 32
MLSTM_CHUNK = 128
GRID_W = 64
NA_ROWS = 8
NA_COLS = 16
N_EXPERTS = 8
ONES_ROWS = 16
VT_ROWS = HEAD_DIM + ONES_ROWS
MLA_REF_LANE = MLA_NOPE + MLA_ROPE
DIFF_REF_LANE = HEAD_DIM

IN_SECTIONS = (192, 128, 32, 256, 256, 256, 512, 256, 256, 16, 256, 256, 256)
U_BQ, U_BK, U_BV, U_DQ, U_DK, U_DV, U_CQK, U_CV, U_CO, U_ACQ, U_ACKV, U_AKR, U_AKRR = (
    0, 2, 4, 6, 8, 10, 12, 16, 18, 20, 22, 23, 24)
PROJ_COLS = 25 * V7X_LANES


def _params(*sem):
    return pltpu.CompilerParams(dimension_semantics=sem, vmem_limit_bytes=V7X_VMEM_LIMIT_BYTES)


def _rms(x, n):
    return x * lax.rsqrt(jnp.sum(x * x, axis=-1, keepdims=True) * (1.0 / n) + EPS)


def _inproj_kernel(x_ref, g_ref, w_ref, wg_ref, o_ref, og_ref, *, col_chunk):
    x = x_ref[...]
    hn = (_rms(x, x.shape[-1]) * g_ref[...]).astype(BF16)
    for c in range(0, o_ref.shape[1], col_chunk):
        w = min(col_chunk, o_ref.shape[1] - c)
        o_ref[:, c:c + w] = jnp.dot(hn, w_ref[:, c:c + w], preferred_element_type=F32).astype(BF16)
    og_ref[...] = jnp.dot(hn, wg_ref[...], preferred_element_type=F32)


def _inproj(x, g, w_main, w_gates, bm=512):
    n, d = x.shape
    bm = min(bm, n)
    return pl.pallas_call(
        functools.partial(_inproj_kernel, col_chunk=640),
        grid=(n // bm,),
        in_specs=[pl.BlockSpec((bm, d), lambda i: (i, 0)),
                  pl.BlockSpec((1, d), lambda i: (0, 0)),
                  pl.BlockSpec(w_main.shape, lambda i: (0, 0)),
                  pl.BlockSpec(w_gates.shape, lambda i: (0, 0))],
        out_specs=[pl.BlockSpec((bm, PROJ_COLS), lambda i: (i, 0)),
                   pl.BlockSpec((bm, V7X_LANES), lambda i: (i, 0))],
        out_shape=[jax.ShapeDtypeStruct((n, PROJ_COLS), BF16),
                   jax.ShapeDtypeStruct((n, V7X_LANES), F32)],
        compiler_params=_params("parallel"),
        name="inproj",
    )(x, g.reshape(1, d), w_main, w_gates)


def _prep_w_in(w):
    d = w.shape[0]
    offs = np.cumsum((0,) + IN_SECTIONS)
    (a_cq, a_ckv, a_kr, b_q, b_k, b_v, c_qk, c_v, c_o, c_g, d_q, d_k, d_v) = [
        w[:, offs[i]:offs[i + 1]] for i in range(len(IN_SECTIONS))]
    z = lambda k: jnp.zeros((d, k), w.dtype)
    half = MLA_ROPE // 2
    a_kr_rot = jnp.concatenate([a_kr[:, half:], a_kr[:, :half]], axis=1)
    main = jnp.concatenate([b_q, b_k, b_v, d_q, d_k, d_v, c_qk, c_v, c_o,
                            a_cq, z(64), a_ckv,
                            z(64), a_kr, z(32),
                            z(64), a_kr_rot, z(32)], axis=1)
    gates = jnp.concatenate([c_g, z(V7X_LANES - 16)], axis=1)
    return main.astype(BF16), gates.astype(BF16)


def _mla_prep_kernel(cq_ref, ckv_ref, kr_ref, krr_ref, cos_ref, sin_ref, gq_ref, gkv_ref,
                     wq_ref, wqr_ref, wkn_ref, wvt_ref, ones_ref, q_out, k_out, vt_out, *, q_scale):
    cos = cos_ref[...]
    sin = sin_ref[...]
    qn = (_rms(cq_ref[...].astype(F32), MLA_Q_LORA) * gq_ref[...]).astype(BF16)
    qa = jnp.dot(qn, wq_ref[...], preferred_element_type=F32)
    qr = jnp.dot(qn, wqr_ref[...], preferred_element_type=F32)
    kvn = (_rms(ckv_ref[...].astype(F32), MLA_KV_LORA) * gkv_ref[...]).astype(BF16)
    kn = jnp.dot(kvn, wkn_ref[...], preferred_element_type=F32)
    k_rope = kr_ref[...].astype(F32) * cos + krr_ref[...].astype(F32) * sin
    k_rope = jnp.where(lax.broadcasted_iota(jnp.int32, k_rope.shape, 1) == MLA_REF_LANE, 1.0, k_rope)
    for h in range(N_HEADS):
        sl = slice(h * V7X_LANES, (h + 1) * V7X_LANES)
        q_h = ((qa[:, sl] * cos + qr[:, sl] * sin) * q_scale).astype(BF16)
        k_h = (kn[:, sl] + k_rope).astype(BF16)
        q_out[0, h] = _with_ref_column(q_h, _diag_ref_max(q_h, k_h), MLA_REF_LANE)
        k_out[0, h] = k_h
        vt = lax.dot_general(wvt_ref[h], kvn, (((1,), (1,)), ((), ())), preferred_element_type=F32)
        vt_out[0, h] = (vt + ones_ref[...]).astype(BF16)


def _mla_prep(proj, b, s, cos_t, sin_t, gq, gkv, wq, wqr, wkn, wvt, bm=512):
    bm = min(bm, s)
    nt = s // bm
    row = lambda bi, si: bi * nt + si
    ones_col = jnp.concatenate([jnp.zeros((HEAD_DIM, 1), F32), jnp.ones((ONES_ROWS, 1), F32)], axis=0)
    full = lambda a: pl.BlockSpec(a.shape, lambda bi, si: (0,) * a.ndim)
    return pl.pallas_call(
        functools.partial(_mla_prep_kernel, q_scale=LOG2E * (MLA_NOPE + MLA_ROPE) ** -0.5),
        grid=(b, nt),
        in_specs=[pl.BlockSpec((bm, 2 * V7X_LANES), lambda bi, si: (row(bi, si), U_ACQ // 2)),
                  pl.BlockSpec((bm, V7X_LANES), lambda bi, si: (row(bi, si), U_ACKV)),
                  pl.BlockSpec((bm, V7X_LANES), lambda bi, si: (row(bi, si), U_AKR)),
                  pl.BlockSpec((bm, V7X_LANES), lambda bi, si: (row(bi, si), U_AKRR)),
                  pl.BlockSpec((bm, V7X_LANES), lambda bi, si: (si, 0)),
                  pl.BlockSpec((bm, V7X_LANES), lambda bi, si: (si, 0)),
                  full(gq), full(gkv), full(wq), full(wqr), full(wkn), full(wvt), full(ones_col)],
        out_specs=[pl.BlockSpec((1, N_HEADS, bm, V7X_LANES), lambda bi, si: (bi, 0, si, 0)),
                   pl.BlockSpec((1, N_HEADS, bm, V7X_LANES), lambda bi, si: (bi, 0, si, 0)),
                   pl.BlockSpec((1, N_HEADS, VT_ROWS, bm), lambda bi, si: (bi, 0, 0, si))],
        out_shape=[jax.ShapeDtypeStruct((b, N_HEADS, s, V7X_LANES), BF16),
                   jax.ShapeDtypeStruct((b, N_HEADS, s, V7X_LANES), BF16),
                   jax.ShapeDtypeStruct((b, N_HEADS, VT_ROWS, s), BF16)],
        compiler_params=_params("parallel", "parallel"),
        name="mla_prep",
    )(proj, proj, proj, proj, cos_t, sin_t, gq, gkv, wq, wqr, wkn, wvt, ones_col)


def _prep_mla_weights(q_norm, kv_norm, w_uq, w_ukv):
    half = MLA_ROPE // 2
    dq = MLA_NOPE + MLA_ROPE
    wq_h = w_uq.reshape(MLA_Q_LORA, N_HEADS, dq)
    zq = lambda k: jnp.zeros((MLA_Q_LORA, N_HEADS, k), w_uq.dtype)
    rope_cols = wq_h[:, :, MLA_NOPE:]
    rope_rot = jnp.concatenate([rope_cols[:, :, half:], rope_cols[:, :, :half]], axis=2)
    wq = jnp.concatenate([wq_h, zq(V7X_LANES - dq)], axis=2).reshape(MLA_Q_LORA, N_HEADS * V7X_LANES)
    wqr = jnp.concatenate([zq(MLA_NOPE), rope_rot, zq(V7X_LANES - dq)], axis=2).reshape(
        MLA_Q_LORA, N_HEADS * V7X_LANES)
    pad_rows = jnp.zeros((2 * V7X_LANES - MLA_Q_LORA, N_HEADS * V7X_LANES), w_uq.dtype)
    wq = jnp.concatenate([wq, pad_rows], axis=0).astype(BF16)
    wqr = jnp.concatenate([wqr, pad_rows], axis=0).astype(BF16)
    wkv_h = w_ukv.reshape(MLA_KV_LORA, N_HEADS, MLA_NOPE + HEAD_DIM)
    wkn = jnp.concatenate([wkv_h[:, :, :MLA_NOPE],
                           jnp.zeros((MLA_KV_LORA, N_HEADS, V7X_LANES - MLA_NOPE), w_ukv.dtype)],
                          axis=2).reshape(MLA_KV_LORA, N_HEADS * V7X_LANES).astype(BF16)
    wvt = jnp.transpose(wkv_h[:, :, MLA_NOPE:], (1, 2, 0))
    wvt = jnp.concatenate([wvt, jnp.zeros((N_HEADS, ONES_ROWS, MLA_KV_LORA), w_ukv.dtype)],
                          axis=1).astype(BF16)
    gq = jnp.concatenate([q_norm, jnp.zeros((2 * V7X_LANES - MLA_Q_LORA,), q_norm.dtype)]).reshape(1, -1)
    gkv = kv_norm.reshape(1, -1)
    return gq, gkv, wq, wqr, wkn, wvt


def _rope_tables(s):
    half = MLA_ROPE // 2
    inv = ROPE_THETA ** (-jnp.arange(half, dtype=F32) / half)
    ang = jnp.arange(s).astype(F32)[:, None] * inv[None, :]
    cos, sin = jnp.cos(ang), jnp.sin(ang)
    ones = jnp.ones((s, MLA_NOPE), F32)
    z = lambda k: jnp.zeros((s, k), F32)
    pad = V7X_LANES - MLA_NOPE - MLA_ROPE
    cos_t = jnp.concatenate([ones, cos, cos, z(pad)], axis=1)
    sin_t = jnp.concatenate([z(MLA_NOPE), -sin, sin, z(pad)], axis=1)
    return cos_t, sin_t


_NT = (((1,), (1,)), ((), ()))


def _diag_ref_max(q, k, bias=None):
    cols = []
    for j in range(q.shape[0] // V7X_LANES):
        rows = slice(j * V7X_LANES, (j + 1) * V7X_LANES)
        sc = lax.dot_general(q[rows], k[rows], _NT, preferred_element_type=F32)
        if bias is not None:
            sc = sc + bias
        cols.append(jnp.max(sc, axis=1, keepdims=True))
    return jnp.concatenate(cols, axis=0)


def _with_ref_column(q, m_col, ref_lane):
    lane = lax.broadcasted_iota(jnp.int32, q.shape, 1)
    return jnp.where(lane == ref_lane, -m_col, q.astype(F32)).astype(BF16)


def _without_ref_column(q_aug, ref_lane):
    lane = lax.broadcasted_iota(jnp.int32, q_aug.shape, 1)
    return jnp.where(lane == ref_lane, 0.0, q_aug.astype(F32)).astype(BF16)


def _not_finite(acc):
    return jnp.max(jnp.where(jnp.isfinite(acc), 0.0, 1.0)) > 0.0


def _online_block(k_blk, q, vt_blk, m, acc, bias=None):
    st = lax.dot_general(k_blk, q, _NT, preferred_element_type=F32)
    if bias is not None:
        st = st + bias
    m_new = jnp.maximum(m, jnp.max(st, axis=0, keepdims=True))
    p = jnp.exp2(st - m_new).astype(BF16)
    return m_new, acc * jnp.exp2(m - m_new) + jnp.dot(vt_blk, p, preferred_element_type=F32)


def _mla_flash_kernel(q_ref, k_ref, vt_ref, o_ref, *, bk, unroll):
    q_aug = q_ref[0, 0]
    bq = q_aug.shape[0]
    s_len = k_ref.shape[2]

    def body(i, acc):
        for u in range(unroll):
            k0 = pl.multiple_of((i * unroll + u) * bk, bk)
            st = lax.dot_general(k_ref[0, 0, pl.ds(k0, bk), :], q_aug, _NT, preferred_element_type=F32)
            acc = acc + jnp.dot(vt_ref[0, 0, :, pl.ds(k0, bk)], jnp.exp2(st).astype(BF16),
                                preferred_element_type=F32)
        return acc

    acc0 = jnp.zeros((VT_ROWS, bq), F32)
    acc = lax.fori_loop(0, s_len // (bk * unroll), body, acc0)
    o_ref[0] = (acc[:HEAD_DIM] / acc[HEAD_DIM:HEAD_DIM + 1]).astype(o_ref.dtype)

    @pl.when(_not_finite(acc))
    def _():
        q = _without_ref_column(q_aug, MLA_REF_LANE)

        def exact(i, carry):
            k0 = pl.multiple_of(i * bk, bk)
            return _online_block(k_ref[0, 0, pl.ds(k0, bk), :], q, vt_ref[0, 0, :, pl.ds(k0, bk)], *carry)

        _, acc_x = lax.fori_loop(0, s_len // bk, exact, (jnp.full((1, bq), NEG_BIG, F32), acc0))
        o_ref[0] = (acc_x[:HEAD_DIM] / acc_x[HEAD_DIM:HEAD_DIM + 1]).astype(o_ref.dtype)


def _mla_flash(q, k, vt, bq=512, bk=2048, unroll=4):
    b, h, s, dq = q.shape
    bq, bk = min(bq, s), min(bk, s)
    unroll = min(unroll, s // bk)
    return pl.pallas_call(
        functools.partial(_mla_flash_kernel, bk=bk, unroll=unroll),
        grid=(b, h, s // bq),
        in_specs=[pl.BlockSpec((1, 1, bq, dq), lambda bi, hi, qi: (bi, hi, qi, 0)),
                  pl.BlockSpec((1, 1, s, dq), lambda bi, hi, qi: (bi, hi, 0, 0)),
                  pl.BlockSpec((1, 1, VT_ROWS, s), lambda bi, hi, qi: (bi, hi, 0, 0))],
        out_specs=pl.BlockSpec((1, HEAD_DIM, bq), lambda bi, hi, qi: (bi, hi, qi)),
        out_shape=jax.ShapeDtypeStruct((b, h * HEAD_DIM, s), BF16),
        compiler_params=_params("parallel", "parallel", "arbitrary"),
        name="mla_flash",
    )(q, k, vt)


def _lane_selector(n_src, n_dst, src0, dst0, width, transposed=False):
    shape = (n_dst, n_src) if transposed else (n_src, n_dst)
    src = lax.broadcasted_iota(jnp.int32, shape, 1 if transposed else 0)
    dst = lax.broadcasted_iota(jnp.int32, shape, 0 if transposed else 1)
    hit = (src - src0 == dst - dst0) & (dst >= dst0) & (dst < dst0 + width)
    return jnp.where(hit, 1.0, 0.0).astype(BF16)


def _diff_prep_kernel(q_ref, k_ref, v_ref, qa_out, k_out, vt_out, *, q_scale, slopes):
    q_tile, k_tile, v_tile = q_ref[...], k_ref[...], v_ref[...]
    lane = lax.broadcasted_iota(jnp.int32, (q_tile.shape[0], V7X_LANES), 1)
    diag = jnp.abs((lax.broadcasted_iota(jnp.int32, (V7X_LANES, V7X_LANES), 0)
                    - lax.broadcasted_iota(jnp.int32, (V7X_LANES, V7X_LANES), 1)).astype(F32))
    ones_rows = jnp.where(lax.broadcasted_iota(jnp.int32, (VT_ROWS, 1), 0) >= HEAD_DIM, 1.0, 0.0)
    for h in range(N_HEADS):
        c0 = h * HEAD_DIM
        k_h = jnp.dot(k_tile, _lane_selector(GROUP_WIDTH, V7X_LANES, c0, 0, HEAD_DIM), preferred_element_type=F32)
        k_h = jnp.where(lane == DIFF_REF_LANE, 1.0, k_h).astype(BF16)
        k_out[0, h] = k_h
        halves = []
        for part in range(2):
            sel = _lane_selector(GROUP_WIDTH, V7X_LANES, c0 + part * DIFF_HALF, part * DIFF_HALF, DIFF_HALF)
            q_p = (jnp.dot(q_tile, sel, preferred_element_type=F32) * q_scale).astype(BF16)
            halves.append(_with_ref_column(q_p, _diag_ref_max(q_p, k_h, diag * slopes[h]), DIFF_REF_LANE))
        qa_out[0, h] = jnp.concatenate(halves, axis=1)
        vt = lax.dot_general(_lane_selector(GROUP_WIDTH, VT_ROWS, c0, 0, HEAD_DIM, transposed=True), v_tile, _NT,
                             preferred_element_type=F32)
        vt_out[0, h] = (vt + ones_rows).astype(BF16)


def _diff_prep(proj, b, s, bm=512):
    bm = min(bm, s)
    nt = s // bm
    slopes = tuple(-LOG2E * 2.0 ** (-8.0 * (h + 1) / N_HEADS) for h in range(N_HEADS))
    spec = lambda unit: pl.BlockSpec((bm, GROUP_WIDTH), lambda bi, si: (bi * nt + si, unit // 2))
    return pl.pallas_call(
        functools.partial(_diff_prep_kernel, q_scale=LOG2E * DIFF_HALF ** -0.5, slopes=slopes),
        grid=(b, nt),
        in_specs=[spec(U_BQ), spec(U_BK), spec(U_BV)],
        out_specs=[pl.BlockSpec((1, N_HEADS, bm, 2 * V7X_LANES), lambda bi, si: (bi, 0, si, 0)),
                   pl.BlockSpec((1, N_HEADS, bm, V7X_LANES), lambda bi, si: (bi, 0, si, 0)),
                   pl.BlockSpec((1, N_HEADS, VT_ROWS, bm), lambda bi, si: (bi, 0, 0, si))],
        out_shape=[jax.ShapeDtypeStruct((b, N_HEADS, s, 2 * V7X_LANES), BF16),
                   jax.ShapeDtypeStruct((b, N_HEADS, s, V7X_LANES), BF16),
                   jax.ShapeDtypeStruct((b, N_HEADS, VT_ROWS, s), BF16)],
        compiler_params=_params("parallel", "parallel"),
        name="diff_prep",
    )(proj, proj, proj)


def _diff_flash_kernel(sc_ref, q_ref, k_ref, vt_ref, rel_ref, g_ref, o_ref, *, bk, unroll, out_scale):
    hi = pl.program_id(1)
    qi = pl.program_id(2)
    slope = sc_ref[hi]
    lam = sc_ref[N_HEADS]
    q1a = q_ref[0, 0, :, :V7X_LANES]
    q2a = q_ref[0, 0, :, V7X_LANES:]
    bq = q1a.shape[0]
    s_len = k_ref.shape[2]
    rel = rel_ref[...]
    q0 = (qi * bq).astype(F32)

    def block_inputs(i):
        k0 = pl.multiple_of(i * bk, bk)
        bias = jnp.abs(rel + (k0.astype(F32) - q0)) * slope
        return k_ref[0, 0, pl.ds(k0, bk), :], vt_ref[0, 0, :, pl.ds(k0, bk)], bias

    def body(i, carry):
        a1, a2 = carry
        for u in range(unroll):
            k_blk, vt_blk, bias = block_inputs(i * unroll + u)
            s1 = lax.dot_general(k_blk, q1a, _NT, preferred_element_type=F32) + bias
            a1 = a1 + jnp.dot(vt_blk, jnp.exp2(s1).astype(BF16), preferred_element_type=F32)
            s2 = lax.dot_general(k_blk, q2a, _NT, preferred_element_type=F32) + bias
            a2 = a2 + jnp.dot(vt_blk, jnp.exp2(s2).astype(BF16), preferred_element_type=F32)
        return a1, a2

    def finish(a1, a2):
        o = a1[:HEAD_DIM] / a1[HEAD_DIM:HEAD_DIM + 1] - lam * (a2[:HEAD_DIM] / a2[HEAD_DIM:HEAD_DIM + 1])
        ms = jnp.sum(o * o, axis=0, keepdims=True) * (1.0 / HEAD_DIM)
        o_ref[0] = (o * lax.rsqrt(ms + EPS) * g_ref[...] * out_scale).astype(o_ref.dtype)

    acc0 = jnp.zeros((VT_ROWS, bq), F32)
    a1, a2 = lax.fori_loop(0, s_len // (bk * unroll), body, (acc0, acc0))
    finish(a1, a2)

    @pl.when(_not_finite(a1) | _not_finite(a2))
    def _():
        q1 = _without_ref_column(q1a, DIFF_REF_LANE)
        q2 = _without_ref_column(q2a, DIFF_REF_LANE)

        def exact(i, carry):
            m1, x1, m2, x2 = carry
            k_blk, vt_blk, bias = block_inputs(i)
            m1, x1 = _online_block(k_blk, q1, vt_blk, m1, x1, bias)
            m2, x2 = _online_block(k_blk, q2, vt_blk, m2, x2, bias)
            return m1, x1, m2, x2

        m0 = jnp.full((1, bq), NEG_BIG, F32)
        _, x1, _, x2 = lax.fori_loop(0, s_len // bk, exact, (m0, acc0, m0, acc0))
        finish(x1, x2)


def _diff_flash(scalars, qa, k, vt, subln, layer_idx, bq=512, bk=1024, unroll=8):
    b, h, s, d = k.shape
    bq, bk = min(bq, s), min(bk, s)
    unroll = min(unroll, s // bk)
    lam_init = 0.8 - 0.6 * math.exp(-0.3 * layer_idx)
    rel = (np.arange(bk)[:, None] - np.arange(bq)[None, :]).astype(np.float32)
    return pl.pallas_call(
        functools.partial(_diff_flash_kernel, bk=bk, unroll=unroll, out_scale=1.0 - lam_init),
        grid=(b, h, s // bq),
        in_specs=[pl.BlockSpec(memory_space=pltpu.SMEM),
                  pl.BlockSpec((1, 1, bq, 2 * d), lambda bi, hi, qi: (bi, hi, qi, 0)),
                  pl.BlockSpec((1, 1, s, d), lambda bi, hi, qi: (bi, hi, 0, 0)),
                  pl.BlockSpec((1, 1, VT_ROWS, s), lambda bi, hi, qi: (bi, hi, 0, 0)),
                  pl.BlockSpec((bk, bq), lambda bi, hi, qi: (0, 0)),
                  pl.BlockSpec((HEAD_DIM, 1), lambda bi, hi, qi: (0, 0))],
        out_specs=pl.BlockSpec((1, HEAD_DIM, bq), lambda bi, hi, qi: (bi, hi, qi)),
        out_shape=jax.ShapeDtypeStruct((b, h * HEAD_DIM, s), BF16),
        compiler_params=_params("parallel", "parallel", "arbitrary"),
        name="diff_flash",
    )(scalars, qa, k, vt, jnp.asarray(rel), subln.reshape(HEAD_DIM, 1))


def _conv_silu_kernel(x_ref, prev_ref, next_ref, w_ref, v_ref, op_ref, q_out, k_out, vt_out, opt_out, *, n_tiles):
    si = pl.program_id(1)
    x = x_ref[...]
    bm = x.shape[0]
    r = lax.broadcasted_iota(jnp.int32, (bm, bm), 0)
    c = lax.broadcasted_iota(jnp.int32, (bm, bm), 1)
    shift_dn = jnp.where(r == c + 1, 1.0, 0.0).astype(BF16)
    shift_up = jnp.where(r + 1 == c, 1.0, 0.0).astype(BF16)
    x_prev = jnp.dot(shift_dn, x, preferred_element_type=F32)
    x_next = jnp.dot(shift_up, x, preferred_element_type=F32)
    row = lax.broadcasted_iota(jnp.int32, x.shape, 0)
    halo_prev = jnp.where(si > 0, prev_ref[7:8, :].astype(F32), 0.0)
    halo_next = jnp.where(si < n_tiles - 1, next_ref[0:1, :].astype(F32), 0.0)
    x_prev = jnp.where(row == 0, halo_prev, x_prev)
    x_next = jnp.where(row == bm - 1, halo_next, x_next)
    w = w_ref[...]
    y = x_prev * w[0:1] + x.astype(F32) * w[1:2] + x_next * w[2:3]
    y = y * jax.nn.sigmoid(y)
    col = lax.broadcasted_iota(jnp.int32, x.shape, 1)
    qk = jnp.where(col >= GROUP_WIDTH, y * (HEAD_DIM ** -0.5), y).astype(BF16)
    for h in range(N_HEADS):
        q_out[0, h] = jnp.dot(qk, _lane_selector(2 * GROUP_WIDTH, HEAD_DIM, h * HEAD_DIM, 0, HEAD_DIM),
                              preferred_element_type=F32).astype(BF16)
        k_out[0, h] = jnp.dot(qk, _lane_selector(2 * GROUP_WIDTH, HEAD_DIM, GROUP_WIDTH + h * HEAD_DIM, 0, HEAD_DIM),
                              preferred_element_type=F32).astype(BF16)
    eye = _lane_selector(GROUP_WIDTH, GROUP_WIDTH, 0, 0, GROUP_WIDTH)
    vt_out[0] = lax.dot_general(eye, v_ref[...], _NT, preferred_element_type=F32).astype(BF16)
    opt_out[0] = lax.dot_general(eye, op_ref[...], _NT, preferred_element_type=F32).astype(BF16)


def _conv_silu(proj, b, s, conv_w, bm=256):
    bm = min(bm, s)
    nt = s // bm
    c = 2 * GROUP_WIDTH
    cb = U_CQK * V7X_LANES // c
    hb = bm // 8
    n8 = b * s // 8
    row = lambda bi, si: bi * nt + si
    hm_spec = pl.BlockSpec((1, N_HEADS, bm, HEAD_DIM), lambda bi, si: (bi, 0, si, 0))
    cm_spec = pl.BlockSpec((1, GROUP_WIDTH, bm), lambda bi, si: (bi, 0, si))
    hm_shape = jax.ShapeDtypeStruct((b, N_HEADS, s, HEAD_DIM), BF16)
    cm_shape = jax.ShapeDtypeStruct((b, GROUP_WIDTH, s), BF16)
    return pl.pallas_call(
        functools.partial(_conv_silu_kernel, n_tiles=nt),
        grid=(b, nt),
        in_specs=[pl.BlockSpec((bm, c), lambda bi, si: (row(bi, si), cb)),
                  pl.BlockSpec((8, c), lambda bi, si: (jnp.maximum(row(bi, si) * hb - 1, 0), cb)),
                  pl.BlockSpec((8, c), lambda bi, si: (jnp.minimum((row(bi, si) + 1) * hb, n8 - 1), cb)),
                  pl.BlockSpec((8, c), lambda bi, si: (0, 0)),
                  pl.BlockSpec((bm, GROUP_WIDTH), lambda bi, si: (row(bi, si), U_CV // 2)),
                  pl.BlockSpec((bm, GROUP_WIDTH), lambda bi, si: (row(bi, si), U_CO // 2))],
        out_specs=[hm_spec, hm_spec, cm_spec, cm_spec],
        out_shape=[hm_shape, hm_shape, cm_shape, cm_shape],
        compiler_params=_params("parallel", "parallel"),
        name="mlstm_conv",
    )(proj, proj, proj, jnp.concatenate([conv_w, jnp.zeros((5, c), conv_w.dtype)], axis=0), proj, proj)


def _log_sigmoid(x):
    return jnp.minimum(x, 0.0) - jnp.log(1.0 + jnp.exp(-jnp.abs(x)))


def _mlstm_chunk(qc, kc, vt1, gcol, grow, caug, m, *, backward):
    L = qc.shape[0]
    d = HEAD_DIM
    li_c, lf_c = gcol[:, 0:1], _log_sigmoid(gcol[:, 1:2])
    li_r, lf_r = grow[0:1, :], _log_sigmoid(grow[1:2, :])
    s_i = lax.broadcasted_iota(jnp.int32, (L, L), 0)
    j_i = lax.broadcasted_iota(jnp.int32, (L, L), 1)
    if backward:
        a_mask, b_mat, valid = j_i <= s_i, s_i >= j_i, s_i >= j_i
        bcum_row_idx, last_lane = L - 1, 0
    else:
        a_mask, b_mat, valid = j_i >= s_i, s_i <= j_i, s_i <= j_i
        bcum_row_idx, last_lane = 0, L - 1
    a = jnp.where(a_mask, lf_r, 0.0)
    a_hi = a.astype(BF16)
    a_lo = (a - a_hi.astype(F32)).astype(BF16)
    ones_b = jnp.where(b_mat, 1.0, 0.0).astype(BF16)
    e = jnp.dot(a_hi, ones_b, preferred_element_type=F32) + jnp.dot(a_lo, ones_b, preferred_element_type=F32)
    bcum_r = e[bcum_row_idx:bcum_row_idx + 1, :]
    dlog = jnp.where(valid, e + (li_c - lf_c), NEG_BIG)
    inter = bcum_r + m
    m_row = jnp.maximum(inter, jnp.max(dlog, axis=0, keepdims=True))
    w_intra = jnp.exp(dlog - m_row)
    w_inter = jnp.exp(inter - m_row)
    st = lax.dot_general(kc, qc, (((1,), (1,)), ((), ())), preferred_element_type=F32)
    at = (st * w_intra).astype(BF16)
    pv = jnp.dot(vt1, at, preferred_element_type=F32)
    cq = lax.dot_general(caug.astype(BF16), qc, (((1,), (1,)), ((), ())), preferred_element_type=F32)
    num = w_inter * cq[:d] + pv[:d]
    den = w_inter * cq[d:d + 1] + pv[d:d + 1]
    h = num / jnp.maximum(jnp.abs(den), jnp.exp(-m_row))
    b_last = bcum_r[:, last_lane:last_lane + 1]
    logw_end = b_last - bcum_r + li_r
    m_new = jnp.maximum(b_last + m, jnp.max(logw_end, axis=1, keepdims=True))
    w_end = jnp.exp(logw_end - m_new)
    decay = jnp.exp(b_last + m - m_new)
    u = jnp.dot((vt1.astype(F32) * w_end).astype(BF16), kc, preferred_element_type=F32)
    caug = decay * caug + u[:d + 8]
    return h, caug, m_new


def _mlstm_kernel(q_ref, k_ref, vt_ref, gc_ref, gr_ref, op_ref, ng_ref, o_ref, hf_sc, hb_sc):
    L = MLSTM_CHUNK
    d = HEAD_DIM
    s = q_ref.shape[2]
    nc = s // L
    ones = jnp.ones((ONES_ROWS, L), BF16)

    def chunk(c, caug, m, backward):
        t0 = pl.multiple_of(c * L, L)
        vt1 = jnp.concatenate([vt_ref[0, :, pl.ds(t0, L)], ones], axis=0)
        gcol = gc_ref[0, 0, pl.ds(t0, L), :]
        grow = gr_ref[0, 0, :, pl.ds(t0, L)]
        if backward:
            gcol, grow = gcol[:, 2:4], grow[2:4, :]
        else:
            gcol, grow = gcol[:, 0:2], grow[0:2, :]
        return _mlstm_chunk(q_ref[0, 0, pl.ds(t0, L), :], k_ref[0, 0, pl.ds(t0, L), :], vt1, gcol, grow,
                            caug, m, backward=backward), t0

    def body(i, carry):
        cf, mf, cb, mb = carry
        (hf, cf, mf), tf = chunk(i, cf, mf, False)
        hf_sc[:, pl.ds(tf, L)] = hf
        (hb, cb, mb), tb = chunk(nc - 1 - i, cb, mb, True)
        hb_sc[:, pl.ds(tb, L)] = hb
        return cf, mf, cb, mb

    c0 = jnp.zeros((d + 8, d), F32)
    m0 = jnp.zeros((1, 1), F32)
    lax.fori_loop(0, nc, body, (c0, m0, c0, m0))
    hsum = hf_sc[...] + hb_sc[...]
    ms = jnp.sum(hsum * hsum, axis=0, keepdims=True) * (1.0 / d)
    hn = hsum * lax.rsqrt(ms + EPS) * ng_ref[0]
    o_ref[0] = (hn * jax.nn.sigmoid(op_ref[0].astype(F32))).astype(o_ref.dtype)


def _mlstm(q, k, vt, gcol, grow, opre_t, norm_g):
    b, h, s, d = q.shape
    return pl.pallas_call(
        _mlstm_kernel,
        grid=(b, h),
        in_specs=[pl.BlockSpec((1, 1, s, d), lambda bi, hi: (bi, hi, 0, 0)),
                  pl.BlockSpec((1, 1, s, d), lambda bi, hi: (bi, hi, 0, 0)),
                  pl.BlockSpec((1, d, s), lambda bi, hi: (bi, hi, 0)),
                  pl.BlockSpec((1, 1, s, 4), lambda bi, hi: (bi, hi, 0, 0)),
                  pl.BlockSpec((1, 1, 4, s), lambda bi, hi: (bi, hi, 0, 0)),
                  pl.BlockSpec((1, d, s), lambda bi, hi: (bi, hi, 0)),
                  pl.BlockSpec((1, d, 1), lambda bi, hi: (hi, 0, 0))],
        out_specs=pl.BlockSpec((1, d, s), lambda bi, hi: (bi, hi, 0)),
        out_shape=jax.ShapeDtypeStruct((b, h * d, s), BF16),
        scratch_shapes=[pltpu.VMEM((d, s), F32), pltpu.VMEM((d, s), F32)],
        compiler_params=_params("parallel", "parallel"),
        name="mlstm_scan",
    )(q, k, vt, gcol, grow, opre_t, norm_g.reshape(h, d, 1))


def _na_kernel(q_ref, k_ref, v_ref, bias_ref, o_ref, *, rows_per_step, n_rows):
    blk = pl.program_id(1)
    win = NA_ROWS * GRID_W
    lane = lax.broadcasted_iota(jnp.int32, (GRID_W, V7X_LANES), 1)
    for i in range(rows_per_step):
        r = blk * rows_per_step + i
        r0 = jnp.clip(r - NA_ROWS // 2, 0, n_rows - NA_ROWS)
        dsel = r - r0
        k0 = pl.multiple_of(r0 * GRID_W, GRID_W)
        rows = slice(i * GRID_W, (i + 1) * GRID_W)
        for pair in range(N_HEADS // 2):
            lanes = slice(pair * V7X_LANES, (pair + 1) * V7X_LANES)
            q_pair = q_ref[rows, lanes]
            kw = k_ref[pl.ds(k0, win), lanes]
            vw = v_ref[pl.ds(k0, win), lanes]
            outs = []
            for sub in range(2):
                mine = (lane >= sub * HEAD_DIM) & (lane < (sub + 1) * HEAD_DIM)
                qh = jnp.where(mine, q_pair, jnp.zeros_like(q_pair))
                sc = lax.dot_general(qh, kw, _NT, preferred_element_type=F32)
                sc = sc * (HEAD_DIM ** -0.5) + bias_ref[2 * pair + sub, dsel]
                mx = jnp.max(sc, axis=-1, keepdims=True)
                p = jnp.exp(sc - mx)
                p = (p / jnp.sum(p, axis=-1, keepdims=True)).astype(BF16)
                outs.append(jnp.dot(p, vw, preferred_element_type=F32))
            o_ref[rows, lanes] = jnp.where(lane < HEAD_DIM, outs[0], outs[1]).astype(o_ref.dtype)


def _na(proj, b, s, bias, rows_per_step=8):
    n_rows = s // GRID_W
    rows_per_step = min(rows_per_step, n_rows)
    bm = rows_per_step * GRID_W
    nt = s // bm
    return pl.pallas_call(
        functools.partial(_na_kernel, rows_per_step=rows_per_step, n_rows=n_rows),
        grid=(b, nt),
        in_specs=[pl.BlockSpec((bm, GROUP_WIDTH), lambda bi, ri: (bi * nt + ri, U_DQ // 2)),
                  pl.BlockSpec((s, GROUP_WIDTH), lambda bi, ri: (bi, U_DK // 2)),
                  pl.BlockSpec((s, GROUP_WIDTH), lambda bi, ri: (bi, U_DV // 2)),
                  pl.BlockSpec(bias.shape, lambda bi, ri: (0, 0, 0, 0))],
        out_specs=pl.BlockSpec((bm, GROUP_WIDTH), lambda bi, ri: (bi * nt + ri, 0)),
        out_shape=jax.ShapeDtypeStruct((b * s, GROUP_WIDTH), BF16),
        compiler_params=_params("parallel", "arbitrary"),
        name="na_attn",
    )(proj, proj, proj, bias)


def _na_bias_table(rpb, n_rows):
    wr = min(NA_ROWS, n_rows)
    cols = np.arange(GRID_W)
    col_start = np.clip(cols - NA_COLS // 2, 0, GRID_W - NA_COLS)
    ck = np.arange(GRID_W)[None, :]
    valid = (ck >= col_start[:, None]) & (ck < col_start[:, None] + NA_COLS)
    crel = ck - cols[:, None] + NA_COLS - 1
    rrel = np.arange(wr)[None, :] - np.arange(wr)[:, None] + NA_ROWS - 1
    c_sel = ((crel[None] == np.arange(2 * NA_COLS - 1)[:, None, None]) & valid[None]).astype(np.float32)
    r_sel = (rrel[None] == np.arange(2 * NA_ROWS - 1)[:, None, None]).astype(np.float32)
    t = jnp.einsum('hab,adw,bqk->hdqwk', rpb.astype(F32), r_sel, c_sel, precision=lax.Precision.HIGHEST)
    t = t + np.where(valid, 0.0, NEG_BIG).astype(np.float32)[None, None, :, None, :]
    return t.reshape(rpb.shape[0], wr, GRID_W, wr * GRID_W)


def _outproj_kernel(ya_ref, yb_ref, yc_ref, yd_ref, x_ref, w_ref, o_ref):
    acc = x_ref[...] + jnp.dot(yd_ref[...], w_ref[3], preferred_element_type=F32)
    for g, y_ref in enumerate((ya_ref, yb_ref, yc_ref)):
        acc = acc + lax.dot_general(y_ref[0], w_ref[g], (((0,), (0,)), ((), ())),
                                    preferred_element_type=F32)
    o_ref[...] = acc


def _outproj(ya, yb, yc, yd, x, w, b, s, bm=512):
    bm = min(bm, s)
    nt = s // bm
    d = x.shape[1]
    yspec = pl.BlockSpec((1, GROUP_WIDTH, bm), lambda bi, si: (bi, 0, si))
    return pl.pallas_call(
        _outproj_kernel,
        grid=(b, nt),
        in_specs=[yspec, yspec, yspec,
                  pl.BlockSpec((bm, GROUP_WIDTH), lambda bi, si: (bi * nt + si, 0)),
                  pl.BlockSpec((bm, d), lambda bi, si: (bi * nt + si, 0)),
                  pl.BlockSpec(w.shape, lambda bi, si: (0, 0, 0))],
        out_specs=pl.BlockSpec((bm, d), lambda bi, si: (bi * nt + si, 0)),
        out_shape=jax.ShapeDtypeStruct(x.shape, F32),
        compiler_params=_params("parallel", "parallel"),
        name="outproj",
    )(ya, yb, yc, yd, x, w)


def _ffn_kernel(x_ref, g_ref, wg_ref, wu_ref, wd_ref, o_ref, hn_sc, acc_sc):
    f = pl.program_id(1)

    @pl.when(f == 0)
    def _():
        x = x_ref[...]
        hn_sc[...] = (_rms(x, x.shape[-1]) * g_ref[...]).astype(BF16)
        acc_sc[...] = jnp.zeros_like(acc_sc)

    hn = hn_sc[...]
    gate = jnp.dot(hn, wg_ref[...], preferred_element_type=F32)
    up = jnp.dot(hn, wu_ref[...], preferred_element_type=F32)
    act = (gate * jax.nn.sigmoid(gate) * up).astype(BF16)
    acc_sc[...] += jnp.dot(act, wd_ref[...], preferred_element_type=F32)

    @pl.when(f == pl.num_programs(1) - 1)
    def _():
        o_ref[...] = x_ref[...] + acc_sc[...]


def _ffn(x, g, wg, wu, wd, bm=512, bf=1408):
    n, d = x.shape
    ff = wg.shape[1]
    bm = min(bm, n)
    return pl.pallas_call(
        _ffn_kernel,
        grid=(n // bm, ff // bf),
        in_specs=[pl.BlockSpec((bm, d), lambda i, f: (i, 0)),
                  pl.BlockSpec((1, d), lambda i, f: (0, 0)),
                  pl.BlockSpec((d, bf), lambda i, f: (0, f)),
                  pl.BlockSpec((d, bf), lambda i, f: (0, f)),
                  pl.BlockSpec((bf, d), lambda i, f: (f, 0))],
        out_specs=pl.BlockSpec((bm, d), lambda i, f: (i, 0)),
        out_shape=jax.ShapeDtypeStruct(x.shape, F32),
        scratch_shapes=[pltpu.VMEM((bm, d), BF16), pltpu.VMEM((bm, d), F32)],
        compiler_params=_params("parallel", "arbitrary"),
        name="ffn",
    )(x, g.reshape(1, d), wg, wu, wd)


R_E1, R_E2, R_W1, R_W2, R_RANK1, R_RANK2 = range(6)


def _lane_pack(lane, cols):
    out = jnp.zeros(lane.shape, F32)
    for idx, col in cols:
        out = out + jnp.where(lane == idx, col, 0.0)
    return out


def _router_kernel(x_ref, g_ref, wr_ref, hn_ref, route_ref, counts_ref, carry_sc):
    @pl.when(pl.program_id(0) == 0)
    def _():
        carry_sc[...] = jnp.zeros_like(carry_sc)

    x = x_ref[...]
    bm = x.shape[0]
    hn = _rms(x, x.shape[-1]) * g_ref[...]
    hn_ref[...] = hn
    logits = jnp.dot(hn, wr_ref[...], preferred_element_type=F32, precision=lax.Precision.HIGHEST)
    lane = lax.broadcasted_iota(jnp.int32, logits.shape, 1)
    logits = jnp.where(lane < N_EXPERTS, logits, NEG_BIG)
    m1 = jnp.max(logits, axis=-1, keepdims=True)
    i1 = jnp.min(jnp.where(logits == m1, lane, V7X_LANES), axis=-1, keepdims=True)
    rest = jnp.where(lane == i1, NEG_BIG, logits)
    m2 = jnp.max(rest, axis=-1, keepdims=True)
    i2 = jnp.min(jnp.where(rest == m2, lane, V7X_LANES), axis=-1, keepdims=True)
    e2 = jnp.exp(m2 - m1)
    w1 = 1.0 / (1.0 + e2)
    w2 = e2 / (1.0 + e2)
    hot1 = jnp.where(lane == i1, 1.0, 0.0)
    hot2 = jnp.where(lane == i2, 1.0, 0.0)
    r_i = lax.broadcasted_iota(jnp.int32, (bm, bm), 0)
    c_i = lax.broadcasted_iota(jnp.int32, (bm, bm), 1)
    below = jnp.where(c_i < r_i, 1.0, 0.0).astype(BF16)
    before1 = jnp.dot(below, hot1.astype(BF16), preferred_element_type=F32)
    before2 = jnp.dot(below, hot2.astype(BF16), preferred_element_type=F32)
    cnt1 = jnp.sum(hot1, axis=0, keepdims=True)
    cnt2 = jnp.sum(hot2, axis=0, keepdims=True)
    carry = carry_sc[...]
    rank1 = jnp.sum(hot1 * (before1 + carry), axis=-1, keepdims=True)
    rank2 = jnp.sum(hot2 * (before2 + carry + cnt1), axis=-1, keepdims=True)
    carry = carry + cnt1 + cnt2
    carry_sc[...] = carry
    counts_ref[...] = carry
    route_ref[...] = _lane_pack(lane, ((R_E1, i1.astype(F32)), (R_E2, i2.astype(F32)), (R_W1, w1), (R_W2, w2),
                                       (R_RANK1, rank1), (R_RANK2, rank2)))


def _router(x, g, w_router, bm=512):
    n, d = x.shape
    bm = min(bm, n)
    wr = jnp.concatenate([w_router, jnp.zeros((d, V7X_LANES - N_EXPERTS), w_router.dtype)], axis=1)
    return pl.pallas_call(
        _router_kernel,
        grid=(n // bm,),
        in_specs=[pl.BlockSpec((bm, d), lambda i: (i, 0)),
                  pl.BlockSpec((1, d), lambda i: (0, 0)),
                  pl.BlockSpec((d, V7X_LANES), lambda i: (0, 0))],
        out_specs=[pl.BlockSpec((bm, d), lambda i: (i, 0)),
                   pl.BlockSpec((bm, V7X_LANES), lambda i: (i, 0)),
                   pl.BlockSpec((1, V7X_LANES), lambda i: (0, 0))],
        out_shape=[jax.ShapeDtypeStruct((n, d), F32),
                   jax.ShapeDtypeStruct((n, V7X_LANES), F32),
                   jax.ShapeDtypeStruct((1, V7X_LANES), F32)],
        scratch_shapes=[pltpu.VMEM((1, V7X_LANES), F32)],
        compiler_params=_params("arbitrary"),
        name="moe_router",
    )(x, g.reshape(1, d), wr)


def _route_plan(route, counts, tm):
    n = route.shape[0]
    counts = counts[0, :N_EXPERTS].astype(jnp.int32)
    padded = ((counts + tm - 1) // tm) * tm
    g_end = jnp.cumsum(padded)
    g_start = (g_end - padded).astype(F32)
    experts = jnp.arange(N_EXPERTS, dtype=F32)[None, :]
    start1 = jnp.sum(jnp.where(route[:, R_E1:R_E1 + 1] == experts, g_start[None, :], 0.0), axis=1)
    start2 = jnp.sum(jnp.where(route[:, R_E2:R_E2 + 1] == experts, g_start[None, :], 0.0), axis=1)
    pos = jnp.stack([start1 + route[:, R_RANK1], start2 + route[:, R_RANK2]], axis=1).astype(jnp.int32)
    n_tiles = 2 * n // tm + N_EXPERTS
    tile_start = jnp.arange(n_tiles, dtype=jnp.int32) * tm
    tile_expert = jnp.minimum(jnp.sum(tile_start[:, None] >= g_end[None, :], axis=1), N_EXPERTS - 1)
    n_used = (g_end[-1:] // tm).astype(jnp.int32)
    return pos.reshape(-1), tile_expert.astype(jnp.int32), n_used, n_tiles


def _row_copies(pos_ref, r, src_at, dst_at, sem):
    copies = []
    for c in range(2):
        p = pos_ref[2 * r + c]
        copies.append(pltpu.make_async_copy(src_at(r, c, p), dst_at(r, c, p), sem))
    return copies


def _move_rows(pos_ref, n_rows, src_at, dst_at, sem):
    def start(r, carry):
        for cp in _row_copies(pos_ref, r, src_at, dst_at, sem):
            cp.start()
        return carry

    def wait(r, carry):
        for cp in _row_copies(pos_ref, r, src_at, dst_at, sem):
            cp.wait()
        return carry

    lax.fori_loop(0, n_rows, start, 0, unroll=8)
    lax.fori_loop(0, n_rows, wait, 0, unroll=8)


def _dispatch_kernel(pos_ref, hn_ref, xg_in_ref, xg_ref, sem):
    del xg_in_ref
    _move_rows(pos_ref, hn_ref.shape[0],
               lambda r, c, p: hn_ref.at[pl.ds(r, 1)],
               lambda r, c, p: xg_ref.at[pl.ds(p, 1)], sem)


def _dispatch(pos, hn, n_rows, bm=512):
    n, d = hn.shape
    bm = min(bm, n)
    return pl.pallas_call(
        _dispatch_kernel,
        grid=(n // bm,),
        in_specs=[pl.BlockSpec((2 * bm,), lambda i: (i,), memory_space=pltpu.SMEM),
                  pl.BlockSpec((bm, d), lambda i: (i, 0)),
                  pl.BlockSpec(memory_space=pl.ANY)],
        out_specs=pl.BlockSpec(memory_space=pl.ANY),
        out_shape=jax.ShapeDtypeStruct((n_rows, d), F32),
        scratch_shapes=[pltpu.SemaphoreType.DMA(())],
        input_output_aliases={2: 0},
        compiler_params=_params("arbitrary"),
        name="moe_dispatch",
    )(pos, hn, jnp.zeros((n_rows, d), F32))


def _expert_ffn_kernel(te_ref, nu_ref, xg_ref, wg_ref, wu_ref, wd_ref, o_ref, hn_sc, acc_sc):
    del te_ref
    t = pl.program_id(0)
    f = pl.program_id(1)
    last = pl.num_programs(1) - 1
    used = t < nu_ref[0]

    @pl.when(used & (f == 0))
    def _():
        hn_sc[...] = xg_ref[...].astype(BF16)
        acc_sc[...] = jnp.zeros_like(acc_sc)

    @pl.when(used)
    def _():
        hn = hn_sc[...]
        gate = jnp.dot(hn, wg_ref[0], preferred_element_type=F32)
        up = jnp.dot(hn, wu_ref[0], preferred_element_type=F32)
        act = (gate * jax.nn.sigmoid(gate) * up).astype(BF16)
        acc_sc[...] += jnp.dot(act, wd_ref[0], preferred_element_type=F32)

    @pl.when(used & (f == last))
    def _():
        o_ref[...] = acc_sc[...]

    @pl.when(jnp.logical_not(used) & (f == last))
    def _():
        o_ref[...] = jnp.zeros_like(o_ref)


def _expert_ffn(tile_expert, n_used, xg, wg, wu, wd, tm, bf=896):
    rows, d = xg.shape
    ff = wg.shape[2]
    grid_spec = pltpu.PrefetchScalarGridSpec(
        num_scalar_prefetch=2,
        grid=(rows // tm, ff // bf),
        in_specs=[pl.BlockSpec((tm, d), lambda t, f, te, nu: (t, 0)),
                  pl.BlockSpec((1, d, bf), lambda t, f, te, nu: (te[t], 0, f)),
                  pl.BlockSpec((1, d, bf), lambda t, f, te, nu: (te[t], 0, f)),
                  pl.BlockSpec((1, bf, d), lambda t, f, te, nu: (te[t], f, 0))],
        out_specs=pl.BlockSpec((tm, d), lambda t, f, te, nu: (t, 0)),
        scratch_shapes=[pltpu.VMEM((tm, d), BF16), pltpu.VMEM((tm, d), F32)])
    return pl.pallas_call(
        _expert_ffn_kernel,
        grid_spec=grid_spec,
        out_shape=jax.ShapeDtypeStruct((rows, d), F32),
        compiler_params=_params("arbitrary", "arbitrary"),
        name="moe_ffn",
    )(tile_expert, n_used, xg, wg, wu, wd)


def _combine_kernel(pos_ref, x_ref, route_ref, g_ref, yg_ref, o_ref, buf, sem, *, final_norm):
    _move_rows(pos_ref, x_ref.shape[0],
               lambda r, c, p: yg_ref.at[pl.ds(p, 1)],
               lambda r, c, p: buf.at[c, pl.ds(r, 1)], sem)
    route = route_ref[...]
    y = x_ref[...] + route[:, R_W1:R_W1 + 1] * buf[0] + route[:, R_W2:R_W2 + 1] * buf[1]
    if final_norm:
        y = _rms(y, y.shape[-1]) * g_ref[...]
    o_ref[...] = y


def _combine(pos, x, route, yg, g_final, final_norm, bm=512):
    n, d = x.shape
    bm = min(bm, n)
    return pl.pallas_call(
        functools.partial(_combine_kernel, final_norm=final_norm),
        grid=(n // bm,),
        in_specs=[pl.BlockSpec((2 * bm,), lambda i: (i,), memory_space=pltpu.SMEM),
                  pl.BlockSpec((bm, d), lambda i: (i, 0)),
                  pl.BlockSpec((bm, V7X_LANES), lambda i: (i, 0)),
                  pl.BlockSpec((1, d), lambda i: (0, 0)),
                  pl.BlockSpec(memory_space=pl.ANY)],
        out_specs=pl.BlockSpec((bm, d), lambda i: (i, 0)),
        out_shape=jax.ShapeDtypeStruct(x.shape, F32),
        scratch_shapes=[pltpu.VMEM((2, bm, d), F32), pltpu.SemaphoreType.DMA(())],
        compiler_params=_params("arbitrary"),
        name="moe_combine",
    )(pos, x, route, g_final.reshape(1, d), yg)


def _moe(x, g, w_router, wg, wu, wd, g_final, final_norm, tm=512):
    hn, route, counts = _router(x, g, w_router)
    pos, tile_expert, n_used, n_tiles = _route_plan(route, counts, tm)
    xg = _dispatch(pos, hn, n_tiles * tm)
    yg = _expert_ffn(tile_expert, n_used, xg, wg, wu, wd, tm)
    return _combine(pos, x, route, yg, g_final, final_norm)


def _final_norm_kernel(x_ref, g_ref, o_ref):
    x = x_ref[...]
    o_ref[...] = _rms(x, x.shape[-1]) * g_ref[...]


def _final_norm(x, g, bm=1024):
    n, d = x.shape
    bm = min(bm, n)
    return pl.pallas_call(
        _final_norm_kernel,
        grid=(n // bm,),
        in_specs=[pl.BlockSpec((bm, d), lambda i: (i, 0)), pl.BlockSpec((1, d), lambda i: (0, 0))],
        out_specs=pl.BlockSpec((bm, d), lambda i: (i, 0)),
        out_shape=jax.ShapeDtypeStruct(x.shape, F32),
        compiler_params=_params("parallel"),
        name="final_norm",
    )(x, g.reshape(1, d))


def _token_mix(x, i, b, s, p):
    w_main, w_gates = _prep_w_in(p['w_in'][i])
    proj, gates = _inproj(x, p['norm_mix'][i], w_main, w_gates)

    cos_t, sin_t = _rope_tables(s)
    q_a, k_a, vt_a = _mla_prep(proj, b, s, cos_t, sin_t, *_prep_mla_weights(
        p['mla_q_norm'][i], p['mla_kv_norm'][i], p['mla_w_uq'][i], p['mla_w_ukv'][i]))
    y_a = _mla_flash(q_a, k_a, vt_a)

    lp = p['diff_lambda'][i].astype(F32)
    lam_init = 0.8 - 0.6 * math.exp(-0.3 * i)
    lam = jnp.exp(jnp.sum(lp[0] * lp[1])) - jnp.exp(jnp.sum(lp[2] * lp[3])) + lam_init
    slopes = 2.0 ** (-8.0 * jnp.arange(1, N_HEADS + 1, dtype=F32) / N_HEADS)
    scalars = jnp.concatenate([-slopes * LOG2E, lam[None]]).astype(F32)
    qa_b, k_b, vt_b = _diff_prep(proj, b, s)
    y_b = _diff_flash(scalars, qa_b, k_b, vt_b, p['diff_subln'][i], i)

    q_c, k_c, vt_c, opt_c = _conv_silu(proj, b, s, p['mlstm_conv'][i])
    g = (gates[:, :4 * N_HEADS] + p['mlstm_gate_bias'][i][None, :]).reshape(b, s, 4, N_HEADS)
    y_c = _mlstm(q_c, k_c, vt_c, g.transpose(0, 3, 1, 2), g.transpose(0, 3, 2, 1), opt_c, p['mlstm_norm'][i])

    y_d = _na(proj, b, s, _na_bias_table(p['na_rpb'][i], s // GRID_W))

    w_out = p['w_out'][i].reshape(4, GROUP_WIDTH, -1).astype(BF16)
    return _outproj(y_a, y_b, y_c, y_d, x, w_out, b, s)


def _trunk(x, p, depth):
    b, s, d = x.shape
    x = x.reshape(b * s, d)
    for i in range(depth):
        x = _token_mix(x, i, b, s, p)
        j = i // 2
        if i % 2 == 0:
            x = _ffn(x, p['norm_ffn'][i], p['ffn_w_gate'][j].astype(BF16), p['ffn_w_up'][j].astype(BF16),
                     p['ffn_w_down'][j].astype(BF16))
        else:
            x = _moe(x, p['norm_ffn'][i], p['moe_router'][j], p['moe_w_gate'][j].astype(BF16),
                     p['moe_w_up'][j].astype(BF16), p['moe_w_down'][j].astype(BF16),
                     p['norm_final'], final_norm=(i == depth - 1))
    if depth % 2 == 1:
        x = _final_norm(x, p['norm_final'])
    return x.reshape(b, s, d)


def kernel(x_prompt, x_sample, norm_mix, norm_ffn, w_in, w_out, mla_q_norm, mla_kv_norm, mla_w_uq, mla_w_ukv,
           diff_lambda, diff_subln, mlstm_conv, mlstm_gate_bias, mlstm_norm, na_rpb, ffn_w_gate, ffn_w_up,
           ffn_w_down, moe_router, moe_w_gate, moe_w_up, moe_w_down, norm_final):
    p = dict(norm_mix=norm_mix, norm_ffn=norm_ffn, w_in=w_in, w_out=w_out, mla_q_norm=mla_q_norm,
             mla_kv_norm=mla_kv_norm, mla_w_uq=mla_w_uq, mla_w_ukv=mla_w_ukv, diff_lambda=diff_lambda,
             diff_subln=diff_subln, mlstm_conv=mlstm_conv, mlstm_gate_bias=mlstm_gate_bias,
             mlstm_norm=mlstm_norm, na_rpb=na_rpb, ffn_w_gate=ffn_w_gate, ffn_w_up=ffn_w_up,
             ffn_w_down=ffn_w_down, moe_router=moe_router, moe_w_gate=moe_w_gate, moe_w_up=moe_w_up,
             moe_w_down=moe_w_down, norm_final=norm_final)
    depth = norm_mix.shape[0]
    nb = x_prompt.shape[0]
    y = _trunk(jnp.concatenate([x_prompt, x_sample], axis=0), p, depth)
    return (y[:nb], y[nb:])
```

```python
import functools
import math

import numpy as np
import jax
import jax.numpy as jnp
from jax import lax
from jax.experimental import pallas as pl
from jax.experimental.pallas import tpu as pltpu

F32 = jnp.float32
BF16 = jnp.bfloat16

V7X_LANES = 128
V7X_VMEM_LIMIT_BYTES = 56 * 1024 * 1024

EPS = 1e-6
LOG2E = 1.4426950408889634
NEG_BIG = -1e30

HEAD_DIM = 64
N_HEADS = 4
GROUP_WIDTH = 256
MLA_Q_LORA = 192
MLA_KV_LORA = 128
MLA_NOPE = 64
MLA_ROPE = 32
ROPE_THETA = 10000.0
DIFF_HALF = 32
MLSTM_CHUNK = 128
GRID_W = 64
NA_ROWS = 8
NA_COLS = 16
N_EXPERTS = 8
ONES_ROWS = 16
VT_ROWS = HEAD_DIM + ONES_ROWS
MLA_REF_LANE = MLA_NOPE + MLA_ROPE
DIFF_REF_LANE = HEAD_DIM

IN_SECTIONS = (192, 128, 32, 256, 256, 256, 512, 256, 256, 16, 256, 256, 256)
U_BQ, U_BK, U_BV, U_DQ, U_DK, U_DV, U_CQK, U_CV, U_CO, U_ACQ, U_ACKV, U_AKR, U_AKRR = (
    0, 2, 4, 6, 8, 10, 12, 16, 18, 20, 22, 23, 24)
PROJ_COLS = 25 * V7X_LANES


def _params(*sem):
    return pltpu.CompilerParams(dimension_semantics=sem, vmem_limit_bytes=V7X_VMEM_LIMIT_BYTES)


def _rms(x, n):
    return x * lax.rsqrt(jnp.sum(x * x, axis=-1, keepdims=True) * (1.0 / n) + EPS)


def _inproj_kernel(x_ref, g_ref, w_ref, wg_ref, o_ref, og_ref, *, col_chunk):
    x = x_ref[...]
    hn = (_rms(x, x.shape[-1]) * g_ref[...]).astype(BF16)
    for c in range(0, o_ref.shape[1], col_chunk):
        w = min(col_chunk, o_ref.shape[1] - c)
        o_ref[:, c:c + w] = jnp.dot(hn, w_ref[:, c:c + w], preferred_element_type=F32).astype(BF16)
    og_ref[...] = jnp.dot(hn, wg_ref[...], preferred_element_type=F32)


def _inproj(x, g, w_main, w_gates, bm=512):
    n, d = x.shape
    bm = min(bm, n)
    return pl.pallas_call(
        functools.partial(_inproj_kernel, col_chunk=640),
        grid=(n // bm,),
        in_specs=[pl.BlockSpec((bm, d), lambda i: (i, 0)),
                  pl.BlockSpec((1, d), lambda i: (0, 0)),
                  pl.BlockSpec(w_main.shape, lambda i: (0, 0)),
                  pl.BlockSpec(w_gates.shape, lambda i: (0, 0))],
        out_specs=[pl.BlockSpec((bm, PROJ_COLS), lambda i: (i, 0)),
                   pl.BlockSpec((bm, V7X_LANES), lambda i: (i, 0))],
        out_shape=[jax.ShapeDtypeStruct((n, PROJ_COLS), BF16),
                   jax.ShapeDtypeStruct((n, V7X_LANES), F32)],
        compiler_params=_params("parallel"),
        name="inproj",
    )(x, g.reshape(1, d), w_main, w_gates)


def _prep_w_in(w):
    d = w.shape[0]
    offs = np.cumsum((0,) + IN_SECTIONS)
    (a_cq, a_ckv, a_kr, b_q, b_k, b_v, c_qk, c_v, c_o, c_g, d_q, d_k, d_v) = [
        w[:, offs[i]:offs[i + 1]] for i in range(len(IN_SECTIONS))]
    z = lambda k: jnp.zeros((d, k), w.dtype)
    half = MLA_ROPE // 2
    a_kr_rot = jnp.concatenate([a_kr[:, half:], a_kr[:, :half]], axis=1)
    main = jnp.concatenate([b_q, b_k, b_v, d_q, d_k, d_v, c_qk, c_v, c_o,
                            a_cq, z(64), a_ckv,
                            z(64), a_kr, z(32),
                            z(64), a_kr_rot, z(32)], axis=1)
    gates = jnp.concatenate([c_g, z(V7X_LANES - 16)], axis=1)
    return main.astype(BF16), gates.astype(BF16)


def _mla_prep_kernel(cq_ref, ckv_ref, kr_ref, krr_ref, cos_ref, sin_ref, gq_ref, gkv_ref,
                     wq_ref, wqr_ref, wkn_ref, wvt_ref, ones_ref, q_out, k_out, vt_out, *, q_scale):
    cos = cos_ref[...]
    sin = sin_ref[...]
    qn = (_rms(cq_ref[...].astype(F32), MLA_Q_LORA) * gq_ref[...]).astype(BF16)
    qa = jnp.dot(qn, wq_ref[...], preferred_element_type=F32)
    qr = jnp.dot(qn, wqr_ref[...], preferred_element_type=F32)
    kvn = (_rms(ckv_ref[...].astype(F32), MLA_KV_LORA) * gkv_ref[...]).astype(BF16)
    kn = jnp.dot(kvn, wkn_ref[...], preferred_element_type=F32)
    k_rope = kr_ref[...].astype(F32) * cos + krr_ref[...].astype(F32) * sin
    k_rope = jnp.where(lax.broadcasted_iota(jnp.int32, k_rope.shape, 1) == MLA_REF_LANE, 1.0, k_rope)
    def head_stages(h):
        sl = slice(h * V7X_LANES, (h + 1) * V7X_LANES)
        q_h = ((qa[:, sl] * cos + qr[:, sl] * sin) * q_scale).astype(BF16)
        k_h = (kn[:, sl] + k_rope).astype(BF16)
        k_out[0, h] = k_h
        vt = lax.dot_general(wvt_ref[h], kvn, _NT, preferred_element_type=F32)
        yield
        vt_out[0, h] = (vt + ones_ref[...]).astype(BF16)
        m_col = _diag_ref_max(q_h, k_h)
        yield
        q_out[0, h] = _with_ref_column(q_h, m_col, MLA_REF_LANE)

    _run_interleaved([head_stages(h) for h in range(N_HEADS)])


def _mla_prep(proj, b, s, cos_t, sin_t, gq, gkv, wq, wqr, wkn, wvt, bm=512):
    bm = min(bm, s)
    nt = s // bm
    row = lambda bi, si: bi * nt + si
    ones_col = jnp.concatenate([jnp.zeros((HEAD_DIM, 1), F32), jnp.ones((ONES_ROWS, 1), F32)], axis=0)
    full = lambda a: pl.BlockSpec(a.shape, lambda bi, si: (0,) * a.ndim)
    return pl.pallas_call(
        functools.partial(_mla_prep_kernel, q_scale=LOG2E * (MLA_NOPE + MLA_ROPE) ** -0.5),
        grid=(b, nt),
        in_specs=[pl.BlockSpec((bm, 2 * V7X_LANES), lambda bi, si: (row(bi, si), U_ACQ // 2)),
                  pl.BlockSpec((bm, V7X_LANES), lambda bi, si: (row(bi, si), U_ACKV)),
                  pl.BlockSpec((bm, V7X_LANES), lambda bi, si: (row(bi, si), U_AKR)),
                  pl.BlockSpec((bm, V7X_LANES), lambda bi, si: (row(bi, si), U_AKRR)),
                  pl.BlockSpec((bm, V7X_LANES), lambda bi, si: (si, 0)),
                  pl.BlockSpec((bm, V7X_LANES), lambda bi, si: (si, 0)),
                  full(gq), full(gkv), full(wq), full(wqr), full(wkn), full(wvt), full(ones_col)],
        out_specs=[pl.BlockSpec((1, N_HEADS, bm, V7X_LANES), lambda bi, si: (bi, 0, si, 0)),
                   pl.BlockSpec((1, N_HEADS, bm, V7X_LANES), lambda bi, si: (bi, 0, si, 0)),
                   pl.BlockSpec((1, N_HEADS, VT_ROWS, bm), lambda bi, si: (bi, 0, 0, si))],
        out_shape=[jax.ShapeDtypeStruct((b, N_HEADS, s, V7X_LANES), BF16),
                   jax.ShapeDtypeStruct((b, N_HEADS, s, V7X_LANES), BF16),
                   jax.ShapeDtypeStruct((b, N_HEADS, VT_ROWS, s), BF16)],
        compiler_params=_params("parallel", "parallel"),
        name="mla_prep",
    )(proj, proj, proj, proj, cos_t, sin_t, gq, gkv, wq, wqr, wkn, wvt, ones_col)


def _prep_mla_weights(q_norm, kv_norm, w_uq, w_ukv):
    half = MLA_ROPE // 2
    dq = MLA_NOPE + MLA_ROPE
    wq_h = w_uq.reshape(MLA_Q_LORA, N_HEADS, dq)
    zq = lambda k: jnp.zeros((MLA_Q_LORA, N_HEADS, k), w_uq.dtype)
    rope_cols = wq_h[:, :, MLA_NOPE:]
    rope_rot = jnp.concatenate([rope_cols[:, :, half:], rope_cols[:, :, :half]], axis=2)
    wq = jnp.concatenate([wq_h, zq(V7X_LANES - dq)], axis=2).reshape(MLA_Q_LORA, N_HEADS * V7X_LANES)
    wqr = jnp.concatenate([zq(MLA_NOPE), rope_rot, zq(V7X_LANES - dq)], axis=2).reshape(
        MLA_Q_LORA, N_HEADS * V7X_LANES)
    pad_rows = jnp.zeros((2 * V7X_LANES - MLA_Q_LORA, N_HEADS * V7X_LANES), w_uq.dtype)
    wq = jnp.concatenate([wq, pad_rows], axis=0).astype(BF16)
    wqr = jnp.concatenate([wqr, pad_rows], axis=0).astype(BF16)
    wkv_h = w_ukv.reshape(MLA_KV_LORA, N_HEADS, MLA_NOPE + HEAD_DIM)
    wkn = jnp.concatenate([wkv_h[:, :, :MLA_NOPE],
                           jnp.zeros((MLA_KV_LORA, N_HEADS, V7X_LANES - MLA_NOPE), w_ukv.dtype)],
                          axis=2).reshape(MLA_KV_LORA, N_HEADS * V7X_LANES).astype(BF16)
    wvt = jnp.transpose(wkv_h[:, :, MLA_NOPE:], (1, 2, 0))
    wvt = jnp.concatenate([wvt, jnp.zeros((N_HEADS, ONES_ROWS, MLA_KV_LORA), w_ukv.dtype)],
                          axis=1).astype(BF16)
    gq = jnp.concatenate([q_norm, jnp.zeros((2 * V7X_LANES - MLA_Q_LORA,), q_norm.dtype)]).reshape(1, -1)
    gkv = kv_norm.reshape(1, -1)
    return gq, gkv, wq, wqr, wkn, wvt


def _rope_tables(s):
    half = MLA_ROPE // 2
    inv = ROPE_THETA ** (-jnp.arange(half, dtype=F32) / half)
    ang = jnp.arange(s).astype(F32)[:, None] * inv[None, :]
    cos, sin = jnp.cos(ang), jnp.sin(ang)
    ones = jnp.ones((s, MLA_NOPE), F32)
    z = lambda k: jnp.zeros((s, k), F32)
    pad = V7X_LANES - MLA_NOPE - MLA_ROPE
    cos_t = jnp.concatenate([ones, cos, cos, z(pad)], axis=1)
    sin_t = jnp.concatenate([z(MLA_NOPE), -sin, sin, z(pad)], axis=1)
    return cos_t, sin_t


_NT = (((1,), (1,)), ((), ()))


def _diag_ref_max(q, k, bias=None):
    groups = [slice(j * V7X_LANES, (j + 1) * V7X_LANES) for j in range(q.shape[0] // V7X_LANES)]
    scores = [lax.dot_general(q[rows], k[rows], _NT, preferred_element_type=F32) for rows in groups]
    if bias is not None:
        scores = [sc + bias for sc in scores]
    return jnp.concatenate([jnp.max(sc, axis=1, keepdims=True) for sc in scores], axis=0)


def _with_ref_column(q, m_col, ref_lane):
    lane = lax.broadcasted_iota(jnp.int32, q.shape, 1)
    return jnp.where(lane == ref_lane, -m_col, q.astype(F32)).astype(BF16)


def _without_ref_column(q_aug, ref_lane):
    lane = lax.broadcasted_iota(jnp.int32, q_aug.shape, 1)
    return jnp.where(lane == ref_lane, 0.0, q_aug.astype(F32)).astype(BF16)


def _not_finite(acc):
    return jnp.max(jnp.where(jnp.isfinite(acc), 0.0, 1.0)) > 0.0


def _online_block(k_blk, q, vt_blk, m, acc, bias=None):
    st = lax.dot_general(k_blk, q, _NT, preferred_element_type=F32)
    if bias is not None:
        st = st + bias
    m_new = jnp.maximum(m, jnp.max(st, axis=0, keepdims=True))
    p = jnp.exp2(st - m_new).astype(BF16)
    return m_new, acc * jnp.exp2(m - m_new) + jnp.dot(vt_blk, p, preferred_element_type=F32)


def _mla_flash_kernel(q_ref, k_ref, vt_ref, o_ref, *, bk, unroll):
    q_aug = q_ref[0, 0]
    bq = q_aug.shape[0]
    s_len = k_ref.shape[2]

    def body(i, acc):
        def scores(u):
            k0 = pl.multiple_of((i * unroll + u) * bk, bk)
            return k0, lax.dot_general(k_ref[0, 0, pl.ds(k0, bk), :], q_aug, _NT, preferred_element_type=F32)

        nxt = scores(0)
        for u in range(unroll):
            (k0, st), nxt = nxt, (scores(u + 1) if u + 1 < unroll else None)
            acc = acc + jnp.dot(vt_ref[0, 0, :, pl.ds(k0, bk)], jnp.exp2(st).astype(BF16),
                                preferred_element_type=F32)
        return acc

    acc0 = jnp.zeros((VT_ROWS, bq), F32)
    acc = lax.fori_loop(0, s_len // (bk * unroll), body, acc0)
    o_ref[0] = (acc[:HEAD_DIM] / acc[HEAD_DIM:HEAD_DIM + 1]).astype(o_ref.dtype)

    @pl.when(_not_finite(acc))
    def _():
        q = _without_ref_column(q_aug, MLA_REF_LANE)

        def exact(i, carry):
            k0 = pl.multiple_of(i * bk, bk)
            return _online_block(k_ref[0, 0, pl.ds(k0, bk), :], q, vt_ref[0, 0, :, pl.ds(k0, bk)], *carry)

        _, acc_x = lax.fori_loop(0, s_len // bk, exact, (jnp.full((1, bq), NEG_BIG, F32), acc0))
        o_ref[0] = (acc_x[:HEAD_DIM] / acc_x[HEAD_DIM:HEAD_DIM + 1]).astype(o_ref.dtype)


def _mla_flash(q, k, vt, bq=512, bk=2048, unroll=4):
    b, h, s, dq = q.shape
    bq, bk = min(bq, s), min(bk, s)
    unroll = min(unroll, s // bk)
    return pl.pallas_call(
        functools.partial(_mla_flash_kernel, bk=bk, unroll=unroll),
        grid=(b, h, s // bq),
        in_specs=[pl.BlockSpec((1, 1, bq, dq), lambda bi, hi, qi: (bi, hi, qi, 0)),
                  pl.BlockSpec((1, 1, s, dq), lambda bi, hi, qi: (bi, hi, 0, 0)),
                  pl.BlockSpec((1, 1, VT_ROWS, s), lambda bi, hi, qi: (bi, hi, 0, 0))],
        out_specs=pl.BlockSpec((1, HEAD_DIM, bq), lambda bi, hi, qi: (bi, hi, qi)),
        out_shape=jax.ShapeDtypeStruct((b, h * HEAD_DIM, s), BF16),
        compiler_params=_params("parallel", "parallel", "arbitrary"),
        name="mla_flash",
    )(q, k, vt)


def _lane_selector(n_src, n_dst, src0, dst0, width, transposed=False):
    shape = (n_dst, n_src) if transposed else (n_src, n_dst)
    src = lax.broadcasted_iota(jnp.int32, shape, 1 if transposed else 0)
    dst = lax.broadcasted_iota(jnp.int32, shape, 0 if transposed else 1)
    hit = (src - src0 == dst - dst0) & (dst >= dst0) & (dst < dst0 + width)
    return jnp.where(hit, 1.0, 0.0).astype(BF16)


def _diff_prep_kernel(q_ref, k_ref, v_ref, qa_out, k_out, vt_out, *, q_scale, slopes):
    q_tile, k_tile, v_tile = q_ref[...], k_ref[...], v_ref[...]
    lane = lax.broadcasted_iota(jnp.int32, (q_tile.shape[0], V7X_LANES), 1)
    diag = jnp.abs((lax.broadcasted_iota(jnp.int32, (V7X_LANES, V7X_LANES), 0)
                    - lax.broadcasted_iota(jnp.int32, (V7X_LANES, V7X_LANES), 1)).astype(F32))
    ones_rows = jnp.where(lax.broadcasted_iota(jnp.int32, (VT_ROWS, 1), 0) >= HEAD_DIM, 1.0, 0.0)
    def head_stages(h):
        c0 = h * HEAD_DIM
        k_h = jnp.dot(k_tile, _lane_selector(GROUP_WIDTH, V7X_LANES, c0, 0, HEAD_DIM), preferred_element_type=F32)
        q_parts = [jnp.dot(q_tile, _lane_selector(GROUP_WIDTH, V7X_LANES, c0 + part * DIFF_HALF,
                                                  part * DIFF_HALF, DIFF_HALF), preferred_element_type=F32)
                   for part in range(2)]
        vt = lax.dot_general(_lane_selector(GROUP_WIDTH, VT_ROWS, c0, 0, HEAD_DIM, transposed=True), v_tile, _NT,
                             preferred_element_type=F32)
        yield
        k_h = jnp.where(lane == DIFF_REF_LANE, 1.0, k_h).astype(BF16)
        k_out[0, h] = k_h
        vt_out[0, h] = (vt + ones_rows).astype(BF16)
        q_parts = [(q_p * q_scale).astype(BF16) for q_p in q_parts]
        yield
        refs = [_diag_ref_max(q_p, k_h, diag * slopes[h]) for q_p in q_parts]
        yield
        qa_out[0, h] = jnp.concatenate([_with_ref_column(q_p, m_col, DIFF_REF_LANE)
                                        for q_p, m_col in zip(q_parts, refs)], axis=1)

    _run_interleaved([head_stages(h) for h in range(N_HEADS)])


def _diff_prep(proj, b, s, bm=512):
    bm = min(bm, s)
    nt = s // bm
    slopes = tuple(-LOG2E * 2.0 ** (-8.0 * (h + 1) / N_HEADS) for h in range(N_HEADS))
    spec = lambda unit: pl.BlockSpec((bm, GROUP_WIDTH), lambda bi, si: (bi * nt + si, unit // 2))
    return pl.pallas_call(
        functools.partial(_diff_prep_kernel, q_scale=LOG2E * DIFF_HALF ** -0.5, slopes=slopes),
        grid=(b, nt),
        in_specs=[spec(U_BQ), spec(U_BK), spec(U_BV)],
        out_specs=[pl.BlockSpec((1, N_HEADS, bm, 2 * V7X_LANES), lambda bi, si: (bi, 0, si, 0)),
                   pl.BlockSpec((1, N_HEADS, bm, V7X_LANES), lambda bi, si: (bi, 0, si, 0)),
                   pl.BlockSpec((1, N_HEADS, VT_ROWS, bm), lambda bi, si: (bi, 0, 0, si))],
        out_shape=[jax.ShapeDtypeStruct((b, N_HEADS, s, 2 * V7X_LANES), BF16),
                   jax.ShapeDtypeStruct((b, N_HEADS, s, V7X_LANES), BF16),
                   jax.ShapeDtypeStruct((b, N_HEADS, VT_ROWS, s), BF16)],
        compiler_params=_params("parallel", "parallel"),
        name="diff_prep",
    )(proj, proj, proj)


def _diff_flash_kernel(sc_ref, q_ref, k_ref, vt_ref, rel_ref, g_ref, o_ref, *, bk, unroll, out_scale):
    hi = pl.program_id(1)
    qi = pl.program_id(2)
    slope = sc_ref[hi]
    lam = sc_ref[N_HEADS]
    q1a = q_ref[0, 0, :, :V7X_LANES]
    q2a = q_ref[0, 0, :, V7X_LANES:]
    bq = q1a.shape[0]
    s_len = k_ref.shape[2]
    rel = rel_ref[...]
    q0 = (qi * bq).astype(F32)

    def block_inputs(i):
        k0 = pl.multiple_of(i * bk, bk)
        bias = jnp.abs(rel + (k0.astype(F32) - q0)) * slope
        return k_ref[0, 0, pl.ds(k0, bk), :], vt_ref[0, 0, :, pl.ds(k0, bk)], bias

    def body(i, carry):
        a1, a2 = carry
        for u in range(unroll):
            k_blk, vt_blk, bias = block_inputs(i * unroll + u)
            s1 = lax.dot_general(k_blk, q1a, _NT, preferred_element_type=F32)
            s2 = lax.dot_general(k_blk, q2a, _NT, preferred_element_type=F32)
            a1 = a1 + jnp.dot(vt_blk, jnp.exp2(s1 + bias).astype(BF16), preferred_element_type=F32)
            a2 = a2 + jnp.dot(vt_blk, jnp.exp2(s2 + bias).astype(BF16), preferred_element_type=F32)
        return a1, a2

    def finish(a1, a2):
        o = a1[:HEAD_DIM] / a1[HEAD_DIM:HEAD_DIM + 1] - lam * (a2[:HEAD_DIM] / a2[HEAD_DIM:HEAD_DIM + 1])
        ms = jnp.sum(o * o, axis=0, keepdims=True) * (1.0 / HEAD_DIM)
        o_ref[0] = (o * lax.rsqrt(ms + EPS) * g_ref[...] * out_scale).astype(o_ref.dtype)

    acc0 = jnp.zeros((VT_ROWS, bq), F32)
    a1, a2 = lax.fori_loop(0, s_len // (bk * unroll), body, (acc0, acc0))
    finish(a1, a2)

    @pl.when(_not_finite(a1) | _not_finite(a2))
    def _():
        q1 = _without_ref_column(q1a, DIFF_REF_LANE)
        q2 = _without_ref_column(q2a, DIFF_REF_LANE)

        def exact(i, carry):
            m1, x1, m2, x2 = carry
            k_blk, vt_blk, bias = block_inputs(i)
            m1, x1 = _online_block(k_blk, q1, vt_blk, m1, x1, bias)
            m2, x2 = _online_block(k_blk, q2, vt_blk, m2, x2, bias)
            return m1, x1, m2, x2

        m0 = jnp.full((1, bq), NEG_BIG, F32)
        _, x1, _, x2 = lax.fori_loop(0, s_len // bk, exact, (m0, acc0, m0, acc0))
        finish(x1, x2)


def _diff_flash(scalars, qa, k, vt, subln, layer_idx, bq=512, bk=1024, unroll=8):
    b, h, s, d = k.shape
    bq, bk = min(bq, s), min(bk, s)
    unroll = min(unroll, s // bk)
    lam_init = 0.8 - 0.6 * math.exp(-0.3 * layer_idx)
    rel = (np.arange(bk)[:, None] - np.arange(bq)[None, :]).astype(np.float32)
    return pl.pallas_call(
        functools.partial(_diff_flash_kernel, bk=bk, unroll=unroll, out_scale=1.0 - lam_init),
        grid=(b, h, s // bq),
        in_specs=[pl.BlockSpec(memory_space=pltpu.SMEM),
                  pl.BlockSpec((1, 1, bq, 2 * d), lambda bi, hi, qi: (bi, hi, qi, 0)),
                  pl.BlockSpec((1, 1, s, d), lambda bi, hi, qi: (bi, hi, 0, 0)),
                  pl.BlockSpec((1, 1, VT_ROWS, s), lambda bi, hi, qi: (bi, hi, 0, 0)),
                  pl.BlockSpec((bk, bq), lambda bi, hi, qi: (0, 0)),
                  pl.BlockSpec((HEAD_DIM, 1), lambda bi, hi, qi: (0, 0))],
        out_specs=pl.BlockSpec((1, HEAD_DIM, bq), lambda bi, hi, qi: (bi, hi, qi)),
        out_shape=jax.ShapeDtypeStruct((b, h * HEAD_DIM, s), BF16),
        compiler_params=_params("parallel", "parallel", "arbitrary"),
        name="diff_flash",
    )(scalars, qa, k, vt, jnp.asarray(rel), subln.reshape(HEAD_DIM, 1))


def _conv_silu_kernel(x_ref, prev_ref, next_ref, w_ref, v_ref, op_ref, qk_out, vt_out, opt_out, *, n_tiles):
    si = pl.program_id(1)
    x = x_ref[...]
    bm = x.shape[0]
    r = lax.broadcasted_iota(jnp.int32, (bm, bm), 0)
    c = lax.broadcasted_iota(jnp.int32, (bm, bm), 1)
    shift_dn = jnp.where(r == c + 1, 1.0, 0.0).astype(BF16)
    shift_up = jnp.where(r + 1 == c, 1.0, 0.0).astype(BF16)
    x_prev = jnp.dot(shift_dn, x, preferred_element_type=F32)
    x_next = jnp.dot(shift_up, x, preferred_element_type=F32)
    row = lax.broadcasted_iota(jnp.int32, x.shape, 0)
    halo_prev = jnp.where(si > 0, prev_ref[7:8, :].astype(F32), 0.0)
    halo_next = jnp.where(si < n_tiles - 1, next_ref[0:1, :].astype(F32), 0.0)
    x_prev = jnp.where(row == 0, halo_prev, x_prev)
    x_next = jnp.where(row == bm - 1, halo_next, x_next)
    w = w_ref[...]
    y = x_prev * w[0:1] + x.astype(F32) * w[1:2] + x_next * w[2:3]
    y = y * jax.nn.sigmoid(y)
    col = lax.broadcasted_iota(jnp.int32, x.shape, 1)
    qk_out[...] = jnp.where(col >= GROUP_WIDTH, y * (HEAD_DIM ** -0.5), y).astype(BF16)
    eye = _lane_selector(GROUP_WIDTH, GROUP_WIDTH, 0, 0, GROUP_WIDTH)
    vt_out[0] = lax.dot_general(eye, v_ref[...], _NT, preferred_element_type=F32).astype(BF16)
    opt_out[0] = lax.dot_general(eye, op_ref[...], _NT, preferred_element_type=F32).astype(BF16)


def _conv_silu(proj, b, s, conv_w, bm=256):
    bm = min(bm, s)
    nt = s // bm
    c = 2 * GROUP_WIDTH
    cb = U_CQK * V7X_LANES // c
    hb = bm // 8
    n8 = b * s // 8
    row = lambda bi, si: bi * nt + si
    cm_spec = pl.BlockSpec((1, GROUP_WIDTH, bm), lambda bi, si: (bi, 0, si))
    cm_shape = jax.ShapeDtypeStruct((b, GROUP_WIDTH, s), BF16)
    return pl.pallas_call(
        functools.partial(_conv_silu_kernel, n_tiles=nt),
        grid=(b, nt),
        in_specs=[pl.BlockSpec((bm, c), lambda bi, si: (row(bi, si), cb)),
                  pl.BlockSpec((8, c), lambda bi, si: (jnp.maximum(row(bi, si) * hb - 1, 0), cb)),
                  pl.BlockSpec((8, c), lambda bi, si: (jnp.minimum((row(bi, si) + 1) * hb, n8 - 1), cb)),
                  pl.BlockSpec((8, c), lambda bi, si: (0, 0)),
                  pl.BlockSpec((bm, GROUP_WIDTH), lambda bi, si: (row(bi, si), U_CV // 2)),
                  pl.BlockSpec((bm, GROUP_WIDTH), lambda bi, si: (row(bi, si), U_CO // 2))],
        out_specs=[pl.BlockSpec((bm, c), lambda bi, si: (row(bi, si), 0)), cm_spec, cm_spec],
        out_shape=[jax.ShapeDtypeStruct((b * s, c), BF16), cm_shape, cm_shape],
        compiler_params=_params("parallel", "parallel"),
        name="mlstm_conv",
    )(proj, proj, proj, jnp.concatenate([conv_w, jnp.zeros((5, c), conv_w.dtype)], axis=0), proj, proj)


def _log_sigmoid(x):
    return jnp.minimum(x, 0.0) - jnp.log(1.0 + jnp.exp(-jnp.abs(x)))


def _mlstm_chunk(qc, kc, vt1, gcol, grow, caug, m, *, backward):
    L = qc.shape[0]
    d = HEAD_DIM
    li_c, lf_c = gcol[0], _log_sigmoid(gcol[1])
    li_r, lf_r = grow[0], _log_sigmoid(grow[1])
    s_i = lax.broadcasted_iota(jnp.int32, (L, L), 0)
    j_i = lax.broadcasted_iota(jnp.int32, (L, L), 1)
    if backward:
        a_mask, b_mat, valid = j_i <= s_i, s_i >= j_i, s_i >= j_i
        bcum_row_idx, last_lane = L - 1, 0
    else:
        a_mask, b_mat, valid = j_i >= s_i, s_i <= j_i, s_i <= j_i
        bcum_row_idx, last_lane = 0, L - 1
    a = jnp.where(a_mask, lf_r, 0.0)
    a_hi = a.astype(BF16)
    a_lo = (a - a_hi.astype(F32)).astype(BF16)
    ones_b = jnp.where(b_mat, 1.0, 0.0).astype(BF16)
    yield
    e = jnp.dot(a_hi, ones_b, preferred_element_type=F32) + jnp.dot(a_lo, ones_b, preferred_element_type=F32)
    st = lax.dot_general(kc, qc, _NT, preferred_element_type=F32)
    cq = lax.dot_general(caug.astype(BF16), qc, _NT, preferred_element_type=F32)
    yield
    bcum_r = e[bcum_row_idx:bcum_row_idx + 1, :]
    dlog = jnp.where(valid, e + (li_c - lf_c), NEG_BIG)
    inter = bcum_r + m
    m_row = jnp.maximum(inter, jnp.max(dlog, axis=0, keepdims=True))
    w_intra = jnp.exp(dlog - m_row)
    w_inter = jnp.exp(inter - m_row)
    at = (st * w_intra).astype(BF16)
    b_last = bcum_r[:, last_lane:last_lane + 1]
    logw_end = b_last - bcum_r + li_r
    m_new = jnp.maximum(b_last + m, jnp.max(logw_end, axis=1, keepdims=True))
    w_end = jnp.exp(logw_end - m_new)
    decay = jnp.exp(b_last + m - m_new)
    vtw = (vt1.astype(F32) * w_end).astype(BF16)
    yield
    pv = jnp.dot(vt1, at, preferred_element_type=F32)
    u = jnp.dot(vtw, kc, preferred_element_type=F32)
    yield
    num = w_inter * cq[:d] + pv[:d]
    den = w_inter * cq[d:d + 1] + pv[d:d + 1]
    h = num / jnp.maximum(jnp.abs(den), jnp.exp(-m_row))
    caug = decay * caug + u[:d + 8]
    return h, caug, m_new


def _run_interleaved(stage_generators):
    results = [None] * len(stage_generators)
    live = list(range(len(stage_generators)))
    while live:
        for idx in list(live):
            try:
                next(stage_generators[idx])
            except StopIteration as done:
                results[idx] = done.value
                live.remove(idx)
    return results


def _mlstm_kernel(qk_ref, vt_ref, gc_ref, gb_ref, gr_ref, op_ref, ng_ref, o_ref, hf_sc, hb_sc, *, col_chunk):
    L = MLSTM_CHUNK
    d = HEAD_DIM
    s = qk_ref.shape[0]
    nc = s // L
    ones = jnp.ones((ONES_ROWS, L), BF16)
    lane = lax.broadcasted_iota(jnp.int32, (L, V7X_LANES), 1)

    def head_chunk(t0, h, gtile, gr, caug, m, backward):
        pair, sub = divmod(h, 2)
        mine = (lane >= sub * d) & (lane < (sub + 1) * d)
        q_pair = qk_ref[pl.ds(t0, L), pair * V7X_LANES:(pair + 1) * V7X_LANES]
        k_pair = qk_ref[pl.ds(t0, L), GROUP_WIDTH + pair * V7X_LANES:GROUP_WIDTH + (pair + 1) * V7X_LANES]
        qc = jnp.where(mine, q_pair, jnp.zeros_like(q_pair))
        kc = jnp.where(mine, k_pair, jnp.zeros_like(k_pair))
        vt1 = jnp.concatenate([vt_ref[0, h * d:(h + 1) * d, pl.ds(t0, L)], ones], axis=0)
        i_idx = (2 * N_HEADS if backward else 0) + h
        f_idx = i_idx + N_HEADS
        return _mlstm_chunk(qc, kc, vt1, (gtile[:, i_idx:i_idx + 1], gtile[:, f_idx:f_idx + 1]),
                            (gr[i_idx:i_idx + 1, :], gr[f_idx:f_idx + 1, :]), caug, m, backward=backward)

    def body(i, carry):
        chains, where = [], []
        for backward in (False, True):
            t0 = pl.multiple_of((nc - 1 - i if backward else i) * L, L)
            gtile = gc_ref[pl.ds(t0, L), :] + gb_ref[...]
            gr = gr_ref[0, :, pl.ds(t0, L)]
            for h in range(N_HEADS):
                caug, m = carry[(N_HEADS if backward else 0) + h]
                chains.append(head_chunk(t0, h, gtile, gr, caug, m, backward))
                where.append((hb_sc if backward else hf_sc, h, t0))
        new = []
        for (h_sc, h, t0), (hh, caug, m) in zip(where, _run_interleaved(chains)):
            h_sc[h * d:(h + 1) * d, pl.ds(t0, L)] = hh
            new.append((caug, m))
        return tuple(new)

    state0 = (jnp.zeros((d + 8, V7X_LANES), F32), jnp.zeros((1, 1), F32))
    lax.fori_loop(0, nc, body, (state0,) * (2 * N_HEADS))

    def finish(j, carry):
        c0 = pl.multiple_of(j * col_chunk, col_chunk)
        hsum = hf_sc[:, pl.ds(c0, col_chunk)] + hb_sc[:, pl.ds(c0, col_chunk)]
        normed = []
        for h in range(N_HEADS):
            x = hsum[h * d:(h + 1) * d]
            normed.append(x * lax.rsqrt(jnp.sum(x * x, axis=0, keepdims=True) * (1.0 / d) + EPS))
        hn = jnp.concatenate(normed, axis=0) * ng_ref[...]
        gate = jax.nn.sigmoid(op_ref[0, :, pl.ds(c0, col_chunk)].astype(F32))
        o_ref[0, :, pl.ds(c0, col_chunk)] = (hn * gate).astype(o_ref.dtype)
        return carry

    lax.fori_loop(0, s // col_chunk, finish, 0)


def _mlstm(qk, vt, gates, gate_bias, grow, opre_t, norm_g, b, s):
    c = GROUP_WIDTH
    once = pl.Buffered(1)
    return pl.pallas_call(
        functools.partial(_mlstm_kernel, col_chunk=min(1024, s)),
        grid=(b,),
        in_specs=[pl.BlockSpec((s, 2 * c), lambda bi: (bi, 0), pipeline_mode=once),
                  pl.BlockSpec((1, c, s), lambda bi: (bi, 0, 0), pipeline_mode=once),
                  pl.BlockSpec((s, V7X_LANES), lambda bi: (bi, 0), pipeline_mode=once),
                  pl.BlockSpec((1, V7X_LANES), lambda bi: (0, 0)),
                  pl.BlockSpec((1, 4 * N_HEADS, s), lambda bi: (bi, 0, 0), pipeline_mode=once),
                  pl.BlockSpec((1, c, s), lambda bi: (bi, 0, 0), pipeline_mode=once),
                  pl.BlockSpec((c, 1), lambda bi: (0, 0))],
        out_specs=pl.BlockSpec((1, c, s), lambda bi: (bi, 0, 0)),
        out_shape=jax.ShapeDtypeStruct((b, c, s), BF16),
        scratch_shapes=[pltpu.VMEM((c, s), F32), pltpu.VMEM((c, s), F32)],
        compiler_params=_params("parallel"),
        name="mlstm_scan",
    )(qk, vt, gates, gate_bias, grow, opre_t, norm_g.reshape(c, 1))


def _na_kernel(q_ref, k_ref, v_ref, bias_ref, o_ref, *, rows_per_step, n_rows, group):
    blk = pl.program_id(1)
    win = NA_ROWS * GRID_W
    lane = lax.broadcasted_iota(jnp.int32, (GRID_W, V7X_LANES), 1)
    def head_chain(q_pair, kw, vw, bias, sub):
        mine = (lane >= sub * HEAD_DIM) & (lane < (sub + 1) * HEAD_DIM)
        qh = jnp.where(mine, q_pair, jnp.zeros_like(q_pair))
        sc = lax.dot_general(qh, kw, _NT, preferred_element_type=F32)
        yield
        sc = sc * (HEAD_DIM ** -0.5) + bias
        mx = jnp.max(sc, axis=-1, keepdims=True)
        p = jnp.exp(sc - mx)
        p = (p / jnp.sum(p, axis=-1, keepdims=True)).astype(BF16)
        yield
        return jnp.dot(p, vw, preferred_element_type=F32)

    for i0 in range(0, rows_per_step, group):
        chains, where = [], []
        for i in range(i0, i0 + group):
            r = blk * rows_per_step + i
            r0 = jnp.clip(r - NA_ROWS // 2, 0, n_rows - NA_ROWS)
            dsel = r - r0
            k0 = pl.multiple_of(r0 * GRID_W, GRID_W)
            rows = slice(i * GRID_W, (i + 1) * GRID_W)
            for pair in range(N_HEADS // 2):
                lanes = slice(pair * V7X_LANES, (pair + 1) * V7X_LANES)
                q_pair = q_ref[rows, lanes]
                kw = k_ref[pl.ds(k0, win), lanes]
                vw = v_ref[pl.ds(k0, win), lanes]
                for sub in range(2):
                    chains.append(head_chain(q_pair, kw, vw, bias_ref[2 * pair + sub, dsel], sub))
                where.append((rows, lanes))
        outs = _run_interleaved(chains)
        for n, (rows, lanes) in enumerate(where):
            o_ref[rows, lanes] = jnp.where(lane < HEAD_DIM, outs[2 * n], outs[2 * n + 1]).astype(o_ref.dtype)


def _na(proj, b, s, bias, rows_per_step=8, group=4):
    n_rows = s // GRID_W
    rows_per_step = min(rows_per_step, n_rows)
    bm = rows_per_step * GRID_W
    nt = s // bm
    return pl.pallas_call(
        functools.partial(_na_kernel, rows_per_step=rows_per_step, n_rows=n_rows,
                          group=math.gcd(group, rows_per_step)),
        grid=(b, nt),
        in_specs=[pl.BlockSpec((bm, GROUP_WIDTH), lambda bi, ri: (bi * nt + ri, U_DQ // 2)),
                  pl.BlockSpec((s, GROUP_WIDTH), lambda bi, ri: (bi, U_DK // 2)),
                  pl.BlockSpec((s, GROUP_WIDTH), lambda bi, ri: (bi, U_DV // 2)),
                  pl.BlockSpec(bias.shape, lambda bi, ri: (0, 0, 0, 0))],
        out_specs=pl.BlockSpec((bm, GROUP_WIDTH), lambda bi, ri: (bi * nt + ri, 0)),
        out_shape=jax.ShapeDtypeStruct((b * s, GROUP_WIDTH), BF16),
        compiler_params=_params("parallel", "arbitrary"),
        name="na_attn",
    )(proj, proj, proj, bias)


def _na_bias_table(rpb, n_rows):
    wr = min(NA_ROWS, n_rows)
    cols = np.arange(GRID_W)
    col_start = np.clip(cols - NA_COLS // 2, 0, GRID_W - NA_COLS)
    ck = np.arange(GRID_W)[None, :]
    valid = (ck >= col_start[:, None]) & (ck < col_start[:, None] + NA_COLS)
    crel = ck - cols[:, None] + NA_COLS - 1
    rrel = np.arange(wr)[None, :] - np.arange(wr)[:, None] + NA_ROWS - 1
    c_sel = ((crel[None] == np.arange(2 * NA_COLS - 1)[:, None, None]) & valid[None]).astype(np.float32)
    r_sel = (rrel[None] == np.arange(2 * NA_ROWS - 1)[:, None, None]).astype(np.float32)
    t = jnp.einsum('hab,adw,bqk->hdqwk', rpb.astype(F32), r_sel, c_sel, precision=lax.Precision.HIGHEST)
    t = t + np.where(valid, 0.0, NEG_BIG).astype(np.float32)[None, None, :, None, :]
    return t.reshape(rpb.shape[0], wr, GRID_W, wr * GRID_W)


def _outproj_kernel(ya_ref, yb_ref, yc_ref, yd_ref, x_ref, w_ref, o_ref):
    acc = x_ref[...] + jnp.dot(yd_ref[...], w_ref[3], preferred_element_type=F32)
    for g, y_ref in enumerate((ya_ref, yb_ref, yc_ref)):
        acc = acc + lax.dot_general(y_ref[0], w_ref[g], (((0,), (0,)), ((), ())),
                                    preferred_element_type=F32)
    o_ref[...] = acc


def _outproj(ya, yb, yc, yd, x, w, b, s, bm=512):
    bm = min(bm, s)
    nt = s // bm
    d = x.shape[1]
    yspec = pl.BlockSpec((1, GROUP_WIDTH, bm), lambda bi, si: (bi, 0, si))
    return pl.pallas_call(
        _outproj_kernel,
        grid=(b, nt),
        in_specs=[yspec, yspec, yspec,
                  pl.BlockSpec((bm, GROUP_WIDTH), lambda bi, si: (bi * nt + si, 0)),
                  pl.BlockSpec((bm, d), lambda bi, si: (bi * nt + si, 0)),
                  pl.BlockSpec(w.shape, lambda bi, si: (0, 0, 0))],
        out_specs=pl.BlockSpec((bm, d), lambda bi, si: (bi * nt + si, 0)),
        out_shape=jax.ShapeDtypeStruct(x.shape, F32),
        compiler_params=_params("parallel", "parallel"),
        name="outproj",
    )(ya, yb, yc, yd, x, w)


def _ffn_kernel(x_ref, g_ref, wg_ref, wu_ref, wd_ref, o_ref, hn_sc, acc_sc):
    f = pl.program_id(1)

    @pl.when(f == 0)
    def _():
        x = x_ref[...]
        hn_sc[...] = (_rms(x, x.shape[-1]) * g_ref[...]).astype(BF16)
        acc_sc[...] = jnp.zeros_like(acc_sc)

    hn = hn_sc[...]
    gate = jnp.dot(hn, wg_ref[...], preferred_element_type=F32)
    up = jnp.dot(hn, wu_ref[...], preferred_element_type=F32)
    act = (gate * jax.nn.sigmoid(gate) * up).astype(BF16)
    acc_sc[...] += jnp.dot(act, wd_ref[...], preferred_element_type=F32)

    @pl.when(f == pl.num_programs(1) - 1)
    def _():
        o_ref[...] = x_ref[...] + acc_sc[...]


def _ffn(x, g, wg, wu, wd, bm=512, bf=1408):
    n, d = x.shape
    ff = wg.shape[1]
    bm = min(bm, n)
    return pl.pallas_call(
        _ffn_kernel,
        grid=(n // bm, ff // bf),
        in_specs=[pl.BlockSpec((bm, d), lambda i, f: (i, 0)),
                  pl.BlockSpec((1, d), lambda i, f: (0, 0)),
                  pl.BlockSpec((d, bf), lambda i, f: (0, f)),
                  pl.BlockSpec((d, bf), lambda i, f: (0, f)),
                  pl.BlockSpec((bf, d), lambda i, f: (f, 0))],
        out_specs=pl.BlockSpec((bm, d), lambda i, f: (i, 0)),
        out_shape=jax.ShapeDtypeStruct(x.shape, F32),
        scratch_shapes=[pltpu.VMEM((bm, d), BF16), pltpu.VMEM((bm, d), F32)],
        compiler_params=_params("parallel", "arbitrary"),
        name="ffn",
    )(x, g.reshape(1, d), wg, wu, wd)


R_E1, R_E2, R_W1, R_W2, R_RANK1, R_RANK2 = range(6)


def _lane_pack(lane, cols):
    out = jnp.zeros(lane.shape, F32)
    for idx, col in cols:
        out = out + jnp.where(lane == idx, col, 0.0)
    return out


def _router_kernel(x_ref, g_ref, wr_ref, hn_ref, route_ref, counts_ref, carry_sc):
    @pl.when(pl.program_id(0) == 0)
    def _():
        carry_sc[...] = jnp.zeros_like(carry_sc)

    x = x_ref[...]
    bm = x.shape[0]
    hn = _rms(x, x.shape[-1]) * g_ref[...]
    hn_ref[...] = hn
    logits = jnp.dot(hn, wr_ref[...], preferred_element_type=F32, precision=lax.Precision.HIGHEST)
    lane = lax.broadcasted_iota(jnp.int32, logits.shape, 1)
    logits = jnp.where(lane < N_EXPERTS, logits, NEG_BIG)
    m1 = jnp.max(logits, axis=-1, keepdims=True)
    i1 = jnp.min(jnp.where(logits == m1, lane, V7X_LANES), axis=-1, keepdims=True)
    rest = jnp.where(lane == i1, NEG_BIG, logits)
    m2 = jnp.max(rest, axis=-1, keepdims=True)
    i2 = jnp.min(jnp.where(rest == m2, lane, V7X_LANES), axis=-1, keepdims=True)
    e2 = jnp.exp(m2 - m1)
    w1 = 1.0 / (1.0 + e2)
    w2 = e2 / (1.0 + e2)
    hot1 = jnp.where(lane == i1, 1.0, 0.0)
    hot2 = jnp.where(lane == i2, 1.0, 0.0)
    r_i = lax.broadcasted_iota(jnp.int32, (bm, bm), 0)
    c_i = lax.broadcasted_iota(jnp.int32, (bm, bm), 1)
    below = jnp.where(c_i < r_i, 1.0, 0.0).astype(BF16)
    before1 = jnp.dot(below, hot1.astype(BF16), preferred_element_type=F32)
    before2 = jnp.dot(below, hot2.astype(BF16), preferred_element_type=F32)
    cnt1 = jnp.sum(hot1, axis=0, keepdims=True)
    cnt2 = jnp.sum(hot2, axis=0, keepdims=True)
    carry = carry_sc[...]
    rank1 = jnp.sum(hot1 * (before1 + carry), axis=-1, keepdims=True)
    rank2 = jnp.sum(hot2 * (before2 + carry + cnt1), axis=-1, keepdims=True)
    carry = carry + cnt1 + cnt2
    carry_sc[...] = carry
    counts_ref[...] = carry
    route_ref[...] = _lane_pack(lane, ((R_E1, i1.astype(F32)), (R_E2, i2.astype(F32)), (R_W1, w1), (R_W2, w2),
                                       (R_RANK1, rank1), (R_RANK2, rank2)))


def _router(x, g, w_router, bm=512):
    n, d = x.shape
    bm = min(bm, n)
    wr = jnp.concatenate([w_router, jnp.zeros((d, V7X_LANES - N_EXPERTS), w_router.dtype)], axis=1)
    return pl.pallas_call(
        _router_kernel,
        grid=(n // bm,),
        in_specs=[pl.BlockSpec((bm, d), lambda i: (i, 0)),
                  pl.BlockSpec((1, d), lambda i: (0, 0)),
                  pl.BlockSpec((d, V7X_LANES), lambda i: (0, 0))],
        out_specs=[pl.BlockSpec((bm, d), lambda i: (i, 0)),
                   pl.BlockSpec((bm, V7X_LANES), lambda i: (i, 0)),
                   pl.BlockSpec((1, V7X_LANES), lambda i: (0, 0))],
        out_shape=[jax.ShapeDtypeStruct((n, d), F32),
                   jax.ShapeDtypeStruct((n, V7X_LANES), F32),
                   jax.ShapeDtypeStruct((1, V7X_LANES), F32)],
        scratch_shapes=[pltpu.VMEM((1, V7X_LANES), F32)],
        compiler_params=_params("arbitrary"),
        name="moe_router",
    )(x, g.reshape(1, d), wr)


def _route_plan(route, counts, tm):
    n = route.shape[0]
    counts = counts[0, :N_EXPERTS].astype(jnp.int32)
    padded = ((counts + tm - 1) // tm) * tm
    g_end = jnp.cumsum(padded)
    g_start = (g_end - padded).astype(F32)
    experts = jnp.arange(N_EXPERTS, dtype=F32)[None, :]
    start1 = jnp.sum(jnp.where(route[:, R_E1:R_E1 + 1] == experts, g_start[None, :], 0.0), axis=1)
    start2 = jnp.sum(jnp.where(route[:, R_E2:R_E2 + 1] == experts, g_start[None, :], 0.0), axis=1)
    pos = jnp.stack([start1 + route[:, R_RANK1], start2 + route[:, R_RANK2]], axis=1).astype(jnp.int32)
    n_tiles = 2 * n // tm + N_EXPERTS
    tile_start = jnp.arange(n_tiles, dtype=jnp.int32) * tm
    tile_expert = jnp.minimum(jnp.sum(tile_start[:, None] >= g_end[None, :], axis=1), N_EXPERTS - 1)
    n_used = (g_end[-1:] // tm).astype(jnp.int32)
    return pos.reshape(-1), tile_expert.astype(jnp.int32), n_used, n_tiles


def _row_copies(pos_ref, r, src_at, dst_at, sem):
    copies = []
    for c in range(2):
        p = pos_ref[2 * r + c]
        copies.append(pltpu.make_async_copy(src_at(r, c, p), dst_at(r, c, p), sem))
    return copies


def _move_rows(pos_ref, n_rows, src_at, dst_at, sem):
    def start(r, carry):
        for cp in _row_copies(pos_ref, r, src_at, dst_at, sem):
            cp.start()
        return carry

    def wait(r, carry):
        for cp in _row_copies(pos_ref, r, src_at, dst_at, sem):
            cp.wait()
        return carry

    lax.fori_loop(0, n_rows, start, 0, unroll=8)
    lax.fori_loop(0, n_rows, wait, 0, unroll=8)


def _dispatch_kernel(pos_ref, hn_ref, xg_in_ref, xg_ref, sem):
    del xg_in_ref
    _move_rows(pos_ref, hn_ref.shape[0],
               lambda r, c, p: hn_ref.at[pl.ds(r, 1)],
               lambda r, c, p: xg_ref.at[pl.ds(p, 1)], sem)


def _dispatch(pos, hn, n_rows, bm=512):
    n, d = hn.shape
    bm = min(bm, n)
    return pl.pallas_call(
        _dispatch_kernel,
        grid=(n // bm,),
        in_specs=[pl.BlockSpec((2 * bm,), lambda i: (i,), memory_space=pltpu.SMEM),
                  pl.BlockSpec((bm, d), lambda i: (i, 0)),
                  pl.BlockSpec(memory_space=pl.ANY)],
        out_specs=pl.BlockSpec(memory_space=pl.ANY),
        out_shape=jax.ShapeDtypeStruct((n_rows, d), F32),
        scratch_shapes=[pltpu.SemaphoreType.DMA(())],
        input_output_aliases={2: 0},
        compiler_params=_params("arbitrary"),
        name="moe_dispatch",
    )(pos, hn, jnp.zeros((n_rows, d), F32))


def _expert_ffn_kernel(te_ref, nu_ref, xg_ref, wg_ref, wu_ref, wd_ref, o_ref, hn_sc, acc_sc):
    del te_ref
    t = pl.program_id(0)
    f = pl.program_id(1)
    last = pl.num_programs(1) - 1
    used = t < nu_ref[0]

    @pl.when(used & (f == 0))
    def _():
        hn_sc[...] = xg_ref[...].astype(BF16)
        acc_sc[...] = jnp.zeros_like(acc_sc)

    @pl.when(used)
    def _():
        hn = hn_sc[...]
        gate = jnp.dot(hn, wg_ref[0], preferred_element_type=F32)
        up = jnp.dot(hn, wu_ref[0], preferred_element_type=F32)
        act = (gate * jax.nn.sigmoid(gate) * up).astype(BF16)
        acc_sc[...] += jnp.dot(act, wd_ref[0], preferred_element_type=F32)

    @pl.when(used & (f == last))
    def _():
        o_ref[...] = acc_sc[...]

    @pl.when(jnp.logical_not(used) & (f == last))
    def _():
        o_ref[...] = jnp.zeros_like(o_ref)


def _expert_ffn(tile_expert, n_used, xg, wg, wu, wd, tm, bf=896):
    rows, d = xg.shape
    ff = wg.shape[2]
    grid_spec = pltpu.PrefetchScalarGridSpec(
        num_scalar_prefetch=2,
        grid=(rows // tm, ff // bf),
        in_specs=[pl.BlockSpec((tm, d), lambda t, f, te, nu: (t, 0)),
                  pl.BlockSpec((1, d, bf), lambda t, f, te, nu: (te[t], 0, f)),
                  pl.BlockSpec((1, d, bf), lambda t, f, te, nu: (te[t], 0, f)),
                  pl.BlockSpec((1, bf, d), lambda t, f, te, nu: (te[t], f, 0))],
        out_specs=pl.BlockSpec((tm, d), lambda t, f, te, nu: (t, 0)),
        scratch_shapes=[pltpu.VMEM((tm, d), BF16), pltpu.VMEM((tm, d), F32)])
    return pl.pallas_call(
        _expert_ffn_kernel,
        grid_spec=grid_spec,
        out_shape=jax.ShapeDtypeStruct((rows, d), F32),
        compiler_params=_params("arbitrary", "arbitrary"),
        name="moe_ffn",
    )(tile_expert, n_used, xg, wg, wu, wd)


def _combine_kernel(pos_ref, x_ref, route_ref, g_ref, yg_ref, o_ref, buf, sem, *, final_norm):
    _move_rows(pos_ref, x_ref.shape[0],
               lambda r, c, p: yg_ref.at[pl.ds(p, 1)],
               lambda r, c, p: buf.at[c, pl.ds(r, 1)], sem)
    route = route_ref[...]
    y = x_ref[...] + route[:, R_W1:R_W1 + 1] * buf[0] + route[:, R_W2:R_W2 + 1] * buf[1]
    if final_norm:
        y = _rms(y, y.shape[-1]) * g_ref[...]
    o_ref[...] = y


def _combine(pos, x, route, yg, g_final, final_norm, bm=512):
    n, d = x.shape
    bm = min(bm, n)
    return pl.pallas_call(
        functools.partial(_combine_kernel, final_norm=final_norm),
        grid=(n // bm,),
        in_specs=[pl.BlockSpec((2 * bm,), lambda i: (i,), memory_space=pltpu.SMEM),
                  pl.BlockSpec((bm, d), lambda i: (i, 0)),
                  pl.BlockSpec((bm, V7X_LANES), lambda i: (i, 0)),
                  pl.BlockSpec((1, d), lambda i: (0, 0)),
                  pl.BlockSpec(memory_space=pl.ANY)],
        out_specs=pl.BlockSpec((bm, d), lambda i: (i, 0)),
        out_shape=jax.ShapeDtypeStruct(x.shape, F32),
        scratch_shapes=[pltpu.VMEM((2, bm, d), F32), pltpu.SemaphoreType.DMA(())],
        compiler_params=_params("arbitrary"),
        name="moe_combine",
    )(pos, x, route, g_final.reshape(1, d), yg)


def _moe(x, g, w_router, wg, wu, wd, g_final, final_norm, tm=512):
    hn, route, counts = _router(x, g, w_router)
    pos, tile_expert, n_used, n_tiles = _route_plan(route, counts, tm)
    xg = _dispatch(pos, hn, n_tiles * tm)
    yg = _expert_ffn(tile_expert, n_used, xg, wg, wu, wd, tm)
    return _combine(pos, x, route, yg, g_final, final_norm)


def _final_norm_kernel(x_ref, g_ref, o_ref):
    x = x_ref[...]
    o_ref[...] = _rms(x, x.shape[-1]) * g_ref[...]


def _final_norm(x, g, bm=1024):
    n, d = x.shape
    bm = min(bm, n)
    return pl.pallas_call(
        _final_norm_kernel,
        grid=(n // bm,),
        in_specs=[pl.BlockSpec((bm, d), lambda i: (i, 0)), pl.BlockSpec((1, d), lambda i: (0, 0))],
        out_specs=pl.BlockSpec((bm, d), lambda i: (i, 0)),
        out_shape=jax.ShapeDtypeStruct(x.shape, F32),
        compiler_params=_params("parallel"),
        name="final_norm",
    )(x, g.reshape(1, d))


def _token_mix(x, i, b, s, p):
    w_main, w_gates = _prep_w_in(p['w_in'][i])
    proj, gates = _inproj(x, p['norm_mix'][i], w_main, w_gates)

    cos_t, sin_t = _rope_tables(s)
    q_a, k_a, vt_a = _mla_prep(proj, b, s, cos_t, sin_t, *_prep_mla_weights(
        p['mla_q_norm'][i], p['mla_kv_norm'][i], p['mla_w_uq'][i], p['mla_w_ukv'][i]))
    y_a = _mla_flash(q_a, k_a, vt_a)

    lp = p['diff_lambda'][i].astype(F32)
    lam_init = 0.8 - 0.6 * math.exp(-0.3 * i)
    lam = jnp.exp(jnp.sum(lp[0] * lp[1])) - jnp.exp(jnp.sum(lp[2] * lp[3])) + lam_init
    slopes = 2.0 ** (-8.0 * jnp.arange(1, N_HEADS + 1, dtype=F32) / N_HEADS)
    scalars = jnp.concatenate([-slopes * LOG2E, lam[None]]).astype(F32)
    qa_b, k_b, vt_b = _diff_prep(proj, b, s)
    y_b = _diff_flash(scalars, qa_b, k_b, vt_b, p['diff_subln'][i], i)

    qk_c, vt_c, opt_c = _conv_silu(proj, b, s, p['mlstm_conv'][i])
    n_gates = 4 * N_HEADS
    gate_bias = jnp.concatenate([p['mlstm_gate_bias'][i], jnp.zeros((V7X_LANES - n_gates,), F32)])[None, :]
    grow = (gates[:, :n_gates] + gate_bias[:, :n_gates]).reshape(b, s, n_gates).transpose(0, 2, 1)
    y_c = _mlstm(qk_c, vt_c, gates, gate_bias, grow, opt_c, p['mlstm_norm'][i], b, s)

    y_d = _na(proj, b, s, _na_bias_table(p['na_rpb'][i], s // GRID_W))

    w_out = p['w_out'][i].reshape(4, GROUP_WIDTH, -1).astype(BF16)
    return _outproj(y_a, y_b, y_c, y_d, x, w_out, b, s)


def _trunk(x, p, depth):
    b, s, d = x.shape
    x = x.reshape(b * s, d)
    for i in range(depth):
        x = _token_mix(x, i, b, s, p)
        j = i // 2
        if i % 2 == 0:
            x = _ffn(x, p['norm_ffn'][i], p['ffn_w_gate'][j].astype(BF16), p['ffn_w_up'][j].astype(BF16),
                     p['ffn_w_down'][j].astype(BF16))
        else:
            x = _moe(x, p['norm_ffn'][i], p['moe_router'][j], p['moe_w_gate'][j].astype(BF16),
                     p['moe_w_up'][j].astype(BF16), p['moe_w_down'][j].astype(BF16),
                     p['norm_final'], final_norm=(i == depth - 1))
    if depth % 2 == 1:
        x = _final_norm(x, p['norm_final'])
    return x.reshape(b, s, d)


def kernel(x_prompt, x_sample, norm_mix, norm_ffn, w_in, w_out, mla_q_norm, mla_kv_norm, mla_w_uq, mla_w_ukv,
           diff_lambda, diff_subln, mlstm_conv, mlstm_gate_bias, mlstm_norm, na_rpb, ffn_w_gate, ffn_w_up,
           ffn_w_down, moe_router, moe_w_gate, moe_w_up, moe_w_down, norm_final):
    p = dict(norm_mix=norm_mix, norm_ffn=norm_ffn, w_in=w_in, w_out=w_out, mla_q_norm=mla_q_norm,
             mla_kv_norm=mla_kv_norm, mla_w_uq=mla_w_uq, mla_w_ukv=mla_w_ukv, diff_lambda=diff_lambda,
             diff_subln=diff_subln, mlstm_conv=mlstm_conv, mlstm_gate_bias=mlstm_gate_bias,
             mlstm_norm=mlstm_norm, na_rpb=na_rpb, ffn_w_gate=ffn_w_gate, ffn_w_up=ffn_w_up,
             ffn_w_down=ffn_w_down, moe_router=moe_router, moe_w_gate=moe_w_gate, moe_w_up=moe_w_up,
             moe_w_down=moe_w_down, norm_final=norm_final)
    depth = norm_mix.shape[0]
    nb = x_prompt.shape[0]
    y = _trunk(jnp.concatenate([x_prompt, x_sample], axis=0), p, depth)
    return (y[:nb], y[nb:])
```

```python
import functools
import math

import numpy as np
import jax
import jax.numpy as jnp
from jax import lax
from jax.experimental import pallas as pl
from jax.experimental.pallas import tpu as pltpu

F32 = jnp.float32
BF16 = jnp.bfloat16

V7X_LANES = 128
V7X_MXU_COLS = 256
V7X_VMEM_LIMIT_BYTES = 56 * 1024 * 1024

EPS = 1e-6
LOG2E = 1.4426950408889634
NEG_BIG = -1e30

HEAD_DIM = 64
N_HEADS = 4
GROUP_WIDTH = 256
MLA_Q_LORA = 192
MLA_KV_LORA = 128
MLA_NOPE = 64
MLA_ROPE = 32
ROPE_THETA = 10000.0
DIFF_HALF = 32
MLSTM_CHUNK = 128
GRID_W = 64
NA_ROWS = 8
NA_COLS = 16
N_EXPERTS = 8
ONES_ROWS = 16
VT_ROWS = HEAD_DIM + ONES_ROWS
MLA_REF_LANE = MLA_NOPE + MLA_ROPE
DIFF_REF_LANE = HEAD_DIM
DIFF_POS_LANE = DIFF_REF_LANE + 1

IN_SECTIONS = (192, 128, 32, 256, 256, 256, 512, 256, 256, 16, 256, 256, 256)
U_BQ, U_BK, U_BV, U_DQ, U_DK, U_DV, U_CQK, U_CV, U_CO, U_ACQ, U_ACKV, U_AKR, U_AKRR = (
    0, 2, 4, 6, 8, 10, 12, 16, 18, 20, 22, 23, 24)
PROJ_COLS = 25 * V7X_LANES


def _params(*sem):
    return pltpu.CompilerParams(dimension_semantics=sem, vmem_limit_bytes=V7X_VMEM_LIMIT_BYTES)


def _rms(x, n):
    return x * lax.rsqrt(jnp.sum(x * x, axis=-1, keepdims=True) * (1.0 / n) + EPS)


def _inproj_kernel(x_ref, g_ref, w_ref, wg_ref, o_ref, og_ref, *, col_chunk):
    x = x_ref[...]
    hn = (_rms(x, x.shape[-1]) * g_ref[...]).astype(BF16)
    for c in range(0, o_ref.shape[1], col_chunk):
        w = min(col_chunk, o_ref.shape[1] - c)
        o_ref[:, c:c + w] = jnp.dot(hn, w_ref[:, c:c + w], preferred_element_type=F32).astype(BF16)
    og_ref[...] = jnp.dot(hn, wg_ref[...], preferred_element_type=F32)


def _inproj(x, g, w_main, w_gates, bm=512):
    n, d = x.shape
    bm = min(bm, n)
    return pl.pallas_call(
        functools.partial(_inproj_kernel, col_chunk=4 * V7X_MXU_COLS),
        grid=(n // bm,),
        in_specs=[pl.BlockSpec((bm, d), lambda i: (i, 0)),
                  pl.BlockSpec((1, d), lambda i: (0, 0)),
                  pl.BlockSpec(w_main.shape, lambda i: (0, 0)),
                  pl.BlockSpec(w_gates.shape, lambda i: (0, 0))],
        out_specs=[pl.BlockSpec((bm, PROJ_COLS), lambda i: (i, 0)),
                   pl.BlockSpec((bm, V7X_LANES), lambda i: (i, 0))],
        out_shape=[jax.ShapeDtypeStruct((n, PROJ_COLS), BF16),
                   jax.ShapeDtypeStruct((n, V7X_LANES), F32)],
        compiler_params=_params("parallel"),
        name="inproj",
    )(x, g.reshape(1, d), w_main, w_gates)


def _prep_w_in(w):
    d = w.shape[0]
    offs = np.cumsum((0,) + IN_SECTIONS)
    (a_cq, a_ckv, a_kr, b_q, b_k, b_v, c_qk, c_v, c_o, c_g, d_q, d_k, d_v) = [
        w[:, offs[i]:offs[i + 1]] for i in range(len(IN_SECTIONS))]
    z = lambda k: jnp.zeros((d, k), w.dtype)
    half = MLA_ROPE // 2
    a_kr_rot = jnp.concatenate([a_kr[:, half:], a_kr[:, :half]], axis=1)
    main = jnp.concatenate([b_q, b_k, b_v, d_q, d_k, d_v, c_qk, c_v, c_o,
                            a_cq, z(64), a_ckv,
                            z(64), a_kr, z(32),
                            z(64), a_kr_rot, z(32)], axis=1)
    gates = jnp.concatenate([c_g, z(V7X_LANES - 16)], axis=1)
    return main.astype(BF16), gates.astype(BF16)


def _mla_prep_kernel(cq_ref, ckv_ref, kr_ref, krr_ref, cos_ref, sin_ref, gq_ref, gkv_ref,
                     wq_ref, wqr_ref, wkn_ref, wvt_ref, ones_ref, q_out, k_out, vt_out, *, q_scale):
    cos = cos_ref[...]
    sin = sin_ref[...]
    qn = (_rms(cq_ref[...].astype(F32), MLA_Q_LORA) * gq_ref[...]).astype(BF16)
    qa = jnp.dot(qn, wq_ref[...], preferred_element_type=F32)
    qr = jnp.dot(qn, wqr_ref[...], preferred_element_type=F32)
    kvn = (_rms(ckv_ref[...].astype(F32), MLA_KV_LORA) * gkv_ref[...]).astype(BF16)
    kn = jnp.dot(kvn, wkn_ref[...], preferred_element_type=F32)
    k_rope = kr_ref[...].astype(F32) * cos + krr_ref[...].astype(F32) * sin
    k_rope = jnp.where(lax.broadcasted_iota(jnp.int32, k_rope.shape, 1) == MLA_REF_LANE, 1.0, k_rope)
    def head_stages(h):
        sl = slice(h * V7X_LANES, (h + 1) * V7X_LANES)
        q_h = ((qa[:, sl] * cos + qr[:, sl] * sin) * q_scale).astype(BF16)
        k_h = (kn[:, sl] + k_rope).astype(BF16)
        k_out[0, h] = k_h
        vt = lax.dot_general(wvt_ref[h], kvn, _NT, preferred_element_type=F32)
        yield
        vt_out[0, h] = (vt + ones_ref[...]).astype(BF16)
        m_col = _diag_ref_max(q_h, k_h)
        yield
        q_out[0, h] = _with_ref_column(q_h, m_col, MLA_REF_LANE)

    _run_interleaved([head_stages(h) for h in range(N_HEADS)])


def _mla_prep(proj, b, s, cos_t, sin_t, gq, gkv, wq, wqr, wkn, wvt, bm=512):
    bm = min(bm, s)
    nt = s // bm
    row = lambda bi, si: bi * nt + si
    ones_col = jnp.concatenate([jnp.zeros((HEAD_DIM, 1), F32), jnp.ones((ONES_ROWS, 1), F32)], axis=0)
    full = lambda a: pl.BlockSpec(a.shape, lambda bi, si: (0,) * a.ndim)
    return pl.pallas_call(
        functools.partial(_mla_prep_kernel, q_scale=LOG2E * (MLA_NOPE + MLA_ROPE) ** -0.5),
        grid=(b, nt),
        in_specs=[pl.BlockSpec((bm, 2 * V7X_LANES), lambda bi, si: (row(bi, si), U_ACQ // 2)),
                  pl.BlockSpec((bm, V7X_LANES), lambda bi, si: (row(bi, si), U_ACKV)),
                  pl.BlockSpec((bm, V7X_LANES), lambda bi, si: (row(bi, si), U_AKR)),
                  pl.BlockSpec((bm, V7X_LANES), lambda bi, si: (row(bi, si), U_AKRR)),
                  pl.BlockSpec((bm, V7X_LANES), lambda bi, si: (si, 0)),
                  pl.BlockSpec((bm, V7X_LANES), lambda bi, si: (si, 0)),
                  full(gq), full(gkv), full(wq), full(wqr), full(wkn), full(wvt), full(ones_col)],
        out_specs=[pl.BlockSpec((1, N_HEADS, bm, V7X_LANES), lambda bi, si: (bi, 0, si, 0)),
                   pl.BlockSpec((1, N_HEADS, bm, V7X_LANES), lambda bi, si: (bi, 0, si, 0)),
                   pl.BlockSpec((1, N_HEADS, VT_ROWS, bm), lambda bi, si: (bi, 0, 0, si))],
        out_shape=[jax.ShapeDtypeStruct((b, N_HEADS, s, V7X_LANES), BF16),
                   jax.ShapeDtypeStruct((b, N_HEADS, s, V7X_LANES), BF16),
                   jax.ShapeDtypeStruct((b, N_HEADS, VT_ROWS, s), BF16)],
        compiler_params=_params("parallel", "parallel"),
        name="mla_prep",
    )(proj, proj, proj, proj, cos_t, sin_t, gq, gkv, wq, wqr, wkn, wvt, ones_col)


def _prep_mla_weights(q_norm, kv_norm, w_uq, w_ukv):
    half = MLA_ROPE // 2
    dq = MLA_NOPE + MLA_ROPE
    wq_h = w_uq.reshape(MLA_Q_LORA, N_HEADS, dq)
    zq = lambda k: jnp.zeros((MLA_Q_LORA, N_HEADS, k), w_uq.dtype)
    rope_cols = wq_h[:, :, MLA_NOPE:]
    rope_rot = jnp.concatenate([rope_cols[:, :, half:], rope_cols[:, :, :half]], axis=2)
    wq = jnp.concatenate([wq_h, zq(V7X_LANES - dq)], axis=2).reshape(MLA_Q_LORA, N_HEADS * V7X_LANES)
    wqr = jnp.concatenate([zq(MLA_NOPE), rope_rot, zq(V7X_LANES - dq)], axis=2).reshape(
        MLA_Q_LORA, N_HEADS * V7X_LANES)
    pad_rows = jnp.zeros((2 * V7X_LANES - MLA_Q_LORA, N_HEADS * V7X_LANES), w_uq.dtype)
    wq = jnp.concatenate([wq, pad_rows], axis=0).astype(BF16)
    wqr = jnp.concatenate([wqr, pad_rows], axis=0).astype(BF16)
    wkv_h = w_ukv.reshape(MLA_KV_LORA, N_HEADS, MLA_NOPE + HEAD_DIM)
    wkn = jnp.concatenate([wkv_h[:, :, :MLA_NOPE],
                           jnp.zeros((MLA_KV_LORA, N_HEADS, V7X_LANES - MLA_NOPE), w_ukv.dtype)],
                          axis=2).reshape(MLA_KV_LORA, N_HEADS * V7X_LANES).astype(BF16)
    wvt = jnp.transpose(wkv_h[:, :, MLA_NOPE:], (1, 2, 0))
    wvt = jnp.concatenate([wvt, jnp.zeros((N_HEADS, ONES_ROWS, MLA_KV_LORA), w_ukv.dtype)],
                          axis=1).astype(BF16)
    gq = jnp.concatenate([q_norm, jnp.zeros((2 * V7X_LANES - MLA_Q_LORA,), q_norm.dtype)]).reshape(1, -1)
    gkv = kv_norm.reshape(1, -1)
    return gq, gkv, wq, wqr, wkn, wvt


def _rope_tables(s):
    half = MLA_ROPE // 2
    inv = ROPE_THETA ** (-jnp.arange(half, dtype=F32) / half)
    ang = jnp.arange(s).astype(F32)[:, None] * inv[None, :]
    cos, sin = jnp.cos(ang), jnp.sin(ang)
    ones = jnp.ones((s, MLA_NOPE), F32)
    z = lambda k: jnp.zeros((s, k), F32)
    pad = V7X_LANES - MLA_NOPE - MLA_ROPE
    cos_t = jnp.concatenate([ones, cos, cos, z(pad)], axis=1)
    sin_t = jnp.concatenate([z(MLA_NOPE), -sin, sin, z(pad)], axis=1)
    return cos_t, sin_t


_NT = (((1,), (1,)), ((), ()))


def _diag_ref_max(q, k, bias=None):
    groups = [slice(j * V7X_LANES, (j + 1) * V7X_LANES) for j in range(q.shape[0] // V7X_LANES)]
    scores = [lax.dot_general(q[rows], k[rows], _NT, preferred_element_type=F32) for rows in groups]
    if bias is not None:
        scores = [sc + bias for sc in scores]
    return jnp.concatenate([jnp.max(sc, axis=1, keepdims=True) for sc in scores], axis=0)


def _with_ref_column(q, m_col, ref_lane):
    lane = lax.broadcasted_iota(jnp.int32, q.shape, 1)
    return jnp.where(lane == ref_lane, -m_col, q.astype(F32)).astype(BF16)


def _without_ref_column(q_aug, ref_lane):
    lane = lax.broadcasted_iota(jnp.int32, q_aug.shape, 1)
    return jnp.where(lane == ref_lane, 0.0, q_aug.astype(F32)).astype(BF16)


def _not_finite(acc):
    return jnp.max(jnp.where(jnp.isfinite(acc), 0.0, 1.0)) > 0.0


def _online_block(k_blk, q, vt_blk, m, acc, bias=None):
    st = lax.dot_general(k_blk, q, _NT, preferred_element_type=F32)
    if bias is not None:
        st = st + bias
    m_new = jnp.maximum(m, jnp.max(st, axis=0, keepdims=True))
    p = jnp.exp2(st - m_new).astype(BF16)
    return m_new, acc * jnp.exp2(m - m_new) + jnp.dot(vt_blk, p, preferred_element_type=F32)


def _mla_flash_kernel(q_ref, k_ref, vt_ref, o_ref, *, bk, unroll, sub):
    q_aug = q_ref[0, 0]
    bq = q_aug.shape[0]
    s_len = k_ref.shape[2]

    def body(i, acc):
        n_sub = unroll * bk // sub

        def scores(j):
            k0 = pl.multiple_of(i * (unroll * bk) + j * sub, sub)
            return k0, lax.dot_general(k_ref[0, 0, pl.ds(k0, sub), :], q_aug, _NT, preferred_element_type=F32)

        nxt = scores(0)
        for j in range(n_sub):
            (k0, st), nxt = nxt, (scores(j + 1) if j + 1 < n_sub else None)
            acc = acc + jnp.dot(vt_ref[0, 0, :, pl.ds(k0, sub)], jnp.exp2(st).astype(BF16),
                                preferred_element_type=F32)
        return acc

    acc0 = jnp.zeros((VT_ROWS, bq), F32)
    acc = lax.fori_loop(0, s_len // (bk * unroll), body, acc0)
    o_ref[0] = (acc[:HEAD_DIM] / acc[HEAD_DIM:HEAD_DIM + 1]).astype(o_ref.dtype)

    @pl.when(_not_finite(acc))
    def _():
        q = _without_ref_column(q_aug, MLA_REF_LANE)

        def exact(i, carry):
            k0 = pl.multiple_of(i * bk, bk)
            return _online_block(k_ref[0, 0, pl.ds(k0, bk), :], q, vt_ref[0, 0, :, pl.ds(k0, bk)], *carry)

        _, acc_x = lax.fori_loop(0, s_len // bk, exact, (jnp.full((1, bq), NEG_BIG, F32), acc0))
        o_ref[0] = (acc_x[:HEAD_DIM] / acc_x[HEAD_DIM:HEAD_DIM + 1]).astype(o_ref.dtype)


def _mla_flash(q, k, vt, bq=512, bk=2048, unroll=4, sub=2048):
    b, h, s, dq = q.shape
    bq, bk = min(bq, s), min(bk, s)
    unroll = min(unroll, s // bk)
    return pl.pallas_call(
        functools.partial(_mla_flash_kernel, bk=bk, unroll=unroll, sub=min(sub, bk)),
        grid=(b, h, s // bq),
        in_specs=[pl.BlockSpec((1, 1, bq, dq), lambda bi, hi, qi: (bi, hi, qi, 0)),
                  pl.BlockSpec((1, 1, s, dq), lambda bi, hi, qi: (bi, hi, 0, 0)),
                  pl.BlockSpec((1, 1, VT_ROWS, s), lambda bi, hi, qi: (bi, hi, 0, 0))],
        out_specs=pl.BlockSpec((1, HEAD_DIM, bq), lambda bi, hi, qi: (bi, hi, qi)),
        out_shape=jax.ShapeDtypeStruct((b, h * HEAD_DIM, s), BF16),
        compiler_params=_params("parallel", "parallel", "arbitrary"),
        name="mla_flash",
    )(q, k, vt)


def _lane_selector(n_src, n_dst, src0, dst0, width, transposed=False):
    shape = (n_dst, n_src) if transposed else (n_src, n_dst)
    src = lax.broadcasted_iota(jnp.int32, shape, 1 if transposed else 0)
    dst = lax.broadcasted_iota(jnp.int32, shape, 0 if transposed else 1)
    hit = (src - src0 == dst - dst0) & (dst >= dst0) & (dst < dst0 + width)
    return jnp.where(hit, 1.0, 0.0).astype(BF16)


def _diff_prep_kernel(q_ref, k_ref, v_ref, kaug_ref, qa_out, k_out, vt_out, *, q_scale, slopes):
    q_tile, k_tile, v_tile = q_ref[...], k_ref[...], v_ref[...]
    lane = lax.broadcasted_iota(jnp.int32, (q_tile.shape[0], V7X_LANES), 1)
    diag = jnp.abs((lax.broadcasted_iota(jnp.int32, (V7X_LANES, V7X_LANES), 0)
                    - lax.broadcasted_iota(jnp.int32, (V7X_LANES, V7X_LANES), 1)).astype(F32))
    ones_rows = jnp.where(lax.broadcasted_iota(jnp.int32, (VT_ROWS, 1), 0) >= HEAD_DIM, 1.0, 0.0)
    def head_stages(h):
        c0 = h * HEAD_DIM
        k_h = jnp.dot(k_tile, _lane_selector(GROUP_WIDTH, V7X_LANES, c0, 0, HEAD_DIM), preferred_element_type=F32)
        q_parts = [jnp.dot(q_tile, _lane_selector(GROUP_WIDTH, V7X_LANES, c0 + part * DIFF_HALF,
                                                  part * DIFF_HALF, DIFF_HALF), preferred_element_type=F32)
                   for part in range(2)]
        vt = lax.dot_general(_lane_selector(GROUP_WIDTH, VT_ROWS, c0, 0, HEAD_DIM, transposed=True), v_tile, _NT,
                             preferred_element_type=F32)
        yield
        k_h = jnp.where(lane == DIFF_REF_LANE, 1.0, k_h).astype(BF16)
        k_out[0, h] = (k_h.astype(F32) + kaug_ref[h].astype(F32)).astype(BF16)
        vt_out[0, h] = (vt + ones_rows).astype(BF16)
        q_parts = [(q_p * q_scale).astype(BF16) for q_p in q_parts]
        yield
        refs = [_diag_ref_max(q_p, k_h, diag * slopes[h]) for q_p in q_parts]
        yield
        qa_out[0, h] = jnp.concatenate([_with_ref_column(q_p, m_col, DIFF_REF_LANE)
                                        for q_p, m_col in zip(q_parts, refs)], axis=1)

    _run_interleaved([head_stages(h) for h in range(N_HEADS)])


def _split3(x):
    def keep_top_bits(v):
        bits = lax.bitcast_convert_type(v, jnp.uint32) & jnp.uint32(0xFFFF0000)
        return lax.bitcast_convert_type(bits, F32)

    hi = keep_top_bits(x)
    r1 = x - hi
    mid = keep_top_bits(r1)
    lo = r1 - mid
    return hi.astype(BF16), mid.astype(BF16), lo.astype(BF16)


def _alibi_tables(s):
    c = jnp.asarray([LOG2E * 2.0 ** (-8.0 * (h + 1) / N_HEADS) for h in range(N_HEADS)], F32)
    terms = jnp.stack(_split3(c[:, None] * jnp.arange(s, dtype=F32)[None, :]), axis=-1)
    ones = jnp.ones_like(terms)
    left = jnp.zeros((N_HEADS, s, DIFF_POS_LANE), BF16)
    right = jnp.zeros((N_HEADS, s, V7X_LANES - DIFF_POS_LANE - 6), BF16)
    k_aug = jnp.concatenate([left, terms, ones, right], axis=-1)
    q_aug = jnp.concatenate([left, -ones, terms, right], axis=-1)
    return k_aug, q_aug


def _diff_prep(proj, k_aug, b, s, bm=512):
    bm = min(bm, s)
    nt = s // bm
    slopes = tuple(-LOG2E * 2.0 ** (-8.0 * (h + 1) / N_HEADS) for h in range(N_HEADS))
    spec = lambda unit: pl.BlockSpec((bm, GROUP_WIDTH), lambda bi, si: (bi * nt + si, unit // 2))
    return pl.pallas_call(
        functools.partial(_diff_prep_kernel, q_scale=LOG2E * DIFF_HALF ** -0.5, slopes=slopes),
        grid=(b, nt),
        in_specs=[spec(U_BQ), spec(U_BK), spec(U_BV),
                  pl.BlockSpec((N_HEADS, bm, V7X_LANES), lambda bi, si: (0, si, 0))],
        out_specs=[pl.BlockSpec((1, N_HEADS, bm, 2 * V7X_LANES), lambda bi, si: (bi, 0, si, 0)),
                   pl.BlockSpec((1, N_HEADS, bm, V7X_LANES), lambda bi, si: (bi, 0, si, 0)),
                   pl.BlockSpec((1, N_HEADS, VT_ROWS, bm), lambda bi, si: (bi, 0, 0, si))],
        out_shape=[jax.ShapeDtypeStruct((b, N_HEADS, s, 2 * V7X_LANES), BF16),
                   jax.ShapeDtypeStruct((b, N_HEADS, s, V7X_LANES), BF16),
                   jax.ShapeDtypeStruct((b, N_HEADS, VT_ROWS, s), BF16)],
        compiler_params=_params("parallel", "parallel"),
        name="diff_prep",
    )(proj, proj, proj, k_aug)


def _diff_flash_kernel(sc_ref, q_ref, qaug_ref, k_ref, vt_ref, rel_ref, g_ref, o_ref, *, bk, out_scale):
    hi = pl.program_id(1)
    qi = pl.program_id(2)
    slope = sc_ref[hi]
    lam = sc_ref[N_HEADS]
    q1a = q_ref[0, 0, :, :V7X_LANES]
    q2a = q_ref[0, 0, :, V7X_LANES:]
    bq = q1a.shape[0]
    s_len = k_ref.shape[2]
    n_blocks = s_len // bk
    rel = rel_ref[...]
    q0 = (qi * bq).astype(F32)
    diag_blk = (qi * bq) // bk

    def block_inputs(i):
        k0 = pl.multiple_of(i * bk, bk)
        bias = jnp.abs(rel + (k0.astype(F32) - q0)) * slope
        return k_ref[0, 0, pl.ds(k0, bk), :], vt_ref[0, 0, :, pl.ds(k0, bk)], bias

    def accumulate(a1, a2, k_blk, vt_blk, qx1, qx2, bias):
        s1 = lax.dot_general(k_blk, qx1, _NT, preferred_element_type=F32)
        s2 = lax.dot_general(k_blk, qx2, _NT, preferred_element_type=F32)
        if bias is not None:
            s1, s2 = s1 + bias, s2 + bias
        a1 = a1 + jnp.dot(vt_blk, jnp.exp2(s1).astype(BF16), preferred_element_type=F32)
        a2 = a2 + jnp.dot(vt_blk, jnp.exp2(s2).astype(BF16), preferred_element_type=F32)
        return a1, a2

    acc0 = jnp.zeros((VT_ROWS, bq), F32)
    k_diag, vt_diag, bias_diag = block_inputs(diag_blk)
    a1, a2 = accumulate(acc0, acc0, k_diag, vt_diag, q1a, q2a, bias_diag)
    pos = qaug_ref[0].astype(F32)
    after = [(q.astype(F32) + pos).astype(BF16) for q in (q1a, q2a)]
    before = [(q.astype(F32) - pos).astype(BF16) for q in (q1a, q2a)]
    for j in range(n_blocks - 1):
        i = jnp.where(j >= diag_blk, j + 1, j)
        k0 = pl.multiple_of(i * bk, bk)
        qx1, qx2 = [jnp.where(i < diag_blk, qb, qa) for qb, qa in zip(before, after)]
        a1, a2 = accumulate(a1, a2, k_ref[0, 0, pl.ds(k0, bk), :], vt_ref[0, 0, :, pl.ds(k0, bk)], qx1, qx2, None)

    def finish(a1, a2):
        o = a1[:HEAD_DIM] / a1[HEAD_DIM:HEAD_DIM + 1] - lam * (a2[:HEAD_DIM] / a2[HEAD_DIM:HEAD_DIM + 1])
        ms = jnp.sum(o * o, axis=0, keepdims=True) * (1.0 / HEAD_DIM)
        o_ref[0] = (o * lax.rsqrt(ms + EPS) * g_ref[...] * out_scale).astype(o_ref.dtype)

    finish(a1, a2)

    @pl.when(_not_finite(a1) | _not_finite(a2))
    def _():
        q1 = _without_ref_column(q1a, DIFF_REF_LANE)
        q2 = _without_ref_column(q2a, DIFF_REF_LANE)

        def exact(i, carry):
            m1, x1, m2, x2 = carry
            k_blk, vt_blk, bias = block_inputs(i)
            m1, x1 = _online_block(k_blk, q1, vt_blk, m1, x1, bias)
            m2, x2 = _online_block(k_blk, q2, vt_blk, m2, x2, bias)
            return m1, x1, m2, x2

        m0 = jnp.full((1, bq), NEG_BIG, F32)
        _, x1, _, x2 = lax.fori_loop(0, n_blocks, exact, (m0, acc0, m0, acc0))
        finish(x1, x2)


def _diff_flash(scalars, qa, q_aug, k, vt, subln, layer_idx, bq=512, bk=1024):
    b, h, s, d = k.shape
    bq, bk = min(bq, s), min(bk, s)
    assert bk % bq == 0, "a query block must sit inside one key block"
    lam_init = 0.8 - 0.6 * math.exp(-0.3 * layer_idx)
    rel = (np.arange(bk)[:, None] - np.arange(bq)[None, :]).astype(np.float32)
    return pl.pallas_call(
        functools.partial(_diff_flash_kernel, bk=bk, out_scale=1.0 - lam_init),
        grid=(b, h, s // bq),
        in_specs=[pl.BlockSpec(memory_space=pltpu.SMEM),
                  pl.BlockSpec((1, 1, bq, 2 * d), lambda bi, hi, qi: (bi, hi, qi, 0)),
                  pl.BlockSpec((1, bq, d), lambda bi, hi, qi: (hi, qi, 0)),
                  pl.BlockSpec((1, 1, s, d), lambda bi, hi, qi: (bi, hi, 0, 0)),
                  pl.BlockSpec((1, 1, VT_ROWS, s), lambda bi, hi, qi: (bi, hi, 0, 0)),
                  pl.BlockSpec((bk, bq), lambda bi, hi, qi: (0, 0)),
                  pl.BlockSpec((HEAD_DIM, 1), lambda bi, hi, qi: (0, 0))],
        out_specs=pl.BlockSpec((1, HEAD_DIM, bq), lambda bi, hi, qi: (bi, hi, qi)),
        out_shape=jax.ShapeDtypeStruct((b, h * HEAD_DIM, s), BF16),
        compiler_params=_params("parallel", "parallel", "arbitrary"),
        name="diff_flash",
    )(scalars, qa, q_aug, k, vt, jnp.asarray(rel), subln.reshape(HEAD_DIM, 1))


def _conv_silu_kernel(x_ref, prev_ref, next_ref, w_ref, v_ref, op_ref, qk_out, vt_out, opt_out, *, n_tiles):
    si = pl.program_id(1)
    x = x_ref[...]
    bm = x.shape[0]
    r = lax.broadcasted_iota(jnp.int32, (bm, bm), 0)
    c = lax.broadcasted_iota(jnp.int32, (bm, bm), 1)
    shift_dn = jnp.where(r == c + 1, 1.0, 0.0).astype(BF16)
    shift_up = jnp.where(r + 1 == c, 1.0, 0.0).astype(BF16)
    x_prev = jnp.dot(shift_dn, x, preferred_element_type=F32)
    x_next = jnp.dot(shift_up, x, preferred_element_type=F32)
    row = lax.broadcasted_iota(jnp.int32, x.shape, 0)
    halo_prev = jnp.where(si > 0, prev_ref[7:8, :].astype(F32), 0.0)
    halo_next = jnp.where(si < n_tiles - 1, next_ref[0:1, :].astype(F32), 0.0)
    x_prev = jnp.where(row == 0, halo_prev, x_prev)
    x_next = jnp.where(row == bm - 1, halo_next, x_next)
    w = w_ref[...]
    y = x_prev * w[0:1] + x.astype(F32) * w[1:2] + x_next * w[2:3]
    y = y * jax.nn.sigmoid(y)
    col = lax.broadcasted_iota(jnp.int32, x.shape, 1)
    qk_out[...] = jnp.where(col >= GROUP_WIDTH, y * (HEAD_DIM ** -0.5), y).astype(BF16)
    eye = _lane_selector(GROUP_WIDTH, GROUP_WIDTH, 0, 0, GROUP_WIDTH)
    vt_out[0] = lax.dot_general(eye, v_ref[...], _NT, preferred_element_type=F32).astype(BF16)
    opt_out[0] = lax.dot_general(eye, op_ref[...], _NT, preferred_element_type=F32).astype(BF16)


def _conv_silu(proj, b, s, conv_w, bm=256):
    bm = min(bm, s)
    nt = s // bm
    c = 2 * GROUP_WIDTH
    cb = U_CQK * V7X_LANES // c
    hb = bm // 8
    n8 = b * s // 8
    row = lambda bi, si: bi * nt + si
    cm_spec = pl.BlockSpec((1, GROUP_WIDTH, bm), lambda bi, si: (bi, 0, si))
    cm_shape = jax.ShapeDtypeStruct((b, GROUP_WIDTH, s), BF16)
    return pl.pallas_call(
        functools.partial(_conv_silu_kernel, n_tiles=nt),
        grid=(b, nt),
        in_specs=[pl.BlockSpec((bm, c), lambda bi, si: (row(bi, si), cb)),
                  pl.BlockSpec((8, c), lambda bi, si: (jnp.maximum(row(bi, si) * hb - 1, 0), cb)),
                  pl.BlockSpec((8, c), lambda bi, si: (jnp.minimum((row(bi, si) + 1) * hb, n8 - 1), cb)),
                  pl.BlockSpec((8, c), lambda bi, si: (0, 0)),
                  pl.BlockSpec((bm, GROUP_WIDTH), lambda bi, si: (row(bi, si), U_CV // 2)),
                  pl.BlockSpec((bm, GROUP_WIDTH), lambda bi, si: (row(bi, si), U_CO // 2))],
        out_specs=[pl.BlockSpec((bm, c), lambda bi, si: (row(bi, si), 0)), cm_spec, cm_spec],
        out_shape=[jax.ShapeDtypeStruct((b * s, c), BF16), cm_shape, cm_shape],
        compiler_params=_params("parallel", "parallel"),
        name="mlstm_conv",
    )(proj, proj, proj, jnp.concatenate([conv_w, jnp.zeros((5, c), conv_w.dtype)], axis=0), proj, proj)


def _log_sigmoid(x):
    return jnp.minimum(x, 0.0) - jnp.log(1.0 + jnp.exp(-jnp.abs(x)))


def _mlstm_chunk(qc, kc, vt1, gcol, grow, caug, m, *, backward):
    L = qc.shape[0]
    d = HEAD_DIM
    li_c, lf_c = gcol[0], _log_sigmoid(gcol[1])
    li_r, lf_r = grow[0], _log_sigmoid(grow[1])
    s_i = lax.broadcasted_iota(jnp.int32, (L, L), 0)
    j_i = lax.broadcasted_iota(jnp.int32, (L, L), 1)
    if backward:
        a_mask, b_mat, valid = j_i <= s_i, s_i >= j_i, s_i >= j_i
        bcum_row_idx, last_lane = L - 1, 0
    else:
        a_mask, b_mat, valid = j_i >= s_i, s_i <= j_i, s_i <= j_i
        bcum_row_idx, last_lane = 0, L - 1
    a = jnp.where(a_mask, lf_r, 0.0)
    a_hi = a.astype(BF16)
    a_lo = (a - a_hi.astype(F32)).astype(BF16)
    ones_b = jnp.where(b_mat, 1.0, 0.0).astype(BF16)
    yield
    e = jnp.dot(a_hi, ones_b, preferred_element_type=F32) + jnp.dot(a_lo, ones_b, preferred_element_type=F32)
    st = lax.dot_general(kc, qc, _NT, preferred_element_type=F32)
    cq = lax.dot_general(caug.astype(BF16), qc, _NT, preferred_element_type=F32)
    yield
    bcum_r = e[bcum_row_idx:bcum_row_idx + 1, :]
    dlog = jnp.where(valid, e + (li_c - lf_c), NEG_BIG)
    inter = bcum_r + m
    m_row = jnp.maximum(inter, jnp.max(dlog, axis=0, keepdims=True))
    w_intra = jnp.exp(dlog - m_row)
    w_inter = jnp.exp(inter - m_row)
    at = (st * w_intra).astype(BF16)
    b_last = bcum_r[:, last_lane:last_lane + 1]
    logw_end = b_last - bcum_r + li_r
    m_new = jnp.maximum(b_last + m, jnp.max(logw_end, axis=1, keepdims=True))
    w_end = jnp.exp(logw_end - m_new)
    decay = jnp.exp(b_last + m - m_new)
    vtw = (vt1.astype(F32) * w_end).astype(BF16)
    yield
    pv = jnp.dot(vt1, at, preferred_element_type=F32)
    u = jnp.dot(vtw, kc, preferred_element_type=F32)
    yield
    num = w_inter * cq[:d] + pv[:d]
    den = w_inter * cq[d:d + 1] + pv[d:d + 1]
    h = num / jnp.maximum(jnp.abs(den), jnp.exp(-m_row))
    caug = decay * caug + u[:d + 8]
    return h, caug, m_new


def _run_interleaved(stage_generators):
    results = [None] * len(stage_generators)
    live = list(range(len(stage_generators)))
    while live:
        for idx in list(live):
            try:
                next(stage_generators[idx])
            except StopIteration as done:
                results[idx] = done.value
                live.remove(idx)
    return results


def _mlstm_kernel(qk_ref, vt_ref, gc_ref, gb_ref, gr_ref, op_ref, ng_ref, o_ref, hf_sc, hb_sc, *, col_chunk):
    L = MLSTM_CHUNK
    d = HEAD_DIM
    s = qk_ref.shape[0]
    nc = s // L
    ones = jnp.ones((ONES_ROWS, L), BF16)
    lane = lax.broadcasted_iota(jnp.int32, (L, V7X_LANES), 1)

    def head_chunk(t0, h, gtile, gr, caug, m, backward):
        pair, sub = divmod(h, 2)
        mine = (lane >= sub * d) & (lane < (sub + 1) * d)
        q_pair = qk_ref[pl.ds(t0, L), pair * V7X_LANES:(pair + 1) * V7X_LANES]
        k_pair = qk_ref[pl.ds(t0, L), GROUP_WIDTH + pair * V7X_LANES:GROUP_WIDTH + (pair + 1) * V7X_LANES]
        qc = jnp.where(mine, q_pair, jnp.zeros_like(q_pair))
        kc = jnp.where(mine, k_pair, jnp.zeros_like(k_pair))
        vt1 = jnp.concatenate([vt_ref[0, h * d:(h + 1) * d, pl.ds(t0, L)], ones], axis=0)
        i_idx = (2 * N_HEADS if backward else 0) + h
        f_idx = i_idx + N_HEADS
        return _mlstm_chunk(qc, kc, vt1, (gtile[:, i_idx:i_idx + 1], gtile[:, f_idx:f_idx + 1]),
                            (gr[i_idx:i_idx + 1, :], gr[f_idx:f_idx + 1, :]), caug, m, backward=backward)

    def body(i, carry):
        chains, where = [], []
        for backward in (False, True):
            t0 = pl.multiple_of((nc - 1 - i if backward else i) * L, L)
            gtile = gc_ref[pl.ds(t0, L), :] + gb_ref[...]
            gr = gr_ref[0, :, pl.ds(t0, L)]
            for h in range(N_HEADS):
                caug, m = carry[(N_HEADS if backward else 0) + h]
                chains.append(head_chunk(t0, h, gtile, gr, caug, m, backward))
                where.append((hb_sc if backward else hf_sc, h, t0))
        new = []
        for (h_sc, h, t0), (hh, caug, m) in zip(where, _run_interleaved(chains)):
            h_sc[h * d:(h + 1) * d, pl.ds(t0, L)] = hh
            new.append((caug, m))
        return tuple(new)

    state0 = (jnp.zeros((d + 8, V7X_LANES), F32), jnp.zeros((1, 1), F32))
    lax.fori_loop(0, nc, body, (state0,) * (2 * N_HEADS))

    def finish(j, carry):
        c0 = pl.multiple_of(j * col_chunk, col_chunk)
        hsum = hf_sc[:, pl.ds(c0, col_chunk)] + hb_sc[:, pl.ds(c0, col_chunk)]
        normed = []
        for h in range(N_HEADS):
            x = hsum[h * d:(h + 1) * d]
            normed.append(x * lax.rsqrt(jnp.sum(x * x, axis=0, keepdims=True) * (1.0 / d) + EPS))
        hn = jnp.concatenate(normed, axis=0) * ng_ref[...]
        gate = jax.nn.sigmoid(op_ref[0, :, pl.ds(c0, col_chunk)].astype(F32))
        o_ref[0, :, pl.ds(c0, col_chunk)] = (hn * gate).astype(o_ref.dtype)
        return carry

    lax.fori_loop(0, s // col_chunk, finish, 0)


def _mlstm(qk, vt, gates, gate_bias, grow, opre_t, norm_g, b, s):
    c = GROUP_WIDTH
    once = pl.Buffered(1)
    return pl.pallas_call(
        functools.partial(_mlstm_kernel, col_chunk=min(1024, s)),
        grid=(b,),
        in_specs=[pl.BlockSpec((s, 2 * c), lambda bi: (bi, 0), pipeline_mode=once),
                  pl.BlockSpec((1, c, s), lambda bi: (bi, 0, 0), pipeline_mode=once),
                  pl.BlockSpec((s, V7X_LANES), lambda bi: (bi, 0), pipeline_mode=once),
                  pl.BlockSpec((1, V7X_LANES), lambda bi: (0, 0)),
                  pl.BlockSpec((1, 4 * N_HEADS, s), lambda bi: (bi, 0, 0), pipeline_mode=once),
                  pl.BlockSpec((1, c, s), lambda bi: (bi, 0, 0), pipeline_mode=once),
                  pl.BlockSpec((c, 1), lambda bi: (0, 0))],
        out_specs=pl.BlockSpec((1, c, s), lambda bi: (bi, 0, 0)),
        out_shape=jax.ShapeDtypeStruct((b, c, s), BF16),
        scratch_shapes=[pltpu.VMEM((c, s), F32), pltpu.VMEM((c, s), F32)],
        compiler_params=_params("parallel"),
        name="mlstm_scan",
    )(qk, vt, gates, gate_bias, grow, opre_t, norm_g.reshape(c, 1))


def _na_kernel(q_ref, k_ref, v_ref, bias_ref, o_ref, *, rows_per_step, n_rows, group):
    blk = pl.program_id(1)
    win = NA_ROWS * GRID_W
    lane = lax.broadcasted_iota(jnp.int32, (GRID_W, V7X_LANES), 1)
    def head_chain(q_pair, kw, vw, bias, sub):
        mine = (lane >= sub * HEAD_DIM) & (lane < (sub + 1) * HEAD_DIM)
        qh = jnp.where(mine, q_pair, jnp.zeros_like(q_pair))
        sc = lax.dot_general(qh, kw, _NT, preferred_element_type=F32)
        yield
        sc = sc * (HEAD_DIM ** -0.5) + bias
        mx = jnp.max(sc, axis=-1, keepdims=True)
        p = jnp.exp(sc - mx)
        p = (p / jnp.sum(p, axis=-1, keepdims=True)).astype(BF16)
        yield
        return jnp.dot(p, vw, preferred_element_type=F32)

    for i0 in range(0, rows_per_step, group):
        chains, where = [], []
        for i in range(i0, i0 + group):
            r = blk * rows_per_step + i
            r0 = jnp.clip(r - NA_ROWS // 2, 0, n_rows - NA_ROWS)
            dsel = r - r0
            k0 = pl.multiple_of(r0 * GRID_W, GRID_W)
            rows = slice(i * GRID_W, (i + 1) * GRID_W)
            for pair in range(N_HEADS // 2):
                lanes = slice(pair * V7X_LANES, (pair + 1) * V7X_LANES)
                q_pair = q_ref[rows, lanes]
                kw = k_ref[pl.ds(k0, win), lanes]
                vw = v_ref[pl.ds(k0, win), lanes]
                for sub in range(2):
                    chains.append(head_chain(q_pair, kw, vw, bias_ref[2 * pair + sub, dsel], sub))
                where.append((rows, lanes))
        outs = _run_interleaved(chains)
        for n, (rows, lanes) in enumerate(where):
            o_ref[rows, lanes] = jnp.where(lane < HEAD_DIM, outs[2 * n], outs[2 * n + 1]).astype(o_ref.dtype)


def _na(proj, b, s, bias, rows_per_step=8, group=4):
    n_rows = s // GRID_W
    rows_per_step = min(rows_per_step, n_rows)
    bm = rows_per_step * GRID_W
    nt = s // bm
    return pl.pallas_call(
        functools.partial(_na_kernel, rows_per_step=rows_per_step, n_rows=n_rows,
                          group=math.gcd(group, rows_per_step)),
        grid=(b, nt),
        in_specs=[pl.BlockSpec((bm, GROUP_WIDTH), lambda bi, ri: (bi * nt + ri, U_DQ // 2)),
                  pl.BlockSpec((s, GROUP_WIDTH), lambda bi, ri: (bi, U_DK // 2)),
                  pl.BlockSpec((s, GROUP_WIDTH), lambda bi, ri: (bi, U_DV // 2)),
                  pl.BlockSpec(bias.shape, lambda bi, ri: (0, 0, 0, 0))],
        out_specs=pl.BlockSpec((bm, GROUP_WIDTH), lambda bi, ri: (bi * nt + ri, 0)),
        out_shape=jax.ShapeDtypeStruct((b * s, GROUP_WIDTH), BF16),
        compiler_params=_params("parallel", "arbitrary"),
        name="na_attn",
    )(proj, proj, proj, bias)


def _na_bias_table(rpb, n_rows):
    wr = min(NA_ROWS, n_rows)
    cols = np.arange(GRID_W)
    col_start = np.clip(cols - NA_COLS // 2, 0, GRID_W - NA_COLS)
    ck = np.arange(GRID_W)[None, :]
    valid = (ck >= col_start[:, None]) & (ck < col_start[:, None] + NA_COLS)
    crel = ck - cols[:, None] + NA_COLS - 1
    rrel = np.arange(wr)[None, :] - np.arange(wr)[:, None] + NA_ROWS - 1
    c_sel = ((crel[None] == np.arange(2 * NA_COLS - 1)[:, None, None]) & valid[None]).astype(np.float32)
    r_sel = (rrel[None] == np.arange(2 * NA_ROWS - 1)[:, None, None]).astype(np.float32)
    t = jnp.einsum('hab,adw,bqk->hdqwk', rpb.astype(F32), r_sel, c_sel, precision=lax.Precision.HIGHEST)
    t = t + np.where(valid, 0.0, NEG_BIG).astype(np.float32)[None, None, :, None, :]
    return t.reshape(rpb.shape[0], wr, GRID_W, wr * GRID_W)


def _outproj_kernel(ya_ref, yb_ref, yc_ref, yd_ref, x_ref, w_ref, o_ref):
    acc = x_ref[...] + jnp.dot(yd_ref[...], w_ref[3], preferred_element_type=F32)
    for g, y_ref in enumerate((ya_ref, yb_ref, yc_ref)):
        acc = acc + lax.dot_general(y_ref[0], w_ref[g], (((0,), (0,)), ((), ())),
                                    preferred_element_type=F32)
    o_ref[...] = acc


def _outproj(ya, yb, yc, yd, x, w, b, s, bm=512):
    bm = min(bm, s)
    nt = s // bm
    d = x.shape[1]
    yspec = pl.BlockSpec((1, GROUP_WIDTH, bm), lambda bi, si: (bi, 0, si))
    return pl.pallas_call(
        _outproj_kernel,
        grid=(b, nt),
        in_specs=[yspec, yspec, yspec,
                  pl.BlockSpec((bm, GROUP_WIDTH), lambda bi, si: (bi * nt + si, 0)),
                  pl.BlockSpec((bm, d), lambda bi, si: (bi * nt + si, 0)),
                  pl.BlockSpec(w.shape, lambda bi, si: (0, 0, 0))],
        out_specs=pl.BlockSpec((bm, d), lambda bi, si: (bi * nt + si, 0)),
        out_shape=jax.ShapeDtypeStruct(x.shape, F32),
        compiler_params=_params("parallel", "parallel"),
        name="outproj",
    )(ya, yb, yc, yd, x, w)


FFN_ROW_CHUNKS = 2


def _swiglu_accumulate(hn_sc, acc_sc, w_gate, w_up, w_down):
    rows = hn_sc.shape[0] // FFN_ROW_CHUNKS

    def chunk(c):
        sl = slice(c * rows, (c + 1) * rows)
        hn = hn_sc[sl, :]
        gate = jnp.dot(hn, w_gate, preferred_element_type=F32)
        up = jnp.dot(hn, w_up, preferred_element_type=F32)
        yield
        act = (gate * jax.nn.sigmoid(gate) * up).astype(BF16)
        yield
        acc_sc[sl, :] += jnp.dot(act, w_down, preferred_element_type=F32)

    _run_interleaved([chunk(c) for c in range(FFN_ROW_CHUNKS)])


def _ffn_kernel(x_ref, g_ref, wg_ref, wu_ref, wd_ref, o_ref, hn_sc, acc_sc):
    f = pl.program_id(1)

    @pl.when(f == 0)
    def _():
        x = x_ref[...]
        hn_sc[...] = (_rms(x, x.shape[-1]) * g_ref[...]).astype(BF16)
        acc_sc[...] = jnp.zeros_like(acc_sc)

    _swiglu_accumulate(hn_sc, acc_sc, wg_ref[...], wu_ref[...], wd_ref[...])

    @pl.when(f == pl.num_programs(1) - 1)
    def _():
        o_ref[...] = x_ref[...] + acc_sc[...]


def _ffn(x, g, wg, wu, wd, bm=512):
    n, d = x.shape
    ff = wg.shape[1]
    bm = min(bm, n)
    bf = ff
    once = pl.Buffered(1)
    return pl.pallas_call(
        _ffn_kernel,
        grid=(n // bm, ff // bf),
        in_specs=[pl.BlockSpec((bm, d), lambda i, f: (i, 0)),
                  pl.BlockSpec((1, d), lambda i, f: (0, 0)),
                  pl.BlockSpec((d, bf), lambda i, f: (0, f), pipeline_mode=once),
                  pl.BlockSpec((d, bf), lambda i, f: (0, f), pipeline_mode=once),
                  pl.BlockSpec((bf, d), lambda i, f: (f, 0), pipeline_mode=once)],
        out_specs=pl.BlockSpec((bm, d), lambda i, f: (i, 0)),
        out_shape=jax.ShapeDtypeStruct(x.shape, F32),
        scratch_shapes=[pltpu.VMEM((bm, d), BF16), pltpu.VMEM((bm, d), F32)],
        compiler_params=_params("parallel", "arbitrary"),
        name="ffn",
    )(x, g.reshape(1, d), wg, wu, wd)


R_E1, R_E2, R_W1, R_W2, R_RANK1, R_RANK2 = range(6)


def _lane_pack(lane, cols):
    out = jnp.zeros(lane.shape, F32)
    for idx, col in cols:
        out = out + jnp.where(lane == idx, col, 0.0)
    return out


def _router_kernel(x_ref, g_ref, wr_ref, hn_ref, route_ref, counts_ref, carry_sc):
    @pl.when(pl.program_id(0) == 0)
    def _():
        carry_sc[...] = jnp.zeros_like(carry_sc)

    x = x_ref[...]
    bm = x.shape[0]
    hn = _rms(x, x.shape[-1]) * g_ref[...]
    hn_ref[...] = hn
    logits = jnp.dot(hn, wr_ref[...], preferred_element_type=F32, precision=lax.Precision.HIGHEST)
    lane = lax.broadcasted_iota(jnp.int32, logits.shape, 1)
    logits = jnp.where(lane < N_EXPERTS, logits, NEG_BIG)
    m1 = jnp.max(logits, axis=-1, keepdims=True)
    i1 = jnp.min(jnp.where(logits == m1, lane, V7X_LANES), axis=-1, keepdims=True)
    rest = jnp.where(lane == i1, NEG_BIG, logits)
    m2 = jnp.max(rest, axis=-1, keepdims=True)
    i2 = jnp.min(jnp.where(rest == m2, lane, V7X_LANES), axis=-1, keepdims=True)
    e2 = jnp.exp(m2 - m1)
    w1 = 1.0 / (1.0 + e2)
    w2 = e2 / (1.0 + e2)
    hot1 = jnp.where(lane == i1, 1.0, 0.0)
    hot2 = jnp.where(lane == i2, 1.0, 0.0)
    r_i = lax.broadcasted_iota(jnp.int32, (bm, bm), 0)
    c_i = lax.broadcasted_iota(jnp.int32, (bm, bm), 1)
    below = jnp.where(c_i < r_i, 1.0, 0.0).astype(BF16)
    before1 = jnp.dot(below, hot1.astype(BF16), preferred_element_type=F32)
    before2 = jnp.dot(below, hot2.astype(BF16), preferred_element_type=F32)
    cnt1 = jnp.sum(hot1, axis=0, keepdims=True)
    cnt2 = jnp.sum(hot2, axis=0, keepdims=True)
    carry = carry_sc[...]
    rank1 = jnp.sum(hot1 * (before1 + carry), axis=-1, keepdims=True)
    rank2 = jnp.sum(hot2 * (before2 + carry + cnt1), axis=-1, keepdims=True)
    carry = carry + cnt1 + cnt2
    carry_sc[...] = carry
    counts_ref[...] = carry
    route_ref[...] = _lane_pack(lane, ((R_E1, i1.astype(F32)), (R_E2, i2.astype(F32)), (R_W1, w1), (R_W2, w2),
                                       (R_RANK1, rank1), (R_RANK2, rank2)))


def _router(x, g, w_router, bm=512):
    n, d = x.shape
    bm = min(bm, n)
    wr = jnp.concatenate([w_router, jnp.zeros((d, V7X_LANES - N_EXPERTS), w_router.dtype)], axis=1)
    return pl.pallas_call(
        _router_kernel,
        grid=(n // bm,),
        in_specs=[pl.BlockSpec((bm, d), lambda i: (i, 0)),
                  pl.BlockSpec((1, d), lambda i: (0, 0)),
                  pl.BlockSpec((d, V7X_LANES), lambda i: (0, 0))],
        out_specs=[pl.BlockSpec((bm, d), lambda i: (i, 0)),
                   pl.BlockSpec((bm, V7X_LANES), lambda i: (i, 0)),
                   pl.BlockSpec((1, V7X_LANES), lambda i: (0, 0))],
        out_shape=[jax.ShapeDtypeStruct((n, d), F32),
                   jax.ShapeDtypeStruct((n, V7X_LANES), F32),
                   jax.ShapeDtypeStruct((1, V7X_LANES), F32)],
        scratch_shapes=[pltpu.VMEM((1, V7X_LANES), F32)],
        compiler_params=_params("arbitrary"),
        name="moe_router",
    )(x, g.reshape(1, d), wr)


def _route_plan(route, counts, tm):
    n = route.shape[0]
    counts = counts[0, :N_EXPERTS].astype(jnp.int32)
    padded = ((counts + tm - 1) // tm) * tm
    g_end = jnp.cumsum(padded)
    g_start = (g_end - padded).astype(F32)
    experts = jnp.arange(N_EXPERTS, dtype=F32)[None, :]
    start1 = jnp.sum(jnp.where(route[:, R_E1:R_E1 + 1] == experts, g_start[None, :], 0.0), axis=1)
    start2 = jnp.sum(jnp.where(route[:, R_E2:R_E2 + 1] == experts, g_start[None, :], 0.0), axis=1)
    pos = jnp.stack([start1 + route[:, R_RANK1], start2 + route[:, R_RANK2]], axis=1).astype(jnp.int32)
    n_tiles = 2 * n // tm + N_EXPERTS
    tile_start = jnp.arange(n_tiles, dtype=jnp.int32) * tm
    tile_expert = jnp.minimum(jnp.sum(tile_start[:, None] >= g_end[None, :], axis=1), N_EXPERTS - 1)
    n_used = (g_end[-1:] // tm).astype(jnp.int32)
    return pos.reshape(-1), tile_expert.astype(jnp.int32), n_used, n_tiles


def _row_copies(pos_ref, r, src_at, dst_at, sem):
    copies = []
    for c in range(2):
        p = pos_ref[2 * r + c]
        copies.append(pltpu.make_async_copy(src_at(r, c, p), dst_at(r, c, p), sem))
    return copies


def _move_rows(pos_ref, n_rows, src_at, dst_at, sem):
    def start(r, carry):
        for cp in _row_copies(pos_ref, r, src_at, dst_at, sem):
            cp.start()
        return carry

    def wait(r, carry):
        for cp in _row_copies(pos_ref, r, src_at, dst_at, sem):
            cp.wait()
        return carry

    lax.fori_loop(0, n_rows, start, 0, unroll=8)
    lax.fori_loop(0, n_rows, wait, 0, unroll=8)


def _dispatch_kernel(pos_ref, hn_ref, xg_in_ref, xg_ref, sem):
    del xg_in_ref
    _move_rows(pos_ref, hn_ref.shape[0],
               lambda r, c, p: hn_ref.at[pl.ds(r, 1)],
               lambda r, c, p: xg_ref.at[pl.ds(p, 1)], sem)


def _dispatch(pos, hn, n_rows, bm=512):
    n, d = hn.shape
    bm = min(bm, n)
    return pl.pallas_call(
        _dispatch_kernel,
        grid=(n // bm,),
        in_specs=[pl.BlockSpec((2 * bm,), lambda i: (i,), memory_space=pltpu.SMEM),
                  pl.BlockSpec((bm, d), lambda i: (i, 0)),
                  pl.BlockSpec(memory_space=pl.ANY)],
        out_specs=pl.BlockSpec(memory_space=pl.ANY),
        out_shape=jax.ShapeDtypeStruct((n_rows, d), F32),
        scratch_shapes=[pltpu.SemaphoreType.DMA(())],
        input_output_aliases={2: 0},
        compiler_params=_params("arbitrary"),
        name="moe_dispatch",
    )(pos, hn, jnp.zeros((n_rows, d), F32))


def _expert_ffn_kernel(te_ref, nu_ref, xg_ref, wg_ref, wu_ref, wd_ref, o_ref, hn_sc, acc_sc):
    del te_ref
    t = pl.program_id(0)
    f = pl.program_id(1)
    last = pl.num_programs(1) - 1
    used = t < nu_ref[0]

    @pl.when(used & (f == 0))
    def _():
        hn_sc[...] = xg_ref[...].astype(BF16)
        acc_sc[...] = jnp.zeros_like(acc_sc)

    @pl.when(used)
    def _():
        _swiglu_accumulate(hn_sc, acc_sc, wg_ref[0], wu_ref[0], wd_ref[0])

    @pl.when(used & (f == last))
    def _():
        o_ref[...] = acc_sc[...]

    @pl.when(jnp.logical_not(used) & (f == last))
    def _():
        o_ref[...] = jnp.zeros_like(o_ref)


def _expert_ffn(tile_expert, n_used, xg, wg, wu, wd, tm, bf=7 * V7X_MXU_COLS):
    rows, d = xg.shape
    ff = wg.shape[2]
    bf = math.gcd(bf, ff)
    grid_spec = pltpu.PrefetchScalarGridSpec(
        num_scalar_prefetch=2,
        grid=(rows // tm, ff // bf),
        in_specs=[pl.BlockSpec((tm, d), lambda t, f, te, nu: (t, 0)),
                  pl.BlockSpec((1, d, bf), lambda t, f, te, nu: (te[t], 0, f)),
                  pl.BlockSpec((1, d, bf), lambda t, f, te, nu: (te[t], 0, f)),
                  pl.BlockSpec((1, bf, d), lambda t, f, te, nu: (te[t], f, 0))],
        out_specs=pl.BlockSpec((tm, d), lambda t, f, te, nu: (t, 0)),
        scratch_shapes=[pltpu.VMEM((tm, d), BF16), pltpu.VMEM((tm, d), F32)])
    return pl.pallas_call(
        _expert_ffn_kernel,
        grid_spec=grid_spec,
        out_shape=jax.ShapeDtypeStruct((rows, d), F32),
        compiler_params=_params("arbitrary", "arbitrary"),
        name="moe_ffn",
    )(tile_expert, n_used, xg, wg, wu, wd)


def _combine_kernel(pos_ref, x_ref, route_ref, g_ref, yg_ref, o_ref, buf, sem, *, final_norm):
    _move_rows(pos_ref, x_ref.shape[0],
               lambda r, c, p: yg_ref.at[pl.ds(p, 1)],
               lambda r, c, p: buf.at[c, pl.ds(r, 1)], sem)
    route = route_ref[...]
    y = x_ref[...] + route[:, R_W1:R_W1 + 1] * buf[0] + route[:, R_W2:R_W2 + 1] * buf[1]
    if final_norm:
        y = _rms(y, y.shape[-1]) * g_ref[...]
    o_ref[...] = y


def _combine(pos, x, route, yg, g_final, final_norm, row0, n_rows, bm=512):
    d = x.shape[1]
    bm = math.gcd(bm, math.gcd(row0, n_rows)) if row0 else min(bm, n_rows)
    off = row0 // bm
    return pl.pallas_call(
        functools.partial(_combine_kernel, final_norm=final_norm),
        grid=(n_rows // bm,),
        in_specs=[pl.BlockSpec((2 * bm,), lambda i: (i + off,), memory_space=pltpu.SMEM),
                  pl.BlockSpec((bm, d), lambda i: (i + off, 0)),
                  pl.BlockSpec((bm, V7X_LANES), lambda i: (i + off, 0)),
                  pl.BlockSpec((1, d), lambda i: (0, 0)),
                  pl.BlockSpec(memory_space=pl.ANY)],
        out_specs=pl.BlockSpec((bm, d), lambda i: (i, 0)),
        out_shape=jax.ShapeDtypeStruct((n_rows, d), F32),
        scratch_shapes=[pltpu.VMEM((2, bm, d), F32), pltpu.SemaphoreType.DMA(())],
        compiler_params=_params("arbitrary"),
        name="moe_combine",
    )(pos, x, route, g_final.reshape(1, d), yg)


def _moe(x, g, w_router, wg, wu, wd, g_final, final_norm, row_splits, tm=512):
    hn, route, counts = _router(x, g, w_router)
    pos, tile_expert, n_used, n_tiles = _route_plan(route, counts, tm)
    xg = _dispatch(pos, hn, n_tiles * tm)
    yg = _expert_ffn(tile_expert, n_used, xg, wg, wu, wd, tm)
    return [_combine(pos, x, route, yg, g_final, final_norm, row0, n_rows) for row0, n_rows in row_splits]


def _final_norm_kernel(x_ref, g_ref, o_ref):
    x = x_ref[...]
    o_ref[...] = _rms(x, x.shape[-1]) * g_ref[...]


def _final_norm(x, g, bm=1024):
    n, d = x.shape
    bm = min(bm, n)
    return pl.pallas_call(
        _final_norm_kernel,
        grid=(n // bm,),
        in_specs=[pl.BlockSpec((bm, d), lambda i: (i, 0)), pl.BlockSpec((1, d), lambda i: (0, 0))],
        out_specs=pl.BlockSpec((bm, d), lambda i: (i, 0)),
        out_shape=jax.ShapeDtypeStruct(x.shape, F32),
        compiler_params=_params("parallel"),
        name="final_norm",
    )(x, g.reshape(1, d))


def _token_mix(x, i, b, s, p):
    w_main, w_gates = _prep_w_in(p['w_in'][i])
    proj, gates = _inproj(x, p['norm_mix'][i], w_main, w_gates)

    cos_t, sin_t = _rope_tables(s)
    q_a, k_a, vt_a = _mla_prep(proj, b, s, cos_t, sin_t, *_prep_mla_weights(
        p['mla_q_norm'][i], p['mla_kv_norm'][i], p['mla_w_uq'][i], p['mla_w_ukv'][i]))
    y_a = _mla_flash(q_a, k_a, vt_a)

    lp = p['diff_lambda'][i].astype(F32)
    lam_init = 0.8 - 0.6 * math.exp(-0.3 * i)
    lam = jnp.exp(jnp.sum(lp[0] * lp[1])) - jnp.exp(jnp.sum(lp[2] * lp[3])) + lam_init
    slopes = 2.0 ** (-8.0 * jnp.arange(1, N_HEADS + 1, dtype=F32) / N_HEADS)
    scalars = jnp.concatenate([-slopes * LOG2E, lam[None]]).astype(F32)
    k_aug, q_aug = _alibi_tables(s)
    qa_b, k_b, vt_b = _diff_prep(proj, k_aug, b, s)
    y_b = _diff_flash(scalars, qa_b, q_aug, k_b, vt_b, p['diff_subln'][i], i)

    qk_c, vt_c, opt_c = _conv_silu(proj, b, s, p['mlstm_conv'][i])
    n_gates = 4 * N_HEADS
    gate_bias = jnp.concatenate([p['mlstm_gate_bias'][i], jnp.zeros((V7X_LANES - n_gates,), F32)])[None, :]
    grow = (gates[:, :n_gates] + gate_bias[:, :n_gates]).reshape(b, s, n_gates).transpose(0, 2, 1)
    y_c = _mlstm(qk_c, vt_c, gates, gate_bias, grow, opt_c, p['mlstm_norm'][i], b, s)

    y_d = _na(proj, b, s, _na_bias_table(p['na_rpb'][i], s // GRID_W))

    w_out = p['w_out'][i].reshape(4, GROUP_WIDTH, -1).astype(BF16)
    return _outproj(y_a, y_b, y_c, y_d, x, w_out, b, s)


def _trunk(x, p, depth, batch_splits):
    b, s, d = x.shape
    x = x.reshape(b * s, d)
    starts = np.cumsum((0,) + tuple(batch_splits))[:-1]
    row_splits = [(int(b0) * s, int(nb) * s) for b0, nb in zip(starts, batch_splits)]
    for i in range(depth):
        x = _token_mix(x, i, b, s, p)
        j = i // 2
        last = i == depth - 1
        if i % 2 == 0:
            x = _ffn(x, p['norm_ffn'][i], p['ffn_w_gate'][j].astype(BF16), p['ffn_w_up'][j].astype(BF16),
                     p['ffn_w_down'][j].astype(BF16))
            if last:
                x = _final_norm(x, p['norm_final'])
                outs = [x[r0:r0 + nr] for r0, nr in row_splits]
        else:
            outs = _moe(x, p['norm_ffn'][i], p['moe_router'][j], p['moe_w_gate'][j].astype(BF16),
                        p['moe_w_up'][j].astype(BF16), p['moe_w_down'][j].astype(BF16), p['norm_final'],
                        final_norm=last, row_splits=row_splits if last else [(0, b * s)])
            x = outs[0]
    return [o.reshape(nb, s, d) for o, nb in zip(outs, batch_splits)]


def kernel(x_prompt, x_sample, norm_mix, norm_ffn, w_in, w_out, mla_q_norm, mla_kv_norm, mla_w_uq, mla_w_ukv,
           diff_lambda, diff_subln, mlstm_conv, mlstm_gate_bias, mlstm_norm, na_rpb, ffn_w_gate, ffn_w_up,
           ffn_w_down, moe_router, moe_w_gate, moe_w_up, moe_w_down, norm_final):
    p = dict(norm_mix=norm_mix, norm_ffn=norm_ffn, w_in=w_in, w_out=w_out, mla_q_norm=mla_q_norm,
             mla_kv_norm=mla_kv_norm, mla_w_uq=mla_w_uq, mla_w_ukv=mla_w_ukv, diff_lambda=diff_lambda,
             diff_subln=diff_subln, mlstm_conv=mlstm_conv, mlstm_gate_bias=mlstm_gate_bias,
             mlstm_norm=mlstm_norm, na_rpb=na_rpb, ffn_w_gate=ffn_w_gate, ffn_w_up=ffn_w_up,
             ffn_w_down=ffn_w_down, moe_router=moe_router, moe_w_gate=moe_w_gate, moe_w_up=moe_w_up,
             moe_w_down=moe_w_down, norm_final=norm_final)
    depth = norm_mix.shape[0]
    y_prompt, y_sample = _trunk(jnp.concatenate([x_prompt, x_sample], axis=0), p, depth,
                                (x_prompt.shape[0], x_sample.shape[0]))
    return (y_prompt, y_sample)
```

```python
import functools
import math

import numpy as np
import jax
import jax.numpy as jnp
from jax import lax
from jax.experimental import pallas as pl
from jax.experimental.pallas import tpu as pltpu

F32 = jnp.float32
BF16 = jnp.bfloat16

V7X_LANES = 128
V7X_MXU_COLS = 256
V7X_VMEM_LIMIT_BYTES = 56 * 1024 * 1024

EPS = 1e-6
LOG2E = 1.4426950408889634
NEG_BIG = -1e30

HEAD_DIM = 64
N_HEADS = 4
GROUP_WIDTH = 256
MLA_Q_LORA = 192
MLA_KV_LORA = 128
MLA_NOPE = 64
MLA_ROPE = 32
ROPE_THETA = 10000.0
DIFF_HALF = 32
MLSTM_CHUNK = 128
GRID_W = 64
NA_ROWS = 8
NA_COLS = 16
N_EXPERTS = 8
ONES_ROWS = 16
VT_ROWS = HEAD_DIM + ONES_ROWS
MLA_REF_LANE = MLA_NOPE + MLA_ROPE
DIFF_REF_LANE = HEAD_DIM
DIFF_POS_LANE = DIFF_REF_LANE + 1

IN_SECTIONS = (192, 128, 32, 256, 256, 256, 512, 256, 256, 16, 256, 256, 256)
U_BQ, U_BK, U_BV, U_DQ, U_DK, U_DV, U_CQK, U_CV, U_CO, U_ACQ, U_ACKV, U_AKR, U_AKRR = (
    0, 2, 4, 6, 8, 10, 12, 16, 18, 20, 22, 23, 24)
PROJ_COLS = 25 * V7X_LANES


def _params(*sem):
    return pltpu.CompilerParams(dimension_semantics=sem, vmem_limit_bytes=V7X_VMEM_LIMIT_BYTES)


def _rms(x, n):
    return x * lax.rsqrt(jnp.sum(x * x, axis=-1, keepdims=True) * (1.0 / n) + EPS)


def _inproj_kernel(x_ref, g_ref, w_ref, wg_ref, o_ref, og_ref, *, col_chunk):
    x = x_ref[...]
    hn = (_rms(x, x.shape[-1]) * g_ref[...]).astype(BF16)
    for c in range(0, o_ref.shape[1], col_chunk):
        w = min(col_chunk, o_ref.shape[1] - c)
        o_ref[:, c:c + w] = jnp.dot(hn, w_ref[:, c:c + w], preferred_element_type=F32).astype(BF16)
    og_ref[...] = jnp.dot(hn, wg_ref[...], preferred_element_type=F32)


def _inproj(x, g, w_main, w_gates, bm=512):
    n, d = x.shape
    bm = min(bm, n)
    return pl.pallas_call(
        functools.partial(_inproj_kernel, col_chunk=4 * V7X_MXU_COLS),
        grid=(n // bm,),
        in_specs=[pl.BlockSpec((bm, d), lambda i: (i, 0)),
                  pl.BlockSpec((1, d), lambda i: (0, 0)),
                  pl.BlockSpec(w_main.shape, lambda i: (0, 0)),
                  pl.BlockSpec(w_gates.shape, lambda i: (0, 0))],
        out_specs=[pl.BlockSpec((bm, PROJ_COLS), lambda i: (i, 0)),
                   pl.BlockSpec((bm, V7X_LANES), lambda i: (i, 0))],
        out_shape=[jax.ShapeDtypeStruct((n, PROJ_COLS), BF16),
                   jax.ShapeDtypeStruct((n, V7X_LANES), F32)],
        compiler_params=_params("parallel"),
        name="inproj",
    )(x, g.reshape(1, d), w_main, w_gates)


def _prep_w_in(w):
    d = w.shape[0]
    offs = np.cumsum((0,) + IN_SECTIONS)
    (a_cq, a_ckv, a_kr, b_q, b_k, b_v, c_qk, c_v, c_o, c_g, d_q, d_k, d_v) = [
        w[:, offs[i]:offs[i + 1]] for i in range(len(IN_SECTIONS))]
    z = lambda k: jnp.zeros((d, k), w.dtype)
    half = MLA_ROPE // 2
    a_kr_rot = jnp.concatenate([a_kr[:, half:], a_kr[:, :half]], axis=1)
    main = jnp.concatenate([b_q, b_k, b_v, d_q, d_k, d_v, c_qk, c_v, c_o,
                            a_cq, z(64), a_ckv,
                            z(64), a_kr, z(32),
                            z(64), a_kr_rot, z(32)], axis=1)
    gates = jnp.concatenate([c_g, z(V7X_LANES - 16)], axis=1)
    return main.astype(BF16), gates.astype(BF16)


def _mla_prep_kernel(cq_ref, ckv_ref, kr_ref, krr_ref, cos_ref, sin_ref, gq_ref, gkv_ref,
                     wq_ref, wqr_ref, wkn_ref, wvt_ref, ones_ref, q_out, k_out, vt_out, *, q_scale):
    cos = cos_ref[...]
    sin = sin_ref[...]
    qn = (_rms(cq_ref[...].astype(F32), MLA_Q_LORA) * gq_ref[...]).astype(BF16)
    qa = jnp.dot(qn, wq_ref[...], preferred_element_type=F32)
    qr = jnp.dot(qn, wqr_ref[...], preferred_element_type=F32)
    kvn = (_rms(ckv_ref[...].astype(F32), MLA_KV_LORA) * gkv_ref[...]).astype(BF16)
    kn = jnp.dot(kvn, wkn_ref[...], preferred_element_type=F32)
    k_rope = kr_ref[...].astype(F32) * cos + krr_ref[...].astype(F32) * sin
    k_rope = jnp.where(lax.broadcasted_iota(jnp.int32, k_rope.shape, 1) == MLA_REF_LANE, 1.0, k_rope)
    def head_stages(h):
        sl = slice(h * V7X_LANES, (h + 1) * V7X_LANES)
        q_h = ((qa[:, sl] * cos + qr[:, sl] * sin) * q_scale).astype(BF16)
        k_h = (kn[:, sl] + k_rope).astype(BF16)
        k_out[0, h] = k_h
        vt = lax.dot_general(wvt_ref[h], kvn, _NT, preferred_element_type=F32)
        yield
        vt_out[0, h] = (vt + ones_ref[...]).astype(BF16)
        m_col = _diag_ref_max(q_h, k_h)
        yield
        q_out[0, h] = _with_ref_column(q_h, m_col, MLA_REF_LANE)

    _run_interleaved([head_stages(h) for h in range(N_HEADS)])


def _mla_prep(proj, b, s, cos_t, sin_t, gq, gkv, wq, wqr, wkn, wvt, bm=512):
    bm = min(bm, s)
    nt = s // bm
    row = lambda bi, si: bi * nt + si
    ones_col = jnp.concatenate([jnp.zeros((HEAD_DIM, 1), F32), jnp.ones((ONES_ROWS, 1), F32)], axis=0)
    full = lambda a: pl.BlockSpec(a.shape, lambda bi, si: (0,) * a.ndim)
    return pl.pallas_call(
        functools.partial(_mla_prep_kernel, q_scale=LOG2E * (MLA_NOPE + MLA_ROPE) ** -0.5),
        grid=(b, nt),
        in_specs=[pl.BlockSpec((bm, 2 * V7X_LANES), lambda bi, si: (row(bi, si), U_ACQ // 2)),
                  pl.BlockSpec((bm, V7X_LANES), lambda bi, si: (row(bi, si), U_ACKV)),
                  pl.BlockSpec((bm, V7X_LANES), lambda bi, si: (row(bi, si), U_AKR)),
                  pl.BlockSpec((bm, V7X_LANES), lambda bi, si: (row(bi, si), U_AKRR)),
                  pl.BlockSpec((bm, V7X_LANES), lambda bi, si: (si, 0)),
                  pl.BlockSpec((bm, V7X_LANES), lambda bi, si: (si, 0)),
                  full(gq), full(gkv), full(wq), full(wqr), full(wkn), full(wvt), full(ones_col)],
        out_specs=[pl.BlockSpec((1, N_HEADS, bm, V7X_LANES), lambda bi, si: (bi, 0, si, 0)),
                   pl.BlockSpec((1, N_HEADS, bm, V7X_LANES), lambda bi, si: (bi, 0, si, 0)),
                   pl.BlockSpec((1, N_HEADS, VT_ROWS, bm), lambda bi, si: (bi, 0, 0, si))],
        out_shape=[jax.ShapeDtypeStruct((b, N_HEADS, s, V7X_LANES), BF16),
                   jax.ShapeDtypeStruct((b, N_HEADS, s, V7X_LANES), BF16),
                   jax.ShapeDtypeStruct((b, N_HEADS, VT_ROWS, s), BF16)],
        compiler_params=_params("parallel", "parallel"),
        name="mla_prep",
    )(proj, proj, proj, proj, cos_t, sin_t, gq, gkv, wq, wqr, wkn, wvt, ones_col)


def _prep_mla_weights(q_norm, kv_norm, w_uq, w_ukv):
    half = MLA_ROPE // 2
    dq = MLA_NOPE + MLA_ROPE
    wq_h = w_uq.reshape(MLA_Q_LORA, N_HEADS, dq)
    zq = lambda k: jnp.zeros((MLA_Q_LORA, N_HEADS, k), w_uq.dtype)
    rope_cols = wq_h[:, :, MLA_NOPE:]
    rope_rot = jnp.concatenate([rope_cols[:, :, half:], rope_cols[:, :, :half]], axis=2)
    wq = jnp.concatenate([wq_h, zq(V7X_LANES - dq)], axis=2).reshape(MLA_Q_LORA, N_HEADS * V7X_LANES)
    wqr = jnp.concatenate([zq(MLA_NOPE), rope_rot, zq(V7X_LANES - dq)], axis=2).reshape(
        MLA_Q_LORA, N_HEADS * V7X_LANES)
    pad_rows = jnp.zeros((2 * V7X_LANES - MLA_Q_LORA, N_HEADS * V7X_LANES), w_uq.dtype)
    wq = jnp.concatenate([wq, pad_rows], axis=0).astype(BF16)
    wqr = jnp.concatenate([wqr, pad_rows], axis=0).astype(BF16)
    wkv_h = w_ukv.reshape(MLA_KV_LORA, N_HEADS, MLA_NOPE + HEAD_DIM)
    wkn = jnp.concatenate([wkv_h[:, :, :MLA_NOPE],
                           jnp.zeros((MLA_KV_LORA, N_HEADS, V7X_LANES - MLA_NOPE), w_ukv.dtype)],
                          axis=2).reshape(MLA_KV_LORA, N_HEADS * V7X_LANES).astype(BF16)
    wvt = jnp.transpose(wkv_h[:, :, MLA_NOPE:], (1, 2, 0))
    wvt = jnp.concatenate([wvt, jnp.zeros((N_HEADS, ONES_ROWS, MLA_KV_LORA), w_ukv.dtype)],
                          axis=1).astype(BF16)
    gq = jnp.concatenate([q_norm, jnp.zeros((2 * V7X_LANES - MLA_Q_LORA,), q_norm.dtype)]).reshape(1, -1)
    gkv = kv_norm.reshape(1, -1)
    return gq, gkv, wq, wqr, wkn, wvt


def _rope_tables(s):
    half = MLA_ROPE // 2
    inv = ROPE_THETA ** (-jnp.arange(half, dtype=F32) / half)
    ang = jnp.arange(s).astype(F32)[:, None] * inv[None, :]
    cos, sin = jnp.cos(ang), jnp.sin(ang)
    ones = jnp.ones((s, MLA_NOPE), F32)
    z = lambda k: jnp.zeros((s, k), F32)
    pad = V7X_LANES - MLA_NOPE - MLA_ROPE
    cos_t = jnp.concatenate([ones, cos, cos, z(pad)], axis=1)
    sin_t = jnp.concatenate([z(MLA_NOPE), -sin, sin, z(pad)], axis=1)
    return cos_t, sin_t


_NT = (((1,), (1,)), ((), ()))


def _diag_ref_max(q, k, bias=None):
    groups = [slice(j * V7X_LANES, (j + 1) * V7X_LANES) for j in range(q.shape[0] // V7X_LANES)]
    scores = [lax.dot_general(q[rows], k[rows], _NT, preferred_element_type=F32) for rows in groups]
    if bias is not None:
        scores = [sc + bias for sc in scores]
    return jnp.concatenate([jnp.max(sc, axis=1, keepdims=True) for sc in scores], axis=0)


def _with_ref_column(q, m_col, ref_lane):
    lane = lax.broadcasted_iota(jnp.int32, q.shape, 1)
    return jnp.where(lane == ref_lane, -m_col, q.astype(F32)).astype(BF16)


def _without_ref_column(q_aug, ref_lane):
    lane = lax.broadcasted_iota(jnp.int32, q_aug.shape, 1)
    return jnp.where(lane == ref_lane, 0.0, q_aug.astype(F32)).astype(BF16)


def _not_finite(acc):
    return jnp.max(jnp.where(jnp.isfinite(acc), 0.0, 1.0)) > 0.0


def _online_block(k_blk, q, vt_blk, m, acc, bias=None):
    st = lax.dot_general(k_blk, q, _NT, preferred_element_type=F32)
    if bias is not None:
        st = st + bias
    m_new = jnp.maximum(m, jnp.max(st, axis=0, keepdims=True))
    p = jnp.exp2(st - m_new).astype(BF16)
    return m_new, acc * jnp.exp2(m - m_new) + jnp.dot(vt_blk, p, preferred_element_type=F32)


def _mla_flash_kernel(q_ref, k_ref, vt_ref, o_ref, *, bk, unroll, sub):
    q_aug = q_ref[0, 0]
    bq = q_aug.shape[0]
    s_len = k_ref.shape[2]

    def body(i, acc):
        n_sub = unroll * bk // sub

        def scores(j):
            k0 = pl.multiple_of(i * (unroll * bk) + j * sub, sub)
            return k0, lax.dot_general(k_ref[0, 0, pl.ds(k0, sub), :], q_aug, _NT, preferred_element_type=F32)

        nxt = scores(0)
        for j in range(n_sub):
            (k0, st), nxt = nxt, (scores(j + 1) if j + 1 < n_sub else None)
            acc = acc + jnp.dot(vt_ref[0, 0, :, pl.ds(k0, sub)], jnp.exp2(st).astype(BF16),
                                preferred_element_type=F32)
        return acc

    acc0 = jnp.zeros((VT_ROWS, bq), F32)
    acc = lax.fori_loop(0, s_len // (bk * unroll), body, acc0)
    o_ref[0] = (acc[:HEAD_DIM] / acc[HEAD_DIM:HEAD_DIM + 1]).astype(o_ref.dtype)

    @pl.when(_not_finite(acc))
    def _():
        q = _without_ref_column(q_aug, MLA_REF_LANE)

        def exact(i, carry):
            k0 = pl.multiple_of(i * bk, bk)
            return _online_block(k_ref[0, 0, pl.ds(k0, bk), :], q, vt_ref[0, 0, :, pl.ds(k0, bk)], *carry)

        _, acc_x = lax.fori_loop(0, s_len // bk, exact, (jnp.full((1, bq), NEG_BIG, F32), acc0))
        o_ref[0] = (acc_x[:HEAD_DIM] / acc_x[HEAD_DIM:HEAD_DIM + 1]).astype(o_ref.dtype)


def _mla_flash(q, k, vt, bq=1024, bk=2048, unroll=4, sub=2048):
    b, h, s, dq = q.shape
    bq, bk = min(bq, s), min(bk, s)
    unroll = min(unroll, s // bk)
    return pl.pallas_call(
        functools.partial(_mla_flash_kernel, bk=bk, unroll=unroll, sub=min(sub, bk)),
        grid=(b, h, s // bq),
        in_specs=[pl.BlockSpec((1, 1, bq, dq), lambda bi, hi, qi: (bi, hi, qi, 0)),
                  pl.BlockSpec((1, 1, s, dq), lambda bi, hi, qi: (bi, hi, 0, 0)),
                  pl.BlockSpec((1, 1, VT_ROWS, s), lambda bi, hi, qi: (bi, hi, 0, 0))],
        out_specs=pl.BlockSpec((1, HEAD_DIM, bq), lambda bi, hi, qi: (bi, hi, qi)),
        out_shape=jax.ShapeDtypeStruct((b, h * HEAD_DIM, s), BF16),
        compiler_params=_params("parallel", "parallel", "arbitrary"),
        name="mla_flash",
    )(q, k, vt)


def _lane_selector(n_src, n_dst, src0, dst0, width, transposed=False):
    shape = (n_dst, n_src) if transposed else (n_src, n_dst)
    src = lax.broadcasted_iota(jnp.int32, shape, 1 if transposed else 0)
    dst = lax.broadcasted_iota(jnp.int32, shape, 0 if transposed else 1)
    hit = (src - src0 == dst - dst0) & (dst >= dst0) & (dst < dst0 + width)
    return jnp.where(hit, 1.0, 0.0).astype(BF16)


def _diff_prep_kernel(q_ref, k_ref, v_ref, kaug_ref, qa_out, k_out, vt_out, *, q_scale, slopes):
    q_tile, k_tile, v_tile = q_ref[...], k_ref[...], v_ref[...]
    lane = lax.broadcasted_iota(jnp.int32, (q_tile.shape[0], V7X_LANES), 1)
    diag = jnp.abs((lax.broadcasted_iota(jnp.int32, (V7X_LANES, V7X_LANES), 0)
                    - lax.broadcasted_iota(jnp.int32, (V7X_LANES, V7X_LANES), 1)).astype(F32))
    ones_rows = jnp.where(lax.broadcasted_iota(jnp.int32, (VT_ROWS, 1), 0) >= HEAD_DIM, 1.0, 0.0)
    def head_stages(h):
        c0 = h * HEAD_DIM
        k_h = jnp.dot(k_tile, _lane_selector(GROUP_WIDTH, V7X_LANES, c0, 0, HEAD_DIM), preferred_element_type=F32)
        q_parts = [jnp.dot(q_tile, _lane_selector(GROUP_WIDTH, V7X_LANES, c0 + part * DIFF_HALF,
                                                  part * DIFF_HALF, DIFF_HALF), preferred_element_type=F32)
                   for part in range(2)]
        vt = lax.dot_general(_lane_selector(GROUP_WIDTH, VT_ROWS, c0, 0, HEAD_DIM, transposed=True), v_tile, _NT,
                             preferred_element_type=F32)
        yield
        k_h = jnp.where(lane == DIFF_REF_LANE, 1.0, k_h).astype(BF16)
        k_out[0, h] = (k_h.astype(F32) + kaug_ref[h].astype(F32)).astype(BF16)
        vt_out[0, h] = (vt + ones_rows).astype(BF16)
        q_parts = [(q_p * q_scale).astype(BF16) for q_p in q_parts]
        yield
        refs = [_diag_ref_max(q_p, k_h, diag * slopes[h]) for q_p in q_parts]
        yield
        qa_out[0, h] = jnp.concatenate([_with_ref_column(q_p, m_col, DIFF_REF_LANE)
                                        for q_p, m_col in zip(q_parts, refs)], axis=1)

    _run_interleaved([head_stages(h) for h in range(N_HEADS)])


def _split3(x):
    def keep_top_bits(v):
        bits = lax.bitcast_convert_type(v, jnp.uint32) & jnp.uint32(0xFFFF0000)
        return lax.bitcast_convert_type(bits, F32)

    hi = keep_top_bits(x)
    r1 = x - hi
    mid = keep_top_bits(r1)
    lo = r1 - mid
    return hi.astype(BF16), mid.astype(BF16), lo.astype(BF16)


def _alibi_tables(s):
    c = jnp.asarray([LOG2E * 2.0 ** (-8.0 * (h + 1) / N_HEADS) for h in range(N_HEADS)], F32)
    terms = jnp.stack(_split3(c[:, None] * jnp.arange(s, dtype=F32)[None, :]), axis=-1)
    ones = jnp.ones_like(terms)
    left = jnp.zeros((N_HEADS, s, DIFF_POS_LANE), BF16)
    right = jnp.zeros((N_HEADS, s, V7X_LANES - DIFF_POS_LANE - 6), BF16)
    k_aug = jnp.concatenate([left, terms, ones, right], axis=-1)
    q_aug = jnp.concatenate([left, -ones, terms, right], axis=-1)
    return k_aug, q_aug


def _diff_prep(proj, k_aug, b, s, bm=512):
    bm = min(bm, s)
    nt = s // bm
    slopes = tuple(-LOG2E * 2.0 ** (-8.0 * (h + 1) / N_HEADS) for h in range(N_HEADS))
    spec = lambda unit: pl.BlockSpec((bm, GROUP_WIDTH), lambda bi, si: (bi * nt + si, unit // 2))
    return pl.pallas_call(
        functools.partial(_diff_prep_kernel, q_scale=LOG2E * DIFF_HALF ** -0.5, slopes=slopes),
        grid=(b, nt),
        in_specs=[spec(U_BQ), spec(U_BK), spec(U_BV),
                  pl.BlockSpec((N_HEADS, bm, V7X_LANES), lambda bi, si: (0, si, 0))],
        out_specs=[pl.BlockSpec((1, N_HEADS, bm, 2 * V7X_LANES), lambda bi, si: (bi, 0, si, 0)),
                   pl.BlockSpec((1, N_HEADS, bm, V7X_LANES), lambda bi, si: (bi, 0, si, 0)),
                   pl.BlockSpec((1, N_HEADS, VT_ROWS, bm), lambda bi, si: (bi, 0, 0, si))],
        out_shape=[jax.ShapeDtypeStruct((b, N_HEADS, s, 2 * V7X_LANES), BF16),
                   jax.ShapeDtypeStruct((b, N_HEADS, s, V7X_LANES), BF16),
                   jax.ShapeDtypeStruct((b, N_HEADS, VT_ROWS, s), BF16)],
        compiler_params=_params("parallel", "parallel"),
        name="diff_prep",
    )(proj, proj, proj, k_aug)


def _diff_flash_kernel(sc_ref, q_ref, qaug_ref, k_ref, vt_ref, rel_ref, g_ref, o_ref, *, bk, out_scale):
    hi = pl.program_id(1)
    qi = pl.program_id(2)
    slope = sc_ref[hi]
    lam = sc_ref[N_HEADS]
    q1a = q_ref[0, 0, :, :V7X_LANES]
    q2a = q_ref[0, 0, :, V7X_LANES:]
    bq = q1a.shape[0]
    s_len = k_ref.shape[2]
    n_blocks = s_len // bk
    rel = rel_ref[...]
    q0 = (qi * bq).astype(F32)
    diag_blk = (qi * bq) // bk

    def block_inputs(i):
        k0 = pl.multiple_of(i * bk, bk)
        bias = jnp.abs(rel + (k0.astype(F32) - q0)) * slope
        return k_ref[0, 0, pl.ds(k0, bk), :], vt_ref[0, 0, :, pl.ds(k0, bk)], bias

    def accumulate(a1, a2, k_blk, vt_blk, qx1, qx2, bias):
        s1 = lax.dot_general(k_blk, qx1, _NT, preferred_element_type=F32)
        s2 = lax.dot_general(k_blk, qx2, _NT, preferred_element_type=F32)
        if bias is not None:
            s1, s2 = s1 + bias, s2 + bias
        a1 = a1 + jnp.dot(vt_blk, jnp.exp2(s1).astype(BF16), preferred_element_type=F32)
        a2 = a2 + jnp.dot(vt_blk, jnp.exp2(s2).astype(BF16), preferred_element_type=F32)
        return a1, a2

    acc0 = jnp.zeros((VT_ROWS, bq), F32)
    k_diag, vt_diag, bias_diag = block_inputs(diag_blk)
    a1, a2 = accumulate(acc0, acc0, k_diag, vt_diag, q1a, q2a, bias_diag)
    pos = qaug_ref[0].astype(F32)
    after = [(q.astype(F32) + pos).astype(BF16) for q in (q1a, q2a)]
    before = [(q.astype(F32) - pos).astype(BF16) for q in (q1a, q2a)]
    for j in range(n_blocks - 1):
        i = jnp.where(j >= diag_blk, j + 1, j)
        k0 = pl.multiple_of(i * bk, bk)
        qx1, qx2 = [jnp.where(i < diag_blk, qb, qa) for qb, qa in zip(before, after)]
        a1, a2 = accumulate(a1, a2, k_ref[0, 0, pl.ds(k0, bk), :], vt_ref[0, 0, :, pl.ds(k0, bk)], qx1, qx2, None)

    def finish(a1, a2):
        o = a1[:HEAD_DIM] / a1[HEAD_DIM:HEAD_DIM + 1] - lam * (a2[:HEAD_DIM] / a2[HEAD_DIM:HEAD_DIM + 1])
        ms = jnp.sum(o * o, axis=0, keepdims=True) * (1.0 / HEAD_DIM)
        o_ref[0] = (o * lax.rsqrt(ms + EPS) * g_ref[...] * out_scale).astype(o_ref.dtype)

    finish(a1, a2)

    @pl.when(_not_finite(a1) | _not_finite(a2))
    def _():
        q1 = _without_ref_column(q1a, DIFF_REF_LANE)
        q2 = _without_ref_column(q2a, DIFF_REF_LANE)

        def exact(i, carry):
            m1, x1, m2, x2 = carry
            k_blk, vt_blk, bias = block_inputs(i)
            m1, x1 = _online_block(k_blk, q1, vt_blk, m1, x1, bias)
            m2, x2 = _online_block(k_blk, q2, vt_blk, m2, x2, bias)
            return m1, x1, m2, x2

        m0 = jnp.full((1, bq), NEG_BIG, F32)
        _, x1, _, x2 = lax.fori_loop(0, n_blocks, exact, (m0, acc0, m0, acc0))
        finish(x1, x2)


def _diff_flash(scalars, qa, q_aug, k, vt, subln, layer_idx, bq=512, bk=1024):
    b, h, s, d = k.shape
    bq, bk = min(bq, s), min(bk, s)
    assert bk % bq == 0, "a query block must sit inside one key block"
    lam_init = 0.8 - 0.6 * math.exp(-0.3 * layer_idx)
    rel = (np.arange(bk)[:, None] - np.arange(bq)[None, :]).astype(np.float32)
    return pl.pallas_call(
        functools.partial(_diff_flash_kernel, bk=bk, out_scale=1.0 - lam_init),
        grid=(b, h, s // bq),
        in_specs=[pl.BlockSpec(memory_space=pltpu.SMEM),
                  pl.BlockSpec((1, 1, bq, 2 * d), lambda bi, hi, qi: (bi, hi, qi, 0)),
                  pl.BlockSpec((1, bq, d), lambda bi, hi, qi: (hi, qi, 0)),
                  pl.BlockSpec((1, 1, s, d), lambda bi, hi, qi: (bi, hi, 0, 0)),
                  pl.BlockSpec((1, 1, VT_ROWS, s), lambda bi, hi, qi: (bi, hi, 0, 0)),
                  pl.BlockSpec((bk, bq), lambda bi, hi, qi: (0, 0)),
                  pl.BlockSpec((HEAD_DIM, 1), lambda bi, hi, qi: (0, 0))],
        out_specs=pl.BlockSpec((1, HEAD_DIM, bq), lambda bi, hi, qi: (bi, hi, qi)),
        out_shape=jax.ShapeDtypeStruct((b, h * HEAD_DIM, s), BF16),
        compiler_params=_params("parallel", "parallel", "arbitrary"),
        name="diff_flash",
    )(scalars, qa, q_aug, k, vt, jnp.asarray(rel), subln.reshape(HEAD_DIM, 1))


def _conv_silu_kernel(x_ref, prev_ref, next_ref, w_ref, v_ref, op_ref, qk_out, vt_out, opt_out, *, n_tiles):
    si = pl.program_id(1)
    x = x_ref[...]
    bm = x.shape[0]
    r = lax.broadcasted_iota(jnp.int32, (bm, bm), 0)
    c = lax.broadcasted_iota(jnp.int32, (bm, bm), 1)
    shift_dn = jnp.where(r == c + 1, 1.0, 0.0).astype(BF16)
    shift_up = jnp.where(r + 1 == c, 1.0, 0.0).astype(BF16)
    x_prev = jnp.dot(shift_dn, x, preferred_element_type=F32)
    x_next = jnp.dot(shift_up, x, preferred_element_type=F32)
    row = lax.broadcasted_iota(jnp.int32, x.shape, 0)
    halo_prev = jnp.where(si > 0, prev_ref[7:8, :].astype(F32), 0.0)
    halo_next = jnp.where(si < n_tiles - 1, next_ref[0:1, :].astype(F32), 0.0)
    x_prev = jnp.where(row == 0, halo_prev, x_prev)
    x_next = jnp.where(row == bm - 1, halo_next, x_next)
    w = w_ref[...]
    y = x_prev * w[0:1] + x.astype(F32) * w[1:2] + x_next * w[2:3]
    y = y * jax.nn.sigmoid(y)
    col = lax.broadcasted_iota(jnp.int32, x.shape, 1)
    qk_out[...] = jnp.where(col >= GROUP_WIDTH, y * (HEAD_DIM ** -0.5), y).astype(BF16)
    eye = _lane_selector(GROUP_WIDTH, GROUP_WIDTH, 0, 0, GROUP_WIDTH)
    vt_out[0] = lax.dot_general(eye, v_ref[...], _NT, preferred_element_type=F32).astype(BF16)
    opt_out[0] = lax.dot_general(eye, op_ref[...], _NT, preferred_element_type=F32).astype(BF16)


def _conv_silu(proj, b, s, conv_w, bm=256):
    bm = min(bm, s)
    nt = s // bm
    c = 2 * GROUP_WIDTH
    cb = U_CQK * V7X_LANES // c
    hb = bm // 8
    n8 = b * s // 8
    row = lambda bi, si: bi * nt + si
    cm_spec = pl.BlockSpec((1, GROUP_WIDTH, bm), lambda bi, si: (bi, 0, si))
    cm_shape = jax.ShapeDtypeStruct((b, GROUP_WIDTH, s), BF16)
    return pl.pallas_call(
        functools.partial(_conv_silu_kernel, n_tiles=nt),
        grid=(b, nt),
        in_specs=[pl.BlockSpec((bm, c), lambda bi, si: (row(bi, si), cb)),
                  pl.BlockSpec((8, c), lambda bi, si: (jnp.maximum(row(bi, si) * hb - 1, 0), cb)),
                  pl.BlockSpec((8, c), lambda bi, si: (jnp.minimum((row(bi, si) + 1) * hb, n8 - 1), cb)),
                  pl.BlockSpec((8, c), lambda bi, si: (0, 0)),
                  pl.BlockSpec((bm, GROUP_WIDTH), lambda bi, si: (row(bi, si), U_CV // 2)),
                  pl.BlockSpec((bm, GROUP_WIDTH), lambda bi, si: (row(bi, si), U_CO // 2))],
        out_specs=[pl.BlockSpec((bm, c), lambda bi, si: (row(bi, si), 0)), cm_spec, cm_spec],
        out_shape=[jax.ShapeDtypeStruct((b * s, c), BF16), cm_shape, cm_shape],
        compiler_params=_params("parallel", "parallel"),
        name="mlstm_conv",
    )(proj, proj, proj, jnp.concatenate([conv_w, jnp.zeros((5, c), conv_w.dtype)], axis=0), proj, proj)


def _log_sigmoid(x):
    return jnp.minimum(x, 0.0) - jnp.log(1.0 + jnp.exp(-jnp.abs(x)))


def _mlstm_chunk(qc, kc, vt1, gcol, grow, caug, m, *, backward):
    L = qc.shape[0]
    d = HEAD_DIM
    li_c, lf_c = gcol[0], _log_sigmoid(gcol[1])
    li_r, lf_r = grow[0], _log_sigmoid(grow[1])
    s_i = lax.broadcasted_iota(jnp.int32, (L, L), 0)
    j_i = lax.broadcasted_iota(jnp.int32, (L, L), 1)
    if backward:
        a_mask, b_mat, valid = j_i <= s_i, s_i >= j_i, s_i >= j_i
        bcum_row_idx, last_lane = L - 1, 0
    else:
        a_mask, b_mat, valid = j_i >= s_i, s_i <= j_i, s_i <= j_i
        bcum_row_idx, last_lane = 0, L - 1
    a = jnp.where(a_mask, lf_r, 0.0)
    a_hi = a.astype(BF16)
    a_lo = (a - a_hi.astype(F32)).astype(BF16)
    ones_b = jnp.where(b_mat, 1.0, 0.0).astype(BF16)
    yield
    e = jnp.dot(a_hi, ones_b, preferred_element_type=F32) + jnp.dot(a_lo, ones_b, preferred_element_type=F32)
    st = lax.dot_general(kc, qc, _NT, preferred_element_type=F32)
    cq = lax.dot_general(caug.astype(BF16), qc, _NT, preferred_element_type=F32)
    yield
    bcum_r = e[bcum_row_idx:bcum_row_idx + 1, :]
    dlog = jnp.where(valid, e + (li_c - lf_c), NEG_BIG)
    inter = bcum_r + m
    m_row = jnp.maximum(inter, jnp.max(dlog, axis=0, keepdims=True))
    w_intra = jnp.exp(dlog - m_row)
    w_inter = jnp.exp(inter - m_row)
    at = (st * w_intra).astype(BF16)
    b_last = bcum_r[:, last_lane:last_lane + 1]
    logw_end = b_last - bcum_r + li_r
    m_new = jnp.maximum(b_last + m, jnp.max(logw_end, axis=1, keepdims=True))
    w_end = jnp.exp(logw_end - m_new)
    decay = jnp.exp(b_last + m - m_new)
    vtw = (vt1.astype(F32) * w_end).astype(BF16)
    yield
    pv = jnp.dot(vt1, at, preferred_element_type=F32)
    u = jnp.dot(vtw, kc, preferred_element_type=F32)
    yield
    num = w_inter * cq[:d] + pv[:d]
    den = w_inter * cq[d:d + 1] + pv[d:d + 1]
    h = num / jnp.maximum(jnp.abs(den), jnp.exp(-m_row))
    caug = decay * caug + u[:d + 8]
    return h, caug, m_new


def _run_interleaved(stage_generators):
    results = [None] * len(stage_generators)
    live = list(range(len(stage_generators)))
    while live:
        for idx in list(live):
            try:
                next(stage_generators[idx])
            except StopIteration as done:
                results[idx] = done.value
                live.remove(idx)
    return results


def _mlstm_kernel(qk_ref, vt_ref, gc_ref, gb_ref, gr_ref, op_ref, ng_ref, o_ref, hf_sc, hb_sc, *, col_chunk):
    L = MLSTM_CHUNK
    d = HEAD_DIM
    s = qk_ref.shape[0]
    nc = s // L
    ones = jnp.ones((ONES_ROWS, L), BF16)
    lane = lax.broadcasted_iota(jnp.int32, (L, V7X_LANES), 1)

    def head_chunk(t0, h, gtile, gr, caug, m, backward):
        pair, sub = divmod(h, 2)
        mine = (lane >= sub * d) & (lane < (sub + 1) * d)
        q_pair = qk_ref[pl.ds(t0, L), pair * V7X_LANES:(pair + 1) * V7X_LANES]
        k_pair = qk_ref[pl.ds(t0, L), GROUP_WIDTH + pair * V7X_LANES:GROUP_WIDTH + (pair + 1) * V7X_LANES]
        qc = jnp.where(mine, q_pair, jnp.zeros_like(q_pair))
        kc = jnp.where(mine, k_pair, jnp.zeros_like(k_pair))
        vt1 = jnp.concatenate([vt_ref[0, h * d:(h + 1) * d, pl.ds(t0, L)], ones], axis=0)
        i_idx = (2 * N_HEADS if backward else 0) + h
        f_idx = i_idx + N_HEADS
        return _mlstm_chunk(qc, kc, vt1, (gtile[:, i_idx:i_idx + 1], gtile[:, f_idx:f_idx + 1]),
                            (gr[i_idx:i_idx + 1, :], gr[f_idx:f_idx + 1, :]), caug, m, backward=backward)

    def body(i, carry):
        chains, where = [], []
        for backward in (False, True):
            t0 = pl.multiple_of((nc - 1 - i if backward else i) * L, L)
            gtile = gc_ref[pl.ds(t0, L), :] + gb_ref[...]
            gr = gr_ref[0, :, pl.ds(t0, L)]
            for h in range(N_HEADS):
                caug, m = carry[(N_HEADS if backward else 0) + h]
                chains.append(head_chunk(t0, h, gtile, gr, caug, m, backward))
                where.append((hb_sc if backward else hf_sc, h, t0))
        new = []
        for (h_sc, h, t0), (hh, caug, m) in zip(where, _run_interleaved(chains)):
            h_sc[h * d:(h + 1) * d, pl.ds(t0, L)] = hh
            new.append((caug, m))
        return tuple(new)

    state0 = (jnp.zeros((d + 8, V7X_LANES), F32), jnp.zeros((1, 1), F32))
    lax.fori_loop(0, nc, body, (state0,) * (2 * N_HEADS))

    def finish(j, carry):
        c0 = pl.multiple_of(j * col_chunk, col_chunk)
        hsum = hf_sc[:, pl.ds(c0, col_chunk)] + hb_sc[:, pl.ds(c0, col_chunk)]
        normed = []
        for h in range(N_HEADS):
            x = hsum[h * d:(h + 1) * d]
            normed.append(x * lax.rsqrt(jnp.sum(x * x, axis=0, keepdims=True) * (1.0 / d) + EPS))
        hn = jnp.concatenate(normed, axis=0) * ng_ref[...]
        gate = jax.nn.sigmoid(op_ref[0, :, pl.ds(c0, col_chunk)].astype(F32))
        o_ref[0, :, pl.ds(c0, col_chunk)] = (hn * gate).astype(o_ref.dtype)
        return carry

    lax.fori_loop(0, s // col_chunk, finish, 0)


def _mlstm(qk, vt, gates, gate_bias, grow, opre_t, norm_g, b, s):
    c = GROUP_WIDTH
    once = pl.Buffered(1)
    return pl.pallas_call(
        functools.partial(_mlstm_kernel, col_chunk=min(1024, s)),
        grid=(b,),
        in_specs=[pl.BlockSpec((s, 2 * c), lambda bi: (bi, 0), pipeline_mode=once),
                  pl.BlockSpec((1, c, s), lambda bi: (bi, 0, 0), pipeline_mode=once),
                  pl.BlockSpec((s, V7X_LANES), lambda bi: (bi, 0), pipeline_mode=once),
                  pl.BlockSpec((1, V7X_LANES), lambda bi: (0, 0)),
                  pl.BlockSpec((1, 4 * N_HEADS, s), lambda bi: (bi, 0, 0), pipeline_mode=once),
                  pl.BlockSpec((1, c, s), lambda bi: (bi, 0, 0), pipeline_mode=once),
                  pl.BlockSpec((c, 1), lambda bi: (0, 0))],
        out_specs=pl.BlockSpec((1, c, s), lambda bi: (bi, 0, 0)),
        out_shape=jax.ShapeDtypeStruct((b, c, s), BF16),
        scratch_shapes=[pltpu.VMEM((c, s), F32), pltpu.VMEM((c, s), F32)],
        compiler_params=_params("parallel"),
        name="mlstm_scan",
    )(qk, vt, gates, gate_bias, grow, opre_t, norm_g.reshape(c, 1))


def _na_kernel(q_ref, k_ref, v_ref, bias_ref, o_ref, *, rows_per_step, n_rows, group):
    blk = pl.program_id(1)
    win = NA_ROWS * GRID_W
    lane = lax.broadcasted_iota(jnp.int32, (GRID_W, V7X_LANES), 1)
    def head_chain(q_pair, kw, vw, bias, sub):
        mine = (lane >= sub * HEAD_DIM) & (lane < (sub + 1) * HEAD_DIM)
        qh = jnp.where(mine, q_pair, jnp.zeros_like(q_pair))
        sc = lax.dot_general(qh, kw, _NT, preferred_element_type=F32)
        yield
        sc = sc * (HEAD_DIM ** -0.5) + bias
        mx = jnp.max(sc, axis=-1, keepdims=True)
        p = jnp.exp(sc - mx)
        p = (p / jnp.sum(p, axis=-1, keepdims=True)).astype(BF16)
        yield
        return jnp.dot(p, vw, preferred_element_type=F32)

    for i0 in range(0, rows_per_step, group):
        chains, where = [], []
        for i in range(i0, i0 + group):
            r = blk * rows_per_step + i
            r0 = jnp.clip(r - NA_ROWS // 2, 0, n_rows - NA_ROWS)
            dsel = r - r0
            k0 = pl.multiple_of(r0 * GRID_W, GRID_W)
            rows = slice(i * GRID_W, (i + 1) * GRID_W)
            for pair in range(N_HEADS // 2):
                lanes = slice(pair * V7X_LANES, (pair + 1) * V7X_LANES)
                q_pair = q_ref[rows, lanes]
                kw = k_ref[pl.ds(k0, win), lanes]
                vw = v_ref[pl.ds(k0, win), lanes]
                for sub in range(2):
                    chains.append(head_chain(q_pair, kw, vw, bias_ref[2 * pair + sub, dsel], sub))
                where.append((rows, lanes))
        outs = _run_interleaved(chains)
        for n, (rows, lanes) in enumerate(where):
            o_ref[rows, lanes] = jnp.where(lane < HEAD_DIM, outs[2 * n], outs[2 * n + 1]).astype(o_ref.dtype)


def _na(proj, b, s, bias, rows_per_step=8, group=4):
    n_rows = s // GRID_W
    rows_per_step = min(rows_per_step, n_rows)
    bm = rows_per_step * GRID_W
    nt = s // bm
    return pl.pallas_call(
        functools.partial(_na_kernel, rows_per_step=rows_per_step, n_rows=n_rows,
                          group=math.gcd(group, rows_per_step)),
        grid=(b, nt),
        in_specs=[pl.BlockSpec((bm, GROUP_WIDTH), lambda bi, ri: (bi * nt + ri, U_DQ // 2)),
                  pl.BlockSpec((s, GROUP_WIDTH), lambda bi, ri: (bi, U_DK // 2)),
                  pl.BlockSpec((s, GROUP_WIDTH), lambda bi, ri: (bi, U_DV // 2)),
                  pl.BlockSpec(bias.shape, lambda bi, ri: (0, 0, 0, 0))],
        out_specs=pl.BlockSpec((bm, GROUP_WIDTH), lambda bi, ri: (bi * nt + ri, 0)),
        out_shape=jax.ShapeDtypeStruct((b * s, GROUP_WIDTH), BF16),
        compiler_params=_params("parallel", "arbitrary"),
        name="na_attn",
    )(proj, proj, proj, bias)


def _na_bias_table(rpb, n_rows):
    wr = min(NA_ROWS, n_rows)
    cols = np.arange(GRID_W)
    col_start = np.clip(cols - NA_COLS // 2, 0, GRID_W - NA_COLS)
    ck = np.arange(GRID_W)[None, :]
    valid = (ck >= col_start[:, None]) & (ck < col_start[:, None] + NA_COLS)
    crel = ck - cols[:, None] + NA_COLS - 1
    rrel = np.arange(wr)[None, :] - np.arange(wr)[:, None] + NA_ROWS - 1
    c_sel = ((crel[None] == np.arange(2 * NA_COLS - 1)[:, None, None]) & valid[None]).astype(np.float32)
    r_sel = (rrel[None] == np.arange(2 * NA_ROWS - 1)[:, None, None]).astype(np.float32)
    t = jnp.einsum('hab,adw,bqk->hdqwk', rpb.astype(F32), r_sel, c_sel, precision=lax.Precision.HIGHEST)
    t = t + np.where(valid, 0.0, NEG_BIG).astype(np.float32)[None, None, :, None, :]
    return t.reshape(rpb.shape[0], wr, GRID_W, wr * GRID_W)


def _outproj_kernel(ya_ref, yb_ref, yc_ref, yd_ref, x_ref, w_ref, o_ref):
    acc = x_ref[...] + jnp.dot(yd_ref[...], w_ref[3], preferred_element_type=F32)
    for g, y_ref in enumerate((ya_ref, yb_ref, yc_ref)):
        acc = acc + lax.dot_general(y_ref[0], w_ref[g], (((0,), (0,)), ((), ())),
                                    preferred_element_type=F32)
    o_ref[...] = acc


def _outproj(ya, yb, yc, yd, x, w, b, s, bm=512):
    bm = min(bm, s)
    nt = s // bm
    d = x.shape[1]
    yspec = pl.BlockSpec((1, GROUP_WIDTH, bm), lambda bi, si: (bi, 0, si))
    return pl.pallas_call(
        _outproj_kernel,
        grid=(b, nt),
        in_specs=[yspec, yspec, yspec,
                  pl.BlockSpec((bm, GROUP_WIDTH), lambda bi, si: (bi * nt + si, 0)),
                  pl.BlockSpec((bm, d), lambda bi, si: (bi * nt + si, 0)),
                  pl.BlockSpec(w.shape, lambda bi, si: (0, 0, 0))],
        out_specs=pl.BlockSpec((bm, d), lambda bi, si: (bi * nt + si, 0)),
        out_shape=jax.ShapeDtypeStruct(x.shape, F32),
        compiler_params=_params("parallel", "parallel"),
        name="outproj",
    )(ya, yb, yc, yd, x, w)


FFN_ROW_CHUNKS = 2


def _swiglu_accumulate(hn_sc, acc_sc, w_gate, w_up, w_down):
    rows = hn_sc.shape[0] // FFN_ROW_CHUNKS

    def chunk(c):
        sl = slice(c * rows, (c + 1) * rows)
        hn = hn_sc[sl, :]
        gate = jnp.dot(hn, w_gate, preferred_element_type=F32)
        up = jnp.dot(hn, w_up, preferred_element_type=F32)
        yield
        act = (gate * jax.nn.sigmoid(gate) * up).astype(BF16)
        yield
        acc_sc[sl, :] += jnp.dot(act, w_down, preferred_element_type=F32)

    _run_interleaved([chunk(c) for c in range(FFN_ROW_CHUNKS)])


def _ffn_kernel(x_ref, g_ref, wg_ref, wu_ref, wd_ref, o_ref, hn_sc, acc_sc):
    f = pl.program_id(1)

    @pl.when(f == 0)
    def _():
        x = x_ref[...]
        hn_sc[...] = (_rms(x, x.shape[-1]) * g_ref[...]).astype(BF16)
        acc_sc[...] = jnp.zeros_like(acc_sc)

    _swiglu_accumulate(hn_sc, acc_sc, wg_ref[...], wu_ref[...], wd_ref[...])

    @pl.when(f == pl.num_programs(1) - 1)
    def _():
        o_ref[...] = x_ref[...] + acc_sc[...]


def _ffn(x, g, wg, wu, wd, bm=512):
    n, d = x.shape
    ff = wg.shape[1]
    bm = min(bm, n)
    bf = ff
    once = pl.Buffered(1)
    return pl.pallas_call(
        _ffn_kernel,
        grid=(n // bm, ff // bf),
        in_specs=[pl.BlockSpec((bm, d), lambda i, f: (i, 0)),
                  pl.BlockSpec((1, d), lambda i, f: (0, 0)),
                  pl.BlockSpec((d, bf), lambda i, f: (0, f), pipeline_mode=once),
                  pl.BlockSpec((d, bf), lambda i, f: (0, f), pipeline_mode=once),
                  pl.BlockSpec((bf, d), lambda i, f: (f, 0), pipeline_mode=once)],
        out_specs=pl.BlockSpec((bm, d), lambda i, f: (i, 0)),
        out_shape=jax.ShapeDtypeStruct(x.shape, F32),
        scratch_shapes=[pltpu.VMEM((bm, d), BF16), pltpu.VMEM((bm, d), F32)],
        compiler_params=_params("parallel", "arbitrary"),
        name="ffn",
    )(x, g.reshape(1, d), wg, wu, wd)


R_E1, R_E2, R_W1, R_W2, R_RANK1, R_RANK2 = range(6)


def _lane_pack(lane, cols):
    out = jnp.zeros(lane.shape, F32)
    for idx, col in cols:
        out = out + jnp.where(lane == idx, col, 0.0)
    return out


def _router_kernel(x_ref, g_ref, wr_ref, hn_ref, route_ref, counts_ref, carry_sc):
    @pl.when(pl.program_id(0) == 0)
    def _():
        carry_sc[...] = jnp.zeros_like(carry_sc)

    x = x_ref[...]
    bm = x.shape[0]
    hn = _rms(x, x.shape[-1]) * g_ref[...]
    hn_ref[...] = hn
    logits = jnp.dot(hn, wr_ref[...], preferred_element_type=F32, precision=lax.Precision.HIGHEST)
    lane = lax.broadcasted_iota(jnp.int32, logits.shape, 1)
    logits = jnp.where(lane < N_EXPERTS, logits, NEG_BIG)
    m1 = jnp.max(logits, axis=-1, keepdims=True)
    i1 = jnp.min(jnp.where(logits == m1, lane, V7X_LANES), axis=-1, keepdims=True)
    rest = jnp.where(lane == i1, NEG_BIG, logits)
    m2 = jnp.max(rest, axis=-1, keepdims=True)
    i2 = jnp.min(jnp.where(rest == m2, lane, V7X_LANES), axis=-1, keepdims=True)
    e2 = jnp.exp(m2 - m1)
    w1 = 1.0 / (1.0 + e2)
    w2 = e2 / (1.0 + e2)
    hot1 = jnp.where(lane == i1, 1.0, 0.0)
    hot2 = jnp.where(lane == i2, 1.0, 0.0)
    r_i = lax.broadcasted_iota(jnp.int32, (bm, bm), 0)
    c_i = lax.broadcasted_iota(jnp.int32, (bm, bm), 1)
    below = jnp.where(c_i < r_i, 1.0, 0.0).astype(BF16)
    before1 = jnp.dot(below, hot1.astype(BF16), preferred_element_type=F32)
    before2 = jnp.dot(below, hot2.astype(BF16), preferred_element_type=F32)
    cnt1 = jnp.sum(hot1, axis=0, keepdims=True)
    cnt2 = jnp.sum(hot2, axis=0, keepdims=True)
    carry = carry_sc[...]
    rank1 = jnp.sum(hot1 * (before1 + carry), axis=-1, keepdims=True)
    rank2 = jnp.sum(hot2 * (before2 + carry + cnt1), axis=-1, keepdims=True)
    carry = carry + cnt1 + cnt2
    carry_sc[...] = carry
    counts_ref[...] = carry
    route_ref[...] = _lane_pack(lane, ((R_E1, i1.astype(F32)), (R_E2, i2.astype(F32)), (R_W1, w1), (R_W2, w2),
                                       (R_RANK1, rank1), (R_RANK2, rank2)))


def _router(x, g, w_router, bm=512):
    n, d = x.shape
    bm = min(bm, n)
    wr = jnp.concatenate([w_router, jnp.zeros((d, V7X_LANES - N_EXPERTS), w_router.dtype)], axis=1)
    return pl.pallas_call(
        _router_kernel,
        grid=(n // bm,),
        in_specs=[pl.BlockSpec((bm, d), lambda i: (i, 0)),
                  pl.BlockSpec((1, d), lambda i: (0, 0)),
                  pl.BlockSpec((d, V7X_LANES), lambda i: (0, 0))],
        out_specs=[pl.BlockSpec((bm, d), lambda i: (i, 0)),
                   pl.BlockSpec((bm, V7X_LANES), lambda i: (i, 0)),
                   pl.BlockSpec((1, V7X_LANES), lambda i: (0, 0))],
        out_shape=[jax.ShapeDtypeStruct((n, d), F32),
                   jax.ShapeDtypeStruct((n, V7X_LANES), F32),
                   jax.ShapeDtypeStruct((1, V7X_LANES), F32)],
        scratch_shapes=[pltpu.VMEM((1, V7X_LANES), F32)],
        compiler_params=_params("arbitrary"),
        name="moe_router",
    )(x, g.reshape(1, d), wr)


def _route_plan(route, counts, tm):
    n = route.shape[0]
    counts = counts[0, :N_EXPERTS].astype(jnp.int32)
    padded = ((counts + tm - 1) // tm) * tm
    g_end = jnp.cumsum(padded)
    g_start = (g_end - padded).astype(F32)
    experts = jnp.arange(N_EXPERTS, dtype=F32)[None, :]
    start1 = jnp.sum(jnp.where(route[:, R_E1:R_E1 + 1] == experts, g_start[None, :], 0.0), axis=1)
    start2 = jnp.sum(jnp.where(route[:, R_E2:R_E2 + 1] == experts, g_start[None, :], 0.0), axis=1)
    pos = jnp.stack([start1 + route[:, R_RANK1], start2 + route[:, R_RANK2]], axis=1).astype(jnp.int32)
    n_tiles = 2 * n // tm + N_EXPERTS
    tile_start = jnp.arange(n_tiles, dtype=jnp.int32) * tm
    tile_expert = jnp.minimum(jnp.sum(tile_start[:, None] >= g_end[None, :], axis=1), N_EXPERTS - 1)
    n_used = (g_end[-1:] // tm).astype(jnp.int32)
    return pos.reshape(-1), tile_expert.astype(jnp.int32), n_used, n_tiles


def _row_copies(pos_ref, r, src_at, dst_at, sem):
    copies = []
    for c in range(2):
        p = pos_ref[2 * r + c]
        copies.append(pltpu.make_async_copy(src_at(r, c, p), dst_at(r, c, p), sem))
    return copies


def _move_rows(pos_ref, n_rows, src_at, dst_at, sem):
    def start(r, carry):
        for cp in _row_copies(pos_ref, r, src_at, dst_at, sem):
            cp.start()
        return carry

    def wait(r, carry):
        for cp in _row_copies(pos_ref, r, src_at, dst_at, sem):
            cp.wait()
        return carry

    lax.fori_loop(0, n_rows, start, 0, unroll=8)
    lax.fori_loop(0, n_rows, wait, 0, unroll=8)


def _dispatch_kernel(pos_ref, hn_ref, xg_in_ref, xg_ref, sem):
    del xg_in_ref
    _move_rows(pos_ref, hn_ref.shape[0],
               lambda r, c, p: hn_ref.at[pl.ds(r, 1)],
               lambda r, c, p: xg_ref.at[pl.ds(p, 1)], sem)


def _dispatch(pos, hn, n_rows, bm=512):
    n, d = hn.shape
    bm = min(bm, n)
    return pl.pallas_call(
        _dispatch_kernel,
        grid=(n // bm,),
        in_specs=[pl.BlockSpec((2 * bm,), lambda i: (i,), memory_space=pltpu.SMEM),
                  pl.BlockSpec((bm, d), lambda i: (i, 0)),
                  pl.BlockSpec(memory_space=pl.ANY)],
        out_specs=pl.BlockSpec(memory_space=pl.ANY),
        out_shape=jax.ShapeDtypeStruct((n_rows, d), F32),
        scratch_shapes=[pltpu.SemaphoreType.DMA(())],
        input_output_aliases={2: 0},
        compiler_params=_params("arbitrary"),
        name="moe_dispatch",
    )(pos, hn, jnp.zeros((n_rows, d), F32))


def _expert_ffn_kernel(te_ref, nu_ref, xg_ref, wg_ref, wu_ref, wd_ref, o_ref, hn_sc, acc_sc):
    del te_ref
    t = pl.program_id(0)
    f = pl.program_id(1)
    last = pl.num_programs(1) - 1
    used = t < nu_ref[0]

    @pl.when(used & (f == 0))
    def _():
        hn_sc[...] = xg_ref[...].astype(BF16)
        acc_sc[...] = jnp.zeros_like(acc_sc)

    @pl.when(used)
    def _():
        _swiglu_accumulate(hn_sc, acc_sc, wg_ref[0], wu_ref[0], wd_ref[0])

    @pl.when(used & (f == last))
    def _():
        o_ref[...] = acc_sc[...]

    @pl.when(jnp.logical_not(used) & (f == last))
    def _():
        o_ref[...] = jnp.zeros_like(o_ref)


def _expert_ffn(tile_expert, n_used, xg, wg, wu, wd, tm, bf=7 * V7X_MXU_COLS):
    rows, d = xg.shape
    ff = wg.shape[2]
    bf = math.gcd(bf, ff)
    grid_spec = pltpu.PrefetchScalarGridSpec(
        num_scalar_prefetch=2,
        grid=(rows // tm, ff // bf),
        in_specs=[pl.BlockSpec((tm, d), lambda t, f, te, nu: (t, 0)),
                  pl.BlockSpec((1, d, bf), lambda t, f, te, nu: (te[t], 0, f)),
                  pl.BlockSpec((1, d, bf), lambda t, f, te, nu: (te[t], 0, f)),
                  pl.BlockSpec((1, bf, d), lambda t, f, te, nu: (te[t], f, 0))],
        out_specs=pl.BlockSpec((tm, d), lambda t, f, te, nu: (t, 0)),
        scratch_shapes=[pltpu.VMEM((tm, d), BF16), pltpu.VMEM((tm, d), F32)])
    return pl.pallas_call(
        _expert_ffn_kernel,
        grid_spec=grid_spec,
        out_shape=jax.ShapeDtypeStruct((rows, d), F32),
        compiler_params=_params("arbitrary", "arbitrary"),
        name="moe_ffn",
    )(tile_expert, n_used, xg, wg, wu, wd)


def _combine_kernel(pos_ref, x_ref, route_ref, g_ref, yg_ref, o_ref, buf, sem, *, final_norm):
    _move_rows(pos_ref, x_ref.shape[0],
               lambda r, c, p: yg_ref.at[pl.ds(p, 1)],
               lambda r, c, p: buf.at[c, pl.ds(r, 1)], sem)
    route = route_ref[...]
    y = x_ref[...] + route[:, R_W1:R_W1 + 1] * buf[0] + route[:, R_W2:R_W2 + 1] * buf[1]
    if final_norm:
        y = _rms(y, y.shape[-1]) * g_ref[...]
    o_ref[...] = y


def _combine(pos, x, route, yg, g_final, final_norm, row0, n_rows, bm=512):
    d = x.shape[1]
    bm = math.gcd(bm, math.gcd(row0, n_rows)) if row0 else min(bm, n_rows)
    off = row0 // bm
    return pl.pallas_call(
        functools.partial(_combine_kernel, final_norm=final_norm),
        grid=(n_rows // bm,),
        in_specs=[pl.BlockSpec((2 * bm,), lambda i: (i + off,), memory_space=pltpu.SMEM),
                  pl.BlockSpec((bm, d), lambda i: (i + off, 0)),
                  pl.BlockSpec((bm, V7X_LANES), lambda i: (i + off, 0)),
                  pl.BlockSpec((1, d), lambda i: (0, 0)),
                  pl.BlockSpec(memory_space=pl.ANY)],
        out_specs=pl.BlockSpec((bm, d), lambda i: (i, 0)),
        out_shape=jax.ShapeDtypeStruct((n_rows, d), F32),
        scratch_shapes=[pltpu.VMEM((2, bm, d), F32), pltpu.SemaphoreType.DMA(())],
        compiler_params=_params("arbitrary"),
        name="moe_combine",
    )(pos, x, route, g_final.reshape(1, d), yg)


def _moe(x, g, w_router, wg, wu, wd, g_final, final_norm, row_splits, tm=512):
    hn, route, counts = _router(x, g, w_router)
    pos, tile_expert, n_used, n_tiles = _route_plan(route, counts, tm)
    xg = _dispatch(pos, hn, n_tiles * tm)
    yg = _expert_ffn(tile_expert, n_used, xg, wg, wu, wd, tm)
    return [_combine(pos, x, route, yg, g_final, final_norm, row0, n_rows) for row0, n_rows in row_splits]


def _final_norm_kernel(x_ref, g_ref, o_ref):
    x = x_ref[...]
    o_ref[...] = _rms(x, x.shape[-1]) * g_ref[...]


def _final_norm(x, g, bm=1024):
    n, d = x.shape
    bm = min(bm, n)
    return pl.pallas_call(
        _final_norm_kernel,
        grid=(n // bm,),
        in_specs=[pl.BlockSpec((bm, d), lambda i: (i, 0)), pl.BlockSpec((1, d), lambda i: (0, 0))],
        out_specs=pl.BlockSpec((bm, d), lambda i: (i, 0)),
        out_shape=jax.ShapeDtypeStruct(x.shape, F32),
        compiler_params=_params("parallel"),
        name="final_norm",
    )(x, g.reshape(1, d))


def _token_mix(x, i, b, s, p):
    w_main, w_gates = _prep_w_in(p['w_in'][i])
    proj, gates = _inproj(x, p['norm_mix'][i], w_main, w_gates)

    cos_t, sin_t = _rope_tables(s)
    q_a, k_a, vt_a = _mla_prep(proj, b, s, cos_t, sin_t, *_prep_mla_weights(
        p['mla_q_norm'][i], p['mla_kv_norm'][i], p['mla_w_uq'][i], p['mla_w_ukv'][i]))
    y_a = _mla_flash(q_a, k_a, vt_a)

    lp = p['diff_lambda'][i].astype(F32)
    lam_init = 0.8 - 0.6 * math.exp(-0.3 * i)
    lam = jnp.exp(jnp.sum(lp[0] * lp[1])) - jnp.exp(jnp.sum(lp[2] * lp[3])) + lam_init
    slopes = 2.0 ** (-8.0 * jnp.arange(1, N_HEADS + 1, dtype=F32) / N_HEADS)
    scalars = jnp.concatenate([-slopes * LOG2E, lam[None]]).astype(F32)
    k_aug, q_aug = _alibi_tables(s)
    qa_b, k_b, vt_b = _diff_prep(proj, k_aug, b, s)
    y_b = _diff_flash(scalars, qa_b, q_aug, k_b, vt_b, p['diff_subln'][i], i)

    qk_c, vt_c, opt_c = _conv_silu(proj, b, s, p['mlstm_conv'][i])
    n_gates = 4 * N_HEADS
    gate_bias = jnp.concatenate([p['mlstm_gate_bias'][i], jnp.zeros((V7X_LANES - n_gates,), F32)])[None, :]
    grow = (gates[:, :n_gates] + gate_bias[:, :n_gates]).reshape(b, s, n_gates).transpose(0, 2, 1)
    y_c = _mlstm(qk_c, vt_c, gates, gate_bias, grow, opt_c, p['mlstm_norm'][i], b, s)

    y_d = _na(proj, b, s, _na_bias_table(p['na_rpb'][i], s // GRID_W))

    w_out = p['w_out'][i].reshape(4, GROUP_WIDTH, -1).astype(BF16)
    return _outproj(y_a, y_b, y_c, y_d, x, w_out, b, s)


def _trunk(x, p, depth, batch_splits):
    b, s, d = x.shape
    x = x.reshape(b * s, d)
    starts = np.cumsum((0,) + tuple(batch_splits))[:-1]
    row_splits = [(int(b0) * s, int(nb) * s) for b0, nb in zip(starts, batch_splits)]
    for i in range(depth):
        x = _token_mix(x, i, b, s, p)
        j = i // 2
        last = i == depth - 1
        if i % 2 == 0:
            x = _ffn(x, p['norm_ffn'][i], p['ffn_w_gate'][j].astype(BF16), p['ffn_w_up'][j].astype(BF16),
                     p['ffn_w_down'][j].astype(BF16))
            if last:
                x = _final_norm(x, p['norm_final'])
                outs = [x[r0:r0 + nr] for r0, nr in row_splits]
        else:
            outs = _moe(x, p['norm_ffn'][i], p['moe_router'][j], p['moe_w_gate'][j].astype(BF16),
                        p['moe_w_up'][j].astype(BF16), p['moe_w_down'][j].astype(BF16), p['norm_final'],
                        final_norm=last, row_splits=row_splits if last else [(0, b * s)])
            x = outs[0]
    return [o.reshape(nb, s, d) for o, nb in zip(outs, batch_splits)]


def kernel(x_prompt, x_sample, norm_mix, norm_ffn, w_in, w_out, mla_q_norm, mla_kv_norm, mla_w_uq, mla_w_ukv,
           diff_lambda, diff_subln, mlstm_conv, mlstm_gate_bias, mlstm_norm, na_rpb, ffn_w_gate, ffn_w_up,
           ffn_w_down, moe_router, moe_w_gate, moe_w_up, moe_w_down, norm_final):
    p = dict(norm_mix=norm_mix, norm_ffn=norm_ffn, w_in=w_in, w_out=w_out, mla_q_norm=mla_q_norm,
             mla_kv_norm=mla_kv_norm, mla_w_uq=mla_w_uq, mla_w_ukv=mla_w_ukv, diff_lambda=diff_lambda,
             diff_subln=diff_subln, mlstm_conv=mlstm_conv, mlstm_gate_bias=mlstm_gate_bias,
             mlstm_norm=mlstm_norm, na_rpb=na_rpb, ffn_w_gate=ffn_w_gate, ffn_w_up=ffn_w_up,
             ffn_w_down=ffn_w_down, moe_router=moe_router, moe_w_gate=moe_w_gate, moe_w_up=moe_w_up,
             moe_w_down=moe_w_down, norm_final=norm_final)
    depth = norm_mix.shape[0]
    y_prompt, y_sample = _trunk(jnp.concatenate([x_prompt, x_sample], axis=0), p, depth,
                                (x_prompt.shape[0], x_sample.shape[0]))
    return (y_prompt, y_sample)
```

```python
import functools
import math

import numpy as np
import jax
import jax.numpy as jnp
from jax import lax
from jax.experimental import pallas as pl
from jax.experimental.pallas import tpu as pltpu

F32 = jnp.float32
BF16 = jnp.bfloat16

V7X_LANES = 128
V7X_MXU_COLS = 256
V7X_VMEM_LIMIT_BYTES = 56 * 1024 * 1024

EPS = 1e-6
LOG2E = 1.4426950408889634
NEG_BIG = -1e30

HEAD_DIM = 64
N_HEADS = 4
GROUP_WIDTH = 256
MLA_Q_LORA = 192
MLA_KV_LORA = 128
MLA_NOPE = 64
MLA_ROPE = 32
ROPE_THETA = 10000.0
DIFF_HALF = 32
MLSTM_CHUNK = 128
GRID_W = 64
NA_ROWS = 8
NA_COLS = 16
N_EXPERTS = 8
ONES_ROWS = 16
VT_ROWS = HEAD_DIM + ONES_ROWS
MLA_REF_LANE = MLA_NOPE + MLA_ROPE
DIFF_REF_LANE = HEAD_DIM
DIFF_POS_LANE = DIFF_REF_LANE + 1

IN_SECTIONS = (192, 128, 32, 256, 256, 256, 512, 256, 256, 16, 256, 256, 256)
U_BQ, U_BK, U_BV, U_DQ, U_DK, U_DV, U_CQK, U_CV, U_CO, U_ACQ, U_ACKV, U_AKR, U_AKRR = (
    0, 2, 4, 6, 8, 10, 12, 16, 18, 20, 22, 23, 24)
PROJ_COLS = 25 * V7X_LANES


def _params(*sem):
    return pltpu.CompilerParams(dimension_semantics=sem, vmem_limit_bytes=V7X_VMEM_LIMIT_BYTES)


def _rms(x, n):
    return x * lax.rsqrt(jnp.sum(x * x, axis=-1, keepdims=True) * (1.0 / n) + EPS)


def _row_source_specs(xs, bm, tile_of):
    d = xs[0].shape[1]
    if len(xs) == 1:
        return [pl.BlockSpec((bm, d), lambda *g: (tile_of(*g), 0))], None
    n_a = xs[0].shape[0] // bm
    return [pl.BlockSpec((bm, d), lambda *g: (jnp.minimum(tile_of(*g), n_a - 1), 0)),
            pl.BlockSpec((bm, d), lambda *g: (jnp.maximum(tile_of(*g) - n_a, 0), 0))], n_a


def _read_rows(x_refs, tile, n_a):
    if n_a is None:
        return x_refs[0][...]
    return jnp.where(tile < n_a, x_refs[0][...], x_refs[1][...])


def _inproj_kernel(*refs, n_src, n_a, col_chunk):
    x_refs, (g_ref, w_ref, wg_ref, o_ref, og_ref) = refs[:n_src], refs[n_src:]
    x = _read_rows(x_refs, pl.program_id(0), n_a)
    hn = (_rms(x, x.shape[-1]) * g_ref[...]).astype(BF16)
    for c in range(0, o_ref.shape[1], col_chunk):
        w = min(col_chunk, o_ref.shape[1] - c)
        o_ref[:, c:c + w] = jnp.dot(hn, w_ref[:, c:c + w], preferred_element_type=F32).astype(BF16)
    og_ref[...] = jnp.dot(hn, wg_ref[...], preferred_element_type=F32)


def _inproj(xs, g, w_main, w_gates, bm=512):
    n = sum(x.shape[0] for x in xs)
    d = xs[0].shape[1]
    bm = min(bm, n)
    x_specs, n_a = _row_source_specs(xs, bm, lambda i: i)
    return pl.pallas_call(
        functools.partial(_inproj_kernel, n_src=len(xs), n_a=n_a, col_chunk=4 * V7X_MXU_COLS),
        grid=(n // bm,),
        in_specs=x_specs + [pl.BlockSpec((1, d), lambda i: (0, 0)),
                            pl.BlockSpec(w_main.shape, lambda i: (0, 0)),
                            pl.BlockSpec(w_gates.shape, lambda i: (0, 0))],
        out_specs=[pl.BlockSpec((bm, PROJ_COLS), lambda i: (i, 0)),
                   pl.BlockSpec((bm, V7X_LANES), lambda i: (i, 0))],
        out_shape=[jax.ShapeDtypeStruct((n, PROJ_COLS), BF16),
                   jax.ShapeDtypeStruct((n, V7X_LANES), F32)],
        compiler_params=_params("parallel"),
        name="inproj",
    )(*xs, g.reshape(1, d), w_main, w_gates)


def _prep_w_in(w):
    d = w.shape[0]
    offs = np.cumsum((0,) + IN_SECTIONS)
    (a_cq, a_ckv, a_kr, b_q, b_k, b_v, c_qk, c_v, c_o, c_g, d_q, d_k, d_v) = [
        w[:, offs[i]:offs[i + 1]] for i in range(len(IN_SECTIONS))]
    z = lambda k: jnp.zeros((d, k), w.dtype)
    half = MLA_ROPE // 2
    a_kr_rot = jnp.concatenate([a_kr[:, half:], a_kr[:, :half]], axis=1)
    main = jnp.concatenate([b_q, b_k, b_v, d_q, d_k, d_v, c_qk, c_v, c_o,
                            a_cq, z(64), a_ckv,
                            z(64), a_kr, z(32),
                            z(64), a_kr_rot, z(32)], axis=1)
    gates = jnp.concatenate([c_g, z(V7X_LANES - 16)], axis=1)
    return main.astype(BF16), gates.astype(BF16)


def _mla_prep_kernel(cq_ref, ckv_ref, kr_ref, krr_ref, cos_ref, sin_ref, gq_ref, gkv_ref,
                     wq_ref, wqr_ref, wkn_ref, wvt_ref, ones_ref, q_out, k_out, vt_out, *, q_scale):
    cos = cos_ref[...]
    sin = sin_ref[...]
    qn = (_rms(cq_ref[...].astype(F32), MLA_Q_LORA) * gq_ref[...]).astype(BF16)
    qa = jnp.dot(qn, wq_ref[...], preferred_element_type=F32)
    qr = jnp.dot(qn, wqr_ref[...], preferred_element_type=F32)
    kvn = (_rms(ckv_ref[...].astype(F32), MLA_KV_LORA) * gkv_ref[...]).astype(BF16)
    kn = jnp.dot(kvn, wkn_ref[...], preferred_element_type=F32)
    k_rope = kr_ref[...].astype(F32) * cos + krr_ref[...].astype(F32) * sin
    k_rope = jnp.where(lax.broadcasted_iota(jnp.int32, k_rope.shape, 1) == MLA_REF_LANE, 1.0, k_rope)
    def head_stages(h):
        sl = slice(h * V7X_LANES, (h + 1) * V7X_LANES)
        q_h = ((qa[:, sl] * cos + qr[:, sl] * sin) * q_scale).astype(BF16)
        k_h = (kn[:, sl] + k_rope).astype(BF16)
        k_out[0, h] = k_h
        vt = lax.dot_general(wvt_ref[h], kvn, _NT, preferred_element_type=F32)
        yield
        vt_out[0, h] = (vt + ones_ref[...]).astype(BF16)
        m_col = _diag_ref_max(q_h, k_h)
        yield
        q_out[0, h] = _with_ref_column(q_h, m_col, MLA_REF_LANE)

    _run_interleaved([head_stages(h) for h in range(N_HEADS)])


def _mla_prep(proj, b, s, cos_t, sin_t, gq, gkv, wq, wqr, wkn, wvt, bm=512):
    bm = min(bm, s)
    nt = s // bm
    row = lambda bi, si: bi * nt + si
    ones_col = jnp.concatenate([jnp.zeros((HEAD_DIM, 1), F32), jnp.ones((ONES_ROWS, 1), F32)], axis=0)
    full = lambda a: pl.BlockSpec(a.shape, lambda bi, si: (0,) * a.ndim)
    return pl.pallas_call(
        functools.partial(_mla_prep_kernel, q_scale=LOG2E * (MLA_NOPE + MLA_ROPE) ** -0.5),
        grid=(b, nt),
        in_specs=[pl.BlockSpec((bm, 2 * V7X_LANES), lambda bi, si: (row(bi, si), U_ACQ // 2)),
                  pl.BlockSpec((bm, V7X_LANES), lambda bi, si: (row(bi, si), U_ACKV)),
                  pl.BlockSpec((bm, V7X_LANES), lambda bi, si: (row(bi, si), U_AKR)),
                  pl.BlockSpec((bm, V7X_LANES), lambda bi, si: (row(bi, si), U_AKRR)),
                  pl.BlockSpec((bm, V7X_LANES), lambda bi, si: (si, 0)),
                  pl.BlockSpec((bm, V7X_LANES), lambda bi, si: (si, 0)),
                  full(gq), full(gkv), full(wq), full(wqr), full(wkn), full(wvt), full(ones_col)],
        out_specs=[pl.BlockSpec((1, N_HEADS, bm, V7X_LANES), lambda bi, si: (bi, 0, si, 0)),
                   pl.BlockSpec((1, N_HEADS, bm, V7X_LANES), lambda bi, si: (bi, 0, si, 0)),
                   pl.BlockSpec((1, N_HEADS, VT_ROWS, bm), lambda bi, si: (bi, 0, 0, si))],
        out_shape=[jax.ShapeDtypeStruct((b, N_HEADS, s, V7X_LANES), BF16),
                   jax.ShapeDtypeStruct((b, N_HEADS, s, V7X_LANES), BF16),
                   jax.ShapeDtypeStruct((b, N_HEADS, VT_ROWS, s), BF16)],
        compiler_params=_params("parallel", "parallel"),
        name="mla_prep",
    )(proj, proj, proj, proj, cos_t, sin_t, gq, gkv, wq, wqr, wkn, wvt, ones_col)


def _prep_mla_weights(q_norm, kv_norm, w_uq, w_ukv):
    half = MLA_ROPE // 2
    dq = MLA_NOPE + MLA_ROPE
    wq_h = w_uq.reshape(MLA_Q_LORA, N_HEADS, dq)
    zq = lambda k: jnp.zeros((MLA_Q_LORA, N_HEADS, k), w_uq.dtype)
    rope_cols = wq_h[:, :, MLA_NOPE:]
    rope_rot = jnp.concatenate([rope_cols[:, :, half:], rope_cols[:, :, :half]], axis=2)
    wq = jnp.concatenate([wq_h, zq(V7X_LANES - dq)], axis=2).reshape(MLA_Q_LORA, N_HEADS * V7X_LANES)
    wqr = jnp.concatenate([zq(MLA_NOPE), rope_rot, zq(V7X_LANES - dq)], axis=2).reshape(
        MLA_Q_LORA, N_HEADS * V7X_LANES)
    pad_rows = jnp.zeros((2 * V7X_LANES - MLA_Q_LORA, N_HEADS * V7X_LANES), w_uq.dtype)
    wq = jnp.concatenate([wq, pad_rows], axis=0).astype(BF16)
    wqr = jnp.concatenate([wqr, pad_rows], axis=0).astype(BF16)
    wkv_h = w_ukv.reshape(MLA_KV_LORA, N_HEADS, MLA_NOPE + HEAD_DIM)
    wkn = jnp.concatenate([wkv_h[:, :, :MLA_NOPE],
                           jnp.zeros((MLA_KV_LORA, N_HEADS, V7X_LANES - MLA_NOPE), w_ukv.dtype)],
                          axis=2).reshape(MLA_KV_LORA, N_HEADS * V7X_LANES).astype(BF16)
    wvt = jnp.transpose(wkv_h[:, :, MLA_NOPE:], (1, 2, 0))
    wvt = jnp.concatenate([wvt, jnp.zeros((N_HEADS, ONES_ROWS, MLA_KV_LORA), w_ukv.dtype)],
                          axis=1).astype(BF16)
    gq = jnp.concatenate([q_norm, jnp.zeros((2 * V7X_LANES - MLA_Q_LORA,), q_norm.dtype)]).reshape(1, -1)
    gkv = kv_norm.reshape(1, -1)
    return gq, gkv, wq, wqr, wkn, wvt


def _rope_tables(s):
    half = MLA_ROPE // 2
    inv = ROPE_THETA ** (-jnp.arange(half, dtype=F32) / half)
    ang = jnp.arange(s).astype(F32)[:, None] * inv[None, :]
    cos, sin = jnp.cos(ang), jnp.sin(ang)
    ones = jnp.ones((s, MLA_NOPE), F32)
    z = lambda k: jnp.zeros((s, k), F32)
    pad = V7X_LANES - MLA_NOPE - MLA_ROPE
    cos_t = jnp.concatenate([ones, cos, cos, z(pad)], axis=1)
    sin_t = jnp.concatenate([z(MLA_NOPE), -sin, sin, z(pad)], axis=1)
    return cos_t, sin_t


_NT = (((1,), (1,)), ((), ()))


def _diag_ref_max(q, k, bias=None):
    groups = [slice(j * V7X_LANES, (j + 1) * V7X_LANES) for j in range(q.shape[0] // V7X_LANES)]
    scores = [lax.dot_general(q[rows], k[rows], _NT, preferred_element_type=F32) for rows in groups]
    if bias is not None:
        scores = [sc + bias for sc in scores]
    return jnp.concatenate([jnp.max(sc, axis=1, keepdims=True) for sc in scores], axis=0)


def _with_ref_column(q, m_col, ref_lane):
    lane = lax.broadcasted_iota(jnp.int32, q.shape, 1)
    return jnp.where(lane == ref_lane, -m_col, q.astype(F32)).astype(BF16)


def _without_ref_column(q_aug, ref_lane):
    lane = lax.broadcasted_iota(jnp.int32, q_aug.shape, 1)
    return jnp.where(lane == ref_lane, 0.0, q_aug.astype(F32)).astype(BF16)


def _not_finite(acc):
    return jnp.max(jnp.where(jnp.isfinite(acc), 0.0, 1.0)) > 0.0


def _online_block(k_blk, q, vt_blk, m, acc, bias=None):
    st = lax.dot_general(k_blk, q, _NT, preferred_element_type=F32)
    if bias is not None:
        st = st + bias
    m_new = jnp.maximum(m, jnp.max(st, axis=0, keepdims=True))
    p = jnp.exp2(st - m_new).astype(BF16)
    return m_new, acc * jnp.exp2(m - m_new) + jnp.dot(vt_blk, p, preferred_element_type=F32)


def _mla_flash_kernel(q_ref, k_ref, vt_ref, o_ref, *, bk, unroll, sub):
    q_aug = q_ref[0, 0]
    bq = q_aug.shape[0]
    s_len = k_ref.shape[2]

    def body(i, acc):
        n_sub = unroll * bk // sub

        def scores(j):
            k0 = pl.multiple_of(i * (unroll * bk) + j * sub, sub)
            return k0, lax.dot_general(k_ref[0, 0, pl.ds(k0, sub), :], q_aug, _NT, preferred_element_type=F32)

        nxt = scores(0)
        for j in range(n_sub):
            (k0, st), nxt = nxt, (scores(j + 1) if j + 1 < n_sub else None)
            acc = acc + jnp.dot(vt_ref[0, 0, :, pl.ds(k0, sub)], jnp.exp2(st).astype(BF16),
                                preferred_element_type=F32)
        return acc

    acc0 = jnp.zeros((VT_ROWS, bq), F32)
    acc = lax.fori_loop(0, s_len // (bk * unroll), body, acc0)
    o_ref[0] = (acc[:HEAD_DIM] / acc[HEAD_DIM:HEAD_DIM + 1]).astype(o_ref.dtype)

    @pl.when(_not_finite(acc))
    def _():
        q = _without_ref_column(q_aug, MLA_REF_LANE)

        def exact(i, carry):
            k0 = pl.multiple_of(i * bk, bk)
            return _online_block(k_ref[0, 0, pl.ds(k0, bk), :], q, vt_ref[0, 0, :, pl.ds(k0, bk)], *carry)

        _, acc_x = lax.fori_loop(0, s_len // bk, exact, (jnp.full((1, bq), NEG_BIG, F32), acc0))
        o_ref[0] = (acc_x[:HEAD_DIM] / acc_x[HEAD_DIM:HEAD_DIM + 1]).astype(o_ref.dtype)


def _mla_flash(q, k, vt, bq=1024, bk=2048, unroll=4, sub=2048):
    b, h, s, dq = q.shape
    bq, bk = min(bq, s), min(bk, s)
    unroll = min(unroll, s // bk)
    return pl.pallas_call(
        functools.partial(_mla_flash_kernel, bk=bk, unroll=unroll, sub=min(sub, bk)),
        grid=(b, h, s // bq),
        in_specs=[pl.BlockSpec((1, 1, bq, dq), lambda bi, hi, qi: (bi, hi, qi, 0)),
                  pl.BlockSpec((1, 1, s, dq), lambda bi, hi, qi: (bi, hi, 0, 0)),
                  pl.BlockSpec((1, 1, VT_ROWS, s), lambda bi, hi, qi: (bi, hi, 0, 0))],
        out_specs=pl.BlockSpec((1, HEAD_DIM, bq), lambda bi, hi, qi: (bi, hi, qi)),
        out_shape=jax.ShapeDtypeStruct((b, h * HEAD_DIM, s), BF16),
        compiler_params=_params("parallel", "parallel", "arbitrary"),
        name="mla_flash",
    )(q, k, vt)


def _lane_selector(n_src, n_dst, src0, dst0, width, transposed=False):
    shape = (n_dst, n_src) if transposed else (n_src, n_dst)
    src = lax.broadcasted_iota(jnp.int32, shape, 1 if transposed else 0)
    dst = lax.broadcasted_iota(jnp.int32, shape, 0 if transposed else 1)
    hit = (src - src0 == dst - dst0) & (dst >= dst0) & (dst < dst0 + width)
    return jnp.where(hit, 1.0, 0.0).astype(BF16)


def _diff_prep_kernel(q_ref, k_ref, v_ref, kaug_ref, qa_out, k_out, vt_out, *, q_scale, slopes):
    q_tile, k_tile, v_tile = q_ref[...], k_ref[...], v_ref[...]
    lane = lax.broadcasted_iota(jnp.int32, (q_tile.shape[0], V7X_LANES), 1)
    diag = jnp.abs((lax.broadcasted_iota(jnp.int32, (V7X_LANES, V7X_LANES), 0)
                    - lax.broadcasted_iota(jnp.int32, (V7X_LANES, V7X_LANES), 1)).astype(F32))
    ones_rows = jnp.where(lax.broadcasted_iota(jnp.int32, (VT_ROWS, 1), 0) >= HEAD_DIM, 1.0, 0.0)
    def head_stages(h):
        c0 = h * HEAD_DIM
        k_h = jnp.dot(k_tile, _lane_selector(GROUP_WIDTH, V7X_LANES, c0, 0, HEAD_DIM), preferred_element_type=F32)
        q_parts = [jnp.dot(q_tile, _lane_selector(GROUP_WIDTH, V7X_LANES, c0 + part * DIFF_HALF,
                                                  part * DIFF_HALF, DIFF_HALF), preferred_element_type=F32)
                   for part in range(2)]
        vt = lax.dot_general(_lane_selector(GROUP_WIDTH, VT_ROWS, c0, 0, HEAD_DIM, transposed=True), v_tile, _NT,
                             preferred_element_type=F32)
        yield
        k_h = jnp.where(lane == DIFF_REF_LANE, 1.0, k_h).astype(BF16)
        k_out[0, h] = (k_h.astype(F32) + kaug_ref[h].astype(F32)).astype(BF16)
        vt_out[0, h] = (vt + ones_rows).astype(BF16)
        q_parts = [(q_p * q_scale).astype(BF16) for q_p in q_parts]
        yield
        refs = [_diag_ref_max(q_p, k_h, diag * slopes[h]) for q_p in q_parts]
        yield
        qa_out[0, h] = jnp.concatenate([_with_ref_column(q_p, m_col, DIFF_REF_LANE)
                                        for q_p, m_col in zip(q_parts, refs)], axis=1)

    _run_interleaved([head_stages(h) for h in range(N_HEADS)])


def _split3(x):
    def keep_top_bits(v):
        bits = lax.bitcast_convert_type(v, jnp.uint32) & jnp.uint32(0xFFFF0000)
        return lax.bitcast_convert_type(bits, F32)

    hi = keep_top_bits(x)
    r1 = x - hi
    mid = keep_top_bits(r1)
    lo = r1 - mid
    return hi.astype(BF16), mid.astype(BF16), lo.astype(BF16)


def _alibi_tables(s):
    c = jnp.asarray([LOG2E * 2.0 ** (-8.0 * (h + 1) / N_HEADS) for h in range(N_HEADS)], F32)
    terms = jnp.stack(_split3(c[:, None] * jnp.arange(s, dtype=F32)[None, :]), axis=-1)
    ones = jnp.ones_like(terms)
    left = jnp.zeros((N_HEADS, s, DIFF_POS_LANE), BF16)
    right = jnp.zeros((N_HEADS, s, V7X_LANES - DIFF_POS_LANE - 6), BF16)
    k_aug = jnp.concatenate([left, terms, ones, right], axis=-1)
    q_aug = jnp.concatenate([left, -ones, terms, right], axis=-1)
    return k_aug, q_aug


def _diff_prep(proj, k_aug, b, s, bm=512):
    bm = min(bm, s)
    nt = s // bm
    slopes = tuple(-LOG2E * 2.0 ** (-8.0 * (h + 1) / N_HEADS) for h in range(N_HEADS))
    spec = lambda unit: pl.BlockSpec((bm, GROUP_WIDTH), lambda bi, si: (bi * nt + si, unit // 2))
    return pl.pallas_call(
        functools.partial(_diff_prep_kernel, q_scale=LOG2E * DIFF_HALF ** -0.5, slopes=slopes),
        grid=(b, nt),
        in_specs=[spec(U_BQ), spec(U_BK), spec(U_BV),
                  pl.BlockSpec((N_HEADS, bm, V7X_LANES), lambda bi, si: (0, si, 0))],
        out_specs=[pl.BlockSpec((1, N_HEADS, bm, 2 * V7X_LANES), lambda bi, si: (bi, 0, si, 0)),
                   pl.BlockSpec((1, N_HEADS, bm, V7X_LANES), lambda bi, si: (bi, 0, si, 0)),
                   pl.BlockSpec((1, N_HEADS, VT_ROWS, bm), lambda bi, si: (bi, 0, 0, si))],
        out_shape=[jax.ShapeDtypeStruct((b, N_HEADS, s, 2 * V7X_LANES), BF16),
                   jax.ShapeDtypeStruct((b, N_HEADS, s, V7X_LANES), BF16),
                   jax.ShapeDtypeStruct((b, N_HEADS, VT_ROWS, s), BF16)],
        compiler_params=_params("parallel", "parallel"),
        name="diff_prep",
    )(proj, proj, proj, k_aug)


def _diff_flash_kernel(sc_ref, q_ref, qaug_ref, k_ref, vt_ref, rel_ref, g_ref, o_ref, *, bk, out_scale):
    hi = pl.program_id(1)
    qi = pl.program_id(2)
    slope = sc_ref[hi]
    lam = sc_ref[N_HEADS]
    q1a = q_ref[0, 0, :, :V7X_LANES]
    q2a = q_ref[0, 0, :, V7X_LANES:]
    bq = q1a.shape[0]
    s_len = k_ref.shape[2]
    n_blocks = s_len // bk
    rel = rel_ref[...]
    q0 = (qi * bq).astype(F32)
    diag_blk = (qi * bq) // bk

    def block_inputs(i):
        k0 = pl.multiple_of(i * bk, bk)
        bias = jnp.abs(rel + (k0.astype(F32) - q0)) * slope
        return k_ref[0, 0, pl.ds(k0, bk), :], vt_ref[0, 0, :, pl.ds(k0, bk)], bias

    def accumulate(a1, a2, k_blk, vt_blk, qx1, qx2, bias):
        s1 = lax.dot_general(k_blk, qx1, _NT, preferred_element_type=F32)
        s2 = lax.dot_general(k_blk, qx2, _NT, preferred_element_type=F32)
        if bias is not None:
            s1, s2 = s1 + bias, s2 + bias
        a1 = a1 + jnp.dot(vt_blk, jnp.exp2(s1).astype(BF16), preferred_element_type=F32)
        a2 = a2 + jnp.dot(vt_blk, jnp.exp2(s2).astype(BF16), preferred_element_type=F32)
        return a1, a2

    acc0 = jnp.zeros((VT_ROWS, bq), F32)
    k_diag, vt_diag, bias_diag = block_inputs(diag_blk)
    a1, a2 = accumulate(acc0, acc0, k_diag, vt_diag, q1a, q2a, bias_diag)
    pos = qaug_ref[0].astype(F32)
    after = [(q.astype(F32) + pos).astype(BF16) for q in (q1a, q2a)]
    before = [(q.astype(F32) - pos).astype(BF16) for q in (q1a, q2a)]
    for j in range(n_blocks - 1):
        i = jnp.where(j >= diag_blk, j + 1, j)
        k0 = pl.multiple_of(i * bk, bk)
        qx1, qx2 = [jnp.where(i < diag_blk, qb, qa) for qb, qa in zip(before, after)]
        a1, a2 = accumulate(a1, a2, k_ref[0, 0, pl.ds(k0, bk), :], vt_ref[0, 0, :, pl.ds(k0, bk)], qx1, qx2, None)

    def finish(a1, a2):
        o = a1[:HEAD_DIM] / a1[HEAD_DIM:HEAD_DIM + 1] - lam * (a2[:HEAD_DIM] / a2[HEAD_DIM:HEAD_DIM + 1])
        ms = jnp.sum(o * o, axis=0, keepdims=True) * (1.0 / HEAD_DIM)
        o_ref[0] = (o * lax.rsqrt(ms + EPS) * g_ref[...] * out_scale).astype(o_ref.dtype)

    finish(a1, a2)

    @pl.when(_not_finite(a1) | _not_finite(a2))
    def _():
        q1 = _without_ref_column(q1a, DIFF_REF_LANE)
        q2 = _without_ref_column(q2a, DIFF_REF_LANE)

        def exact(i, carry):
            m1, x1, m2, x2 = carry
            k_blk, vt_blk, bias = block_inputs(i)
            m1, x1 = _online_block(k_blk, q1, vt_blk, m1, x1, bias)
            m2, x2 = _online_block(k_blk, q2, vt_blk, m2, x2, bias)
            return m1, x1, m2, x2

        m0 = jnp.full((1, bq), NEG_BIG, F32)
        _, x1, _, x2 = lax.fori_loop(0, n_blocks, exact, (m0, acc0, m0, acc0))
        finish(x1, x2)


def _diff_flash(scalars, qa, q_aug, k, vt, subln, layer_idx, bq=512, bk=1024):
    b, h, s, d = k.shape
    bq, bk = min(bq, s), min(bk, s)
    assert bk % bq == 0, "a query block must sit inside one key block"
    lam_init = 0.8 - 0.6 * math.exp(-0.3 * layer_idx)
    rel = (np.arange(bk)[:, None] - np.arange(bq)[None, :]).astype(np.float32)
    return pl.pallas_call(
        functools.partial(_diff_flash_kernel, bk=bk, out_scale=1.0 - lam_init),
        grid=(b, h, s // bq),
        in_specs=[pl.BlockSpec(memory_space=pltpu.SMEM),
                  pl.BlockSpec((1, 1, bq, 2 * d), lambda bi, hi, qi: (bi, hi, qi, 0)),
                  pl.BlockSpec((1, bq, d), lambda bi, hi, qi: (hi, qi, 0)),
                  pl.BlockSpec((1, 1, s, d), lambda bi, hi, qi: (bi, hi, 0, 0)),
                  pl.BlockSpec((1, 1, VT_ROWS, s), lambda bi, hi, qi: (bi, hi, 0, 0)),
                  pl.BlockSpec((bk, bq), lambda bi, hi, qi: (0, 0)),
                  pl.BlockSpec((HEAD_DIM, 1), lambda bi, hi, qi: (0, 0))],
        out_specs=pl.BlockSpec((1, HEAD_DIM, bq), lambda bi, hi, qi: (bi, hi, qi)),
        out_shape=jax.ShapeDtypeStruct((b, h * HEAD_DIM, s), BF16),
        compiler_params=_params("parallel", "parallel", "arbitrary"),
        name="diff_flash",
    )(scalars, qa, q_aug, k, vt, jnp.asarray(rel), subln.reshape(HEAD_DIM, 1))


def _conv_silu_kernel(x_ref, prev_ref, next_ref, w_ref, v_ref, op_ref, qk_out, vt_out, opt_out, *, n_tiles):
    si = pl.program_id(1)
    x = x_ref[...]
    bm = x.shape[0]
    r = lax.broadcasted_iota(jnp.int32, (bm, bm), 0)
    c = lax.broadcasted_iota(jnp.int32, (bm, bm), 1)
    shift_dn = jnp.where(r == c + 1, 1.0, 0.0).astype(BF16)
    shift_up = jnp.where(r + 1 == c, 1.0, 0.0).astype(BF16)
    x_prev = jnp.dot(shift_dn, x, preferred_element_type=F32)
    x_next = jnp.dot(shift_up, x, preferred_element_type=F32)
    row = lax.broadcasted_iota(jnp.int32, x.shape, 0)
    halo_prev = jnp.where(si > 0, prev_ref[7:8, :].astype(F32), 0.0)
    halo_next = jnp.where(si < n_tiles - 1, next_ref[0:1, :].astype(F32), 0.0)
    x_prev = jnp.where(row == 0, halo_prev, x_prev)
    x_next = jnp.where(row == bm - 1, halo_next, x_next)
    w = w_ref[...]
    y = x_prev * w[0:1] + x.astype(F32) * w[1:2] + x_next * w[2:3]
    y = y * jax.nn.sigmoid(y)
    col = lax.broadcasted_iota(jnp.int32, x.shape, 1)
    qk_out[...] = jnp.where(col >= GROUP_WIDTH, y * (HEAD_DIM ** -0.5), y).astype(BF16)
    eye = _lane_selector(GROUP_WIDTH, GROUP_WIDTH, 0, 0, GROUP_WIDTH)
    vt_out[0] = lax.dot_general(eye, v_ref[...], _NT, preferred_element_type=F32).astype(BF16)
    opt_out[0] = lax.dot_general(eye, op_ref[...], _NT, preferred_element_type=F32).astype(BF16)


def _conv_silu(proj, b, s, conv_w, bm=256):
    bm = min(bm, s)
    nt = s // bm
    c = 2 * GROUP_WIDTH
    cb = U_CQK * V7X_LANES // c
    hb = bm // 8
    n8 = b * s // 8
    row = lambda bi, si: bi * nt + si
    cm_spec = pl.BlockSpec((1, GROUP_WIDTH, bm), lambda bi, si: (bi, 0, si))
    cm_shape = jax.ShapeDtypeStruct((b, GROUP_WIDTH, s), BF16)
    return pl.pallas_call(
        functools.partial(_conv_silu_kernel, n_tiles=nt),
        grid=(b, nt),
        in_specs=[pl.BlockSpec((bm, c), lambda bi, si: (row(bi, si), cb)),
                  pl.BlockSpec((8, c), lambda bi, si: (jnp.maximum(row(bi, si) * hb - 1, 0), cb)),
                  pl.BlockSpec((8, c), lambda bi, si: (jnp.minimum((row(bi, si) + 1) * hb, n8 - 1), cb)),
                  pl.BlockSpec((8, c), lambda bi, si: (0, 0)),
                  pl.BlockSpec((bm, GROUP_WIDTH), lambda bi, si: (row(bi, si), U_CV // 2)),
                  pl.BlockSpec((bm, GROUP_WIDTH), lambda bi, si: (row(bi, si), U_CO // 2))],
        out_specs=[pl.BlockSpec((bm, c), lambda bi, si: (row(bi, si), 0)), cm_spec, cm_spec],
        out_shape=[jax.ShapeDtypeStruct((b * s, c), BF16), cm_shape, cm_shape],
        compiler_params=_params("parallel", "parallel"),
        name="mlstm_conv",
    )(proj, proj, proj, jnp.concatenate([conv_w, jnp.zeros((5, c), conv_w.dtype)], axis=0), proj, proj)


def _log_sigmoid(x):
    return jnp.minimum(x, 0.0) - jnp.log(1.0 + jnp.exp(-jnp.abs(x)))


def _mlstm_chunk(qc, kc, vt1, gcol, grow, caug, m, *, backward):
    L = qc.shape[0]
    d = HEAD_DIM
    li_c, lf_c = gcol[0], _log_sigmoid(gcol[1])
    li_r, lf_r = grow[0], _log_sigmoid(grow[1])
    s_i = lax.broadcasted_iota(jnp.int32, (L, L), 0)
    j_i = lax.broadcasted_iota(jnp.int32, (L, L), 1)
    if backward:
        a_mask, b_mat, valid = j_i <= s_i, s_i >= j_i, s_i >= j_i
        bcum_row_idx, last_lane = L - 1, 0
    else:
        a_mask, b_mat, valid = j_i >= s_i, s_i <= j_i, s_i <= j_i
        bcum_row_idx, last_lane = 0, L - 1
    a = jnp.where(a_mask, lf_r, 0.0)
    a_hi = a.astype(BF16)
    a_lo = (a - a_hi.astype(F32)).astype(BF16)
    ones_b = jnp.where(b_mat, 1.0, 0.0).astype(BF16)
    yield
    e = jnp.dot(a_hi, ones_b, preferred_element_type=F32) + jnp.dot(a_lo, ones_b, preferred_element_type=F32)
    st = lax.dot_general(kc, qc, _NT, preferred_element_type=F32)
    cq = lax.dot_general(caug.astype(BF16), qc, _NT, preferred_element_type=F32)
    yield
    bcum_r = e[bcum_row_idx:bcum_row_idx + 1, :]
    dlog = jnp.where(valid, e + (li_c - lf_c), NEG_BIG)
    inter = bcum_r + m
    m_row = jnp.maximum(inter, jnp.max(dlog, axis=0, keepdims=True))
    w_intra = jnp.exp(dlog - m_row)
    w_inter = jnp.exp(inter - m_row)
    at = (st * w_intra).astype(BF16)
    b_last = bcum_r[:, last_lane:last_lane + 1]
    logw_end = b_last - bcum_r + li_r
    m_new = jnp.maximum(b_last + m, jnp.max(logw_end, axis=1, keepdims=True))
    w_end = jnp.exp(logw_end - m_new)
    decay = jnp.exp(b_last + m - m_new)
    vtw = (vt1.astype(F32) * w_end).astype(BF16)
    yield
    pv = jnp.dot(vt1, at, preferred_element_type=F32)
    u = jnp.dot(vtw, kc, preferred_element_type=F32)
    yield
    num = w_inter * cq[:d] + pv[:d]
    den = w_inter * cq[d:d + 1] + pv[d:d + 1]
    h = num / jnp.maximum(jnp.abs(den), jnp.exp(-m_row))
    caug = decay * caug + u[:d + 8]
    return h, caug, m_new


def _run_interleaved(stage_generators):
    results = [None] * len(stage_generators)
    live = list(range(len(stage_generators)))
    while live:
        for idx in list(live):
            try:
                next(stage_generators[idx])
            except StopIteration as done:
                results[idx] = done.value
                live.remove(idx)
    return results


def _mlstm_kernel(qk_ref, vt_ref, gc_ref, gb_ref, gr_ref, op_ref, ng_ref, o_ref, hf_sc, hb_sc, *, col_chunk):
    L = MLSTM_CHUNK
    d = HEAD_DIM
    s = qk_ref.shape[0]
    nc = s // L
    ones = jnp.ones((ONES_ROWS, L), BF16)
    lane = lax.broadcasted_iota(jnp.int32, (L, V7X_LANES), 1)

    def head_chunk(t0, h, gtile, gr, caug, m, backward):
        pair, sub = divmod(h, 2)
        mine = (lane >= sub * d) & (lane < (sub + 1) * d)
        q_pair = qk_ref[pl.ds(t0, L), pair * V7X_LANES:(pair + 1) * V7X_LANES]
        k_pair = qk_ref[pl.ds(t0, L), GROUP_WIDTH + pair * V7X_LANES:GROUP_WIDTH + (pair + 1) * V7X_LANES]
        qc = jnp.where(mine, q_pair, jnp.zeros_like(q_pair))
        kc = jnp.where(mine, k_pair, jnp.zeros_like(k_pair))
        vt1 = jnp.concatenate([vt_ref[0, h * d:(h + 1) * d, pl.ds(t0, L)], ones], axis=0)
        i_idx = (2 * N_HEADS if backward else 0) + h
        f_idx = i_idx + N_HEADS
        return _mlstm_chunk(qc, kc, vt1, (gtile[:, i_idx:i_idx + 1], gtile[:, f_idx:f_idx + 1]),
                            (gr[i_idx:i_idx + 1, :], gr[f_idx:f_idx + 1, :]), caug, m, backward=backward)

    def body(i, carry):
        chains, where = [], []
        for backward in (False, True):
            t0 = pl.multiple_of((nc - 1 - i if backward else i) * L, L)
            gtile = gc_ref[pl.ds(t0, L), :] + gb_ref[...]
            gr = gr_ref[0, :, pl.ds(t0, L)]
            for h in range(N_HEADS):
                caug, m = carry[(N_HEADS if backward else 0) + h]
                chains.append(head_chunk(t0, h, gtile, gr, caug, m, backward))
                where.append((hb_sc if backward else hf_sc, h, t0))
        new = []
        for (h_sc, h, t0), (hh, caug, m) in zip(where, _run_interleaved(chains)):
            h_sc[h * d:(h + 1) * d, pl.ds(t0, L)] = hh
            new.append((caug, m))
        return tuple(new)

    state0 = (jnp.zeros((d + 8, V7X_LANES), F32), jnp.zeros((1, 1), F32))
    lax.fori_loop(0, nc, body, (state0,) * (2 * N_HEADS))

    def finish(j, carry):
        c0 = pl.multiple_of(j * col_chunk, col_chunk)
        hsum = hf_sc[:, pl.ds(c0, col_chunk)] + hb_sc[:, pl.ds(c0, col_chunk)]
        normed = []
        for h in range(N_HEADS):
            x = hsum[h * d:(h + 1) * d]
            normed.append(x * lax.rsqrt(jnp.sum(x * x, axis=0, keepdims=True) * (1.0 / d) + EPS))
        hn = jnp.concatenate(normed, axis=0) * ng_ref[...]
        gate = jax.nn.sigmoid(op_ref[0, :, pl.ds(c0, col_chunk)].astype(F32))
        o_ref[0, :, pl.ds(c0, col_chunk)] = (hn * gate).astype(o_ref.dtype)
        return carry

    lax.fori_loop(0, s // col_chunk, finish, 0)


def _mlstm(qk, vt, gates, gate_bias, grow, opre_t, norm_g, b, s):
    c = GROUP_WIDTH
    once = pl.Buffered(1)
    return pl.pallas_call(
        functools.partial(_mlstm_kernel, col_chunk=min(1024, s)),
        grid=(b,),
        in_specs=[pl.BlockSpec((s, 2 * c), lambda bi: (bi, 0), pipeline_mode=once),
                  pl.BlockSpec((1, c, s), lambda bi: (bi, 0, 0), pipeline_mode=once),
                  pl.BlockSpec((s, V7X_LANES), lambda bi: (bi, 0), pipeline_mode=once),
                  pl.BlockSpec((1, V7X_LANES), lambda bi: (0, 0)),
                  pl.BlockSpec((1, 4 * N_HEADS, s), lambda bi: (bi, 0, 0), pipeline_mode=once),
                  pl.BlockSpec((1, c, s), lambda bi: (bi, 0, 0), pipeline_mode=once),
                  pl.BlockSpec((c, 1), lambda bi: (0, 0))],
        out_specs=pl.BlockSpec((1, c, s), lambda bi: (bi, 0, 0)),
        out_shape=jax.ShapeDtypeStruct((b, c, s), BF16),
        scratch_shapes=[pltpu.VMEM((c, s), F32), pltpu.VMEM((c, s), F32)],
        compiler_params=_params("parallel"),
        name="mlstm_scan",
    )(qk, vt, gates, gate_bias, grow, opre_t, norm_g.reshape(c, 1))


def _na_kernel(q_ref, k_ref, v_ref, bias_ref, o_ref, *, rows_per_step, n_rows, group):
    blk = pl.program_id(1)
    win = NA_ROWS * GRID_W
    lane = lax.broadcasted_iota(jnp.int32, (GRID_W, V7X_LANES), 1)
    def head_chain(q_pair, kw, vw, bias, sub):
        mine = (lane >= sub * HEAD_DIM) & (lane < (sub + 1) * HEAD_DIM)
        qh = jnp.where(mine, q_pair, jnp.zeros_like(q_pair))
        sc = lax.dot_general(qh, kw, _NT, preferred_element_type=F32)
        yield
        sc = sc + bias
        p = jnp.exp(sc - jnp.max(sc, axis=-1, keepdims=True))
        inv_l = 1.0 / jnp.sum(p, axis=-1, keepdims=True)
        yield
        return jnp.dot(p.astype(BF16), vw, preferred_element_type=F32) * inv_l

    for i0 in range(0, rows_per_step, group):
        chains, where = [], []
        for i in range(i0, i0 + group):
            r = blk * rows_per_step + i
            r0 = jnp.clip(r - NA_ROWS // 2, 0, n_rows - NA_ROWS)
            dsel = r - r0
            k0 = pl.multiple_of(r0 * GRID_W, GRID_W)
            rows = slice(i * GRID_W, (i + 1) * GRID_W)
            for pair in range(N_HEADS // 2):
                lanes = slice(pair * V7X_LANES, (pair + 1) * V7X_LANES)
                q_pair = q_ref[rows, lanes] * (HEAD_DIM ** -0.5)
                kw = k_ref[pl.ds(k0, win), lanes]
                vw = v_ref[pl.ds(k0, win), lanes]
                for sub in range(2):
                    chains.append(head_chain(q_pair, kw, vw, bias_ref[2 * pair + sub, dsel], sub))
                where.append((rows, lanes))
        outs = _run_interleaved(chains)
        for n, (rows, lanes) in enumerate(where):
            o_ref[rows, lanes] = jnp.where(lane < HEAD_DIM, outs[2 * n], outs[2 * n + 1]).astype(o_ref.dtype)


def _na(proj, b, s, bias, rows_per_step=8, group=8):
    n_rows = s // GRID_W
    rows_per_step = min(rows_per_step, n_rows)
    bm = rows_per_step * GRID_W
    nt = s // bm
    return pl.pallas_call(
        functools.partial(_na_kernel, rows_per_step=rows_per_step, n_rows=n_rows,
                          group=math.gcd(group, rows_per_step)),
        grid=(b, nt),
        in_specs=[pl.BlockSpec((bm, GROUP_WIDTH), lambda bi, ri: (bi * nt + ri, U_DQ // 2)),
                  pl.BlockSpec((s, GROUP_WIDTH), lambda bi, ri: (bi, U_DK // 2)),
                  pl.BlockSpec((s, GROUP_WIDTH), lambda bi, ri: (bi, U_DV // 2)),
                  pl.BlockSpec(bias.shape, lambda bi, ri: (0, 0, 0, 0))],
        out_specs=pl.BlockSpec((bm, GROUP_WIDTH), lambda bi, ri: (bi * nt + ri, 0)),
        out_shape=jax.ShapeDtypeStruct((b * s, GROUP_WIDTH), BF16),
        compiler_params=_params("parallel", "arbitrary"),
        name="na_attn",
    )(proj, proj, proj, bias)


def _na_bias_table(rpb, n_rows):
    wr = min(NA_ROWS, n_rows)
    cols = np.arange(GRID_W)
    col_start = np.clip(cols - NA_COLS // 2, 0, GRID_W - NA_COLS)
    ck = np.arange(GRID_W)[None, :]
    valid = (ck >= col_start[:, None]) & (ck < col_start[:, None] + NA_COLS)
    crel = ck - cols[:, None] + NA_COLS - 1
    rrel = np.arange(wr)[None, :] - np.arange(wr)[:, None] + NA_ROWS - 1
    c_sel = ((crel[None] == np.arange(2 * NA_COLS - 1)[:, None, None]) & valid[None]).astype(np.float32)
    r_sel = (rrel[None] == np.arange(2 * NA_ROWS - 1)[:, None, None]).astype(np.float32)
    t = jnp.einsum('hab,adw,bqk->hdqwk', rpb.astype(F32), r_sel, c_sel, precision=lax.Precision.HIGHEST)
    t = t + np.where(valid, 0.0, NEG_BIG).astype(np.float32)[None, None, :, None, :]
    return t.reshape(rpb.shape[0], wr, GRID_W, wr * GRID_W)


def _outproj_kernel(*refs, n_src, n_a, tiles_per_seq):
    (ya_ref, yb_ref, yc_ref, yd_ref), x_refs, (w_ref, o_ref) = refs[:4], refs[4:4 + n_src], refs[4 + n_src:]
    x = _read_rows(x_refs, pl.program_id(0) * tiles_per_seq + pl.program_id(1), n_a)
    acc = x + jnp.dot(yd_ref[...], w_ref[3], preferred_element_type=F32)
    for g, y_ref in enumerate((ya_ref, yb_ref, yc_ref)):
        acc = acc + lax.dot_general(y_ref[0], w_ref[g], (((0,), (0,)), ((), ())),
                                    preferred_element_type=F32)
    o_ref[...] = acc


def _outproj(ya, yb, yc, yd, xs, w, b, s, bm=512):
    bm = min(bm, s)
    nt = s // bm
    d = xs[0].shape[1]
    yspec = pl.BlockSpec((1, GROUP_WIDTH, bm), lambda bi, si: (bi, 0, si))
    x_specs, n_a = _row_source_specs(xs, bm, lambda bi, si: bi * nt + si)
    return pl.pallas_call(
        functools.partial(_outproj_kernel, n_src=len(xs), n_a=n_a, tiles_per_seq=nt),
        grid=(b, nt),
        in_specs=[yspec, yspec, yspec,
                  pl.BlockSpec((bm, GROUP_WIDTH), lambda bi, si: (bi * nt + si, 0))] + x_specs
                 + [pl.BlockSpec(w.shape, lambda bi, si: (0, 0, 0))],
        out_specs=pl.BlockSpec((bm, d), lambda bi, si: (bi * nt + si, 0)),
        out_shape=jax.ShapeDtypeStruct((b * s, d), F32),
        compiler_params=_params("parallel", "parallel"),
        name="outproj",
    )(ya, yb, yc, yd, *xs, w)


FFN_ROW_CHUNKS = 2


def _swiglu_accumulate(hn_sc, acc_sc, w_gate, w_up, w_down):
    rows = hn_sc.shape[0] // FFN_ROW_CHUNKS

    def chunk(c):
        sl = slice(c * rows, (c + 1) * rows)
        hn = hn_sc[sl, :]
        gate = jnp.dot(hn, w_gate, preferred_element_type=F32)
        up = jnp.dot(hn, w_up, preferred_element_type=F32)
        yield
        act = (gate * jax.nn.sigmoid(gate) * up).astype(BF16)
        yield
        acc_sc[sl, :] += jnp.dot(act, w_down, preferred_element_type=F32)

    _run_interleaved([chunk(c) for c in range(FFN_ROW_CHUNKS)])


def _ffn_kernel(x_ref, g_ref, wg_ref, wu_ref, wd_ref, o_ref, hn_sc, acc_sc):
    f = pl.program_id(1)

    @pl.when(f == 0)
    def _():
        x = x_ref[...]
        hn_sc[...] = (_rms(x, x.shape[-1]) * g_ref[...]).astype(BF16)
        acc_sc[...] = jnp.zeros_like(acc_sc)

    _swiglu_accumulate(hn_sc, acc_sc, wg_ref[...], wu_ref[...], wd_ref[...])

    @pl.when(f == pl.num_programs(1) - 1)
    def _():
        o_ref[...] = x_ref[...] + acc_sc[...]


def _ffn(x, g, wg, wu, wd, bm=512):
    n, d = x.shape
    ff = wg.shape[1]
    bm = min(bm, n)
    bf = ff
    once = pl.Buffered(1)
    return pl.pallas_call(
        _ffn_kernel,
        grid=(n // bm, ff // bf),
        in_specs=[pl.BlockSpec((bm, d), lambda i, f: (i, 0)),
                  pl.BlockSpec((1, d), lambda i, f: (0, 0)),
                  pl.BlockSpec((d, bf), lambda i, f: (0, f), pipeline_mode=once),
                  pl.BlockSpec((d, bf), lambda i, f: (0, f), pipeline_mode=once),
                  pl.BlockSpec((bf, d), lambda i, f: (f, 0), pipeline_mode=once)],
        out_specs=pl.BlockSpec((bm, d), lambda i, f: (i, 0)),
        out_shape=jax.ShapeDtypeStruct(x.shape, F32),
        scratch_shapes=[pltpu.VMEM((bm, d), BF16), pltpu.VMEM((bm, d), F32)],
        compiler_params=_params("parallel", "arbitrary"),
        name="ffn",
    )(x, g.reshape(1, d), wg, wu, wd)


R_E1, R_E2, R_W1, R_W2, R_RANK1, R_RANK2 = range(6)


def _lane_pack(lane, cols):
    out = jnp.zeros(lane.shape, F32)
    for idx, col in cols:
        out = out + jnp.where(lane == idx, col, 0.0)
    return out


def _router_kernel(x_ref, g_ref, wr_ref, hn_ref, route_ref, counts_ref, carry_sc):
    @pl.when(pl.program_id(0) == 0)
    def _():
        carry_sc[...] = jnp.zeros_like(carry_sc)

    x = x_ref[...]
    bm = x.shape[0]
    hn = _rms(x, x.shape[-1]) * g_ref[...]
    hn_ref[...] = hn
    logits = jnp.dot(hn, wr_ref[...], preferred_element_type=F32, precision=lax.Precision.HIGHEST)
    lane = lax.broadcasted_iota(jnp.int32, logits.shape, 1)
    logits = jnp.where(lane < N_EXPERTS, logits, NEG_BIG)
    m1 = jnp.max(logits, axis=-1, keepdims=True)
    i1 = jnp.min(jnp.where(logits == m1, lane, V7X_LANES), axis=-1, keepdims=True)
    rest = jnp.where(lane == i1, NEG_BIG, logits)
    m2 = jnp.max(rest, axis=-1, keepdims=True)
    i2 = jnp.min(jnp.where(rest == m2, lane, V7X_LANES), axis=-1, keepdims=True)
    e2 = jnp.exp(m2 - m1)
    w1 = 1.0 / (1.0 + e2)
    w2 = e2 / (1.0 + e2)
    hot1 = jnp.where(lane == i1, 1.0, 0.0)
    hot2 = jnp.where(lane == i2, 1.0, 0.0)
    r_i = lax.broadcasted_iota(jnp.int32, (bm, bm), 0)
    c_i = lax.broadcasted_iota(jnp.int32, (bm, bm), 1)
    below = jnp.where(c_i < r_i, 1.0, 0.0).astype(BF16)
    before1 = jnp.dot(below, hot1.astype(BF16), preferred_element_type=F32)
    before2 = jnp.dot(below, hot2.astype(BF16), preferred_element_type=F32)
    cnt1 = jnp.sum(hot1, axis=0, keepdims=True)
    cnt2 = jnp.sum(hot2, axis=0, keepdims=True)
    carry = carry_sc[...]
    rank1 = jnp.sum(hot1 * (before1 + carry), axis=-1, keepdims=True)
    rank2 = jnp.sum(hot2 * (before2 + carry + cnt1), axis=-1, keepdims=True)
    carry = carry + cnt1 + cnt2
    carry_sc[...] = carry
    counts_ref[...] = carry
    route_ref[...] = _lane_pack(lane, ((R_E1, i1.astype(F32)), (R_E2, i2.astype(F32)), (R_W1, w1), (R_W2, w2),
                                       (R_RANK1, rank1), (R_RANK2, rank2)))


def _router(x, g, w_router, bm=512):
    n, d = x.shape
    bm = min(bm, n)
    wr = jnp.concatenate([w_router, jnp.zeros((d, V7X_LANES - N_EXPERTS), w_router.dtype)], axis=1)
    return pl.pallas_call(
        _router_kernel,
        grid=(n // bm,),
        in_specs=[pl.BlockSpec((bm, d), lambda i: (i, 0)),
                  pl.BlockSpec((1, d), lambda i: (0, 0)),
                  pl.BlockSpec((d, V7X_LANES), lambda i: (0, 0))],
        out_specs=[pl.BlockSpec((bm, d), lambda i: (i, 0)),
                   pl.BlockSpec((bm, V7X_LANES), lambda i: (i, 0)),
                   pl.BlockSpec((1, V7X_LANES), lambda i: (0, 0))],
        out_shape=[jax.ShapeDtypeStruct((n, d), F32),
                   jax.ShapeDtypeStruct((n, V7X_LANES), F32),
                   jax.ShapeDtypeStruct((1, V7X_LANES), F32)],
        scratch_shapes=[pltpu.VMEM((1, V7X_LANES), F32)],
        compiler_params=_params("arbitrary"),
        name="moe_router",
    )(x, g.reshape(1, d), wr)


def _route_plan(route, counts, tm):
    n = route.shape[0]
    counts = counts[0, :N_EXPERTS].astype(jnp.int32)
    padded = ((counts + tm - 1) // tm) * tm
    g_end = jnp.cumsum(padded)
    g_start = (g_end - padded).astype(F32)
    experts = jnp.arange(N_EXPERTS, dtype=F32)[None, :]
    start1 = jnp.sum(jnp.where(route[:, R_E1:R_E1 + 1] == experts, g_start[None, :], 0.0), axis=1)
    start2 = jnp.sum(jnp.where(route[:, R_E2:R_E2 + 1] == experts, g_start[None, :], 0.0), axis=1)
    pos = jnp.stack([start1 + route[:, R_RANK1], start2 + route[:, R_RANK2]], axis=1).astype(jnp.int32)
    n_tiles = 2 * n // tm + N_EXPERTS
    tile_start = jnp.arange(n_tiles, dtype=jnp.int32) * tm
    tile_expert = jnp.minimum(jnp.sum(tile_start[:, None] >= g_end[None, :], axis=1), N_EXPERTS - 1)
    n_used = (g_end[-1:] // tm).astype(jnp.int32)
    return pos.reshape(-1), tile_expert.astype(jnp.int32), n_used, n_tiles


def _row_copies(pos_ref, r, src_at, dst_at, sem):
    copies = []
    for c in range(2):
        p = pos_ref[2 * r + c]
        copies.append(pltpu.make_async_copy(src_at(r, c, p), dst_at(r, c, p), sem))
    return copies


def _move_rows(pos_ref, n_rows, src_at, dst_at, sem):
    def start(r, carry):
        for cp in _row_copies(pos_ref, r, src_at, dst_at, sem):
            cp.start()
        return carry

    def wait(r, carry):
        for cp in _row_copies(pos_ref, r, src_at, dst_at, sem):
            cp.wait()
        return carry

    lax.fori_loop(0, n_rows, start, 0, unroll=8)
    lax.fori_loop(0, n_rows, wait, 0, unroll=8)


def _dispatch_kernel(pos_ref, hn_ref, xg_in_ref, xg_ref, sem):
    del xg_in_ref
    _move_rows(pos_ref, hn_ref.shape[0],
               lambda r, c, p: hn_ref.at[pl.ds(r, 1)],
               lambda r, c, p: xg_ref.at[pl.ds(p, 1)], sem)


def _dispatch(pos, hn, n_rows, bm=512):
    n, d = hn.shape
    bm = min(bm, n)
    return pl.pallas_call(
        _dispatch_kernel,
        grid=(n // bm,),
        in_specs=[pl.BlockSpec((2 * bm,), lambda i: (i,), memory_space=pltpu.SMEM),
                  pl.BlockSpec((bm, d), lambda i: (i, 0)),
                  pl.BlockSpec(memory_space=pl.ANY)],
        out_specs=pl.BlockSpec(memory_space=pl.ANY),
        out_shape=jax.ShapeDtypeStruct((n_rows, d), F32),
        scratch_shapes=[pltpu.SemaphoreType.DMA(())],
        input_output_aliases={2: 0},
        compiler_params=_params("arbitrary"),
        name="moe_dispatch",
    )(pos, hn, jnp.zeros((n_rows, d), F32))


def _expert_ffn_kernel(te_ref, nu_ref, xg_ref, wg_ref, wu_ref, wd_ref, o_ref, hn_sc, acc_sc):
    del te_ref
    t = pl.program_id(0)
    f = pl.program_id(1)
    last = pl.num_programs(1) - 1
    used = t < nu_ref[0]

    @pl.when(used & (f == 0))
    def _():
        hn_sc[...] = xg_ref[...].astype(BF16)
        acc_sc[...] = jnp.zeros_like(acc_sc)

    @pl.when(used)
    def _():
        _swiglu_accumulate(hn_sc, acc_sc, wg_ref[0], wu_ref[0], wd_ref[0])

    @pl.when(used & (f == last))
    def _():
        o_ref[...] = acc_sc[...]

    @pl.when(jnp.logical_not(used) & (f == last))
    def _():
        o_ref[...] = jnp.zeros_like(o_ref)


def _expert_ffn(tile_expert, n_used, xg, wg, wu, wd, tm, bf=7 * V7X_MXU_COLS):
    rows, d = xg.shape
    ff = wg.shape[2]
    bf = math.gcd(bf, ff)
    grid_spec = pltpu.PrefetchScalarGridSpec(
        num_scalar_prefetch=2,
        grid=(rows // tm, ff // bf),
        in_specs=[pl.BlockSpec((tm, d), lambda t, f, te, nu: (t, 0)),
                  pl.BlockSpec((1, d, bf), lambda t, f, te, nu: (te[t], 0, f)),
                  pl.BlockSpec((1, d, bf), lambda t, f, te, nu: (te[t], 0, f)),
                  pl.BlockSpec((1, bf, d), lambda t, f, te, nu: (te[t], f, 0))],
        out_specs=pl.BlockSpec((tm, d), lambda t, f, te, nu: (t, 0)),
        scratch_shapes=[pltpu.VMEM((tm, d), BF16), pltpu.VMEM((tm, d), F32)])
    return pl.pallas_call(
        _expert_ffn_kernel,
        grid_spec=grid_spec,
        out_shape=jax.ShapeDtypeStruct((rows, d), F32),
        compiler_params=_params("arbitrary", "arbitrary"),
        name="moe_ffn",
    )(tile_expert, n_used, xg, wg, wu, wd)


def _combine_kernel(pos_ref, x_ref, route_ref, g_ref, yg_ref, o_ref, buf, sem, *, final_norm):
    _move_rows(pos_ref, x_ref.shape[0],
               lambda r, c, p: yg_ref.at[pl.ds(p, 1)],
               lambda r, c, p: buf.at[c, pl.ds(r, 1)], sem)
    route = route_ref[...]
    y = x_ref[...] + route[:, R_W1:R_W1 + 1] * buf[0] + route[:, R_W2:R_W2 + 1] * buf[1]
    if final_norm:
        y = _rms(y, y.shape[-1]) * g_ref[...]
    o_ref[...] = y


def _combine(pos, x, route, yg, g_final, final_norm, row0, n_rows, bm=512):
    d = x.shape[1]
    bm = math.gcd(bm, math.gcd(row0, n_rows)) if row0 else min(bm, n_rows)
    off = row0 // bm
    return pl.pallas_call(
        functools.partial(_combine_kernel, final_norm=final_norm),
        grid=(n_rows // bm,),
        in_specs=[pl.BlockSpec((2 * bm,), lambda i: (i + off,), memory_space=pltpu.SMEM),
                  pl.BlockSpec((bm, d), lambda i: (i + off, 0)),
                  pl.BlockSpec((bm, V7X_LANES), lambda i: (i + off, 0)),
                  pl.BlockSpec((1, d), lambda i: (0, 0)),
                  pl.BlockSpec(memory_space=pl.ANY)],
        out_specs=pl.BlockSpec((bm, d), lambda i: (i, 0)),
        out_shape=jax.ShapeDtypeStruct((n_rows, d), F32),
        scratch_shapes=[pltpu.VMEM((2, bm, d), F32), pltpu.SemaphoreType.DMA(())],
        compiler_params=_params("arbitrary"),
        name="moe_combine",
    )(pos, x, route, g_final.reshape(1, d), yg)


def _moe(x, g, w_router, wg, wu, wd, g_final, final_norm, row_splits, tm=512):
    hn, route, counts = _router(x, g, w_router)
    pos, tile_expert, n_used, n_tiles = _route_plan(route, counts, tm)
    xg = _dispatch(pos, hn, n_tiles * tm)
    yg = _expert_ffn(tile_expert, n_used, xg, wg, wu, wd, tm)
    return [_combine(pos, x, route, yg, g_final, final_norm, row0, n_rows) for row0, n_rows in row_splits]


def _final_norm_kernel(x_ref, g_ref, o_ref):
    x = x_ref[...]
    o_ref[...] = _rms(x, x.shape[-1]) * g_ref[...]


def _final_norm(x, g, bm=1024):
    n, d = x.shape
    bm = min(bm, n)
    return pl.pallas_call(
        _final_norm_kernel,
        grid=(n // bm,),
        in_specs=[pl.BlockSpec((bm, d), lambda i: (i, 0)), pl.BlockSpec((1, d), lambda i: (0, 0))],
        out_specs=pl.BlockSpec((bm, d), lambda i: (i, 0)),
        out_shape=jax.ShapeDtypeStruct(x.shape, F32),
        compiler_params=_params("parallel"),
        name="final_norm",
    )(x, g.reshape(1, d))


def _token_mix(xs, i, b, s, p):
    w_main, w_gates = _prep_w_in(p['w_in'][i])
    proj, gates = _inproj(xs, p['norm_mix'][i], w_main, w_gates)

    cos_t, sin_t = _rope_tables(s)
    q_a, k_a, vt_a = _mla_prep(proj, b, s, cos_t, sin_t, *_prep_mla_weights(
        p['mla_q_norm'][i], p['mla_kv_norm'][i], p['mla_w_uq'][i], p['mla_w_ukv'][i]))
    y_a = _mla_flash(q_a, k_a, vt_a)

    lp = p['diff_lambda'][i].astype(F32)
    lam_init = 0.8 - 0.6 * math.exp(-0.3 * i)
    lam = jnp.exp(jnp.sum(lp[0] * lp[1])) - jnp.exp(jnp.sum(lp[2] * lp[3])) + lam_init
    slopes = 2.0 ** (-8.0 * jnp.arange(1, N_HEADS + 1, dtype=F32) / N_HEADS)
    scalars = jnp.concatenate([-slopes * LOG2E, lam[None]]).astype(F32)
    k_aug, q_aug = _alibi_tables(s)
    qa_b, k_b, vt_b = _diff_prep(proj, k_aug, b, s)
    y_b = _diff_flash(scalars, qa_b, q_aug, k_b, vt_b, p['diff_subln'][i], i)

    qk_c, vt_c, opt_c = _conv_silu(proj, b, s, p['mlstm_conv'][i])
    n_gates = 4 * N_HEADS
    gate_bias = jnp.concatenate([p['mlstm_gate_bias'][i], jnp.zeros((V7X_LANES - n_gates,), F32)])[None, :]
    grow = (gates[:, :n_gates] + gate_bias[:, :n_gates]).reshape(b, s, n_gates).transpose(0, 2, 1)
    y_c = _mlstm(qk_c, vt_c, gates, gate_bias, grow, opt_c, p['mlstm_norm'][i], b, s)

    y_d = _na(proj, b, s, _na_bias_table(p['na_rpb'][i], s // GRID_W))

    w_out = p['w_out'][i].reshape(4, GROUP_WIDTH, -1).astype(BF16)
    return _outproj(y_a, y_b, y_c, y_d, xs, w_out, b, s)


def _trunk(x_parts, p, depth):
    batch_splits = tuple(x.shape[0] for x in x_parts)
    b = sum(batch_splits)
    _, s, d = x_parts[0].shape
    starts = np.cumsum((0,) + batch_splits)[:-1]
    row_splits = [(int(b0) * s, int(nb) * s) for b0, nb in zip(starts, batch_splits)]
    xs = [x.reshape(-1, d) for x in x_parts]
    for i in range(depth):
        x = _token_mix(xs, i, b, s, p)
        j = i // 2
        last = i == depth - 1
        if i % 2 == 0:
            x = _ffn(x, p['norm_ffn'][i], p['ffn_w_gate'][j].astype(BF16), p['ffn_w_up'][j].astype(BF16),
                     p['ffn_w_down'][j].astype(BF16))
            if last:
                x = _final_norm(x, p['norm_final'])
                outs = [x[r0:r0 + nr] for r0, nr in row_splits]
        else:
            outs = _moe(x, p['norm_ffn'][i], p['moe_router'][j], p['moe_w_gate'][j].astype(BF16),
                        p['moe_w_up'][j].astype(BF16), p['moe_w_down'][j].astype(BF16), p['norm_final'],
                        final_norm=last, row_splits=row_splits if last else [(0, b * s)])
            x = outs[0]
        xs = [x]
    return [o.reshape(nb, s, d) for o, nb in zip(outs, batch_splits)]


def kernel(x_prompt, x_sample, norm_mix, norm_ffn, w_in, w_out, mla_q_norm, mla_kv_norm, mla_w_uq, mla_w_ukv,
           diff_lambda, diff_subln, mlstm_conv, mlstm_gate_bias, mlstm_norm, na_rpb, ffn_w_gate, ffn_w_up,
           ffn_w_down, moe_router, moe_w_gate, moe_w_up, moe_w_down, norm_final):
    p = dict(norm_mix=norm_mix, norm_ffn=norm_ffn, w_in=w_in, w_out=w_out, mla_q_norm=mla_q_norm,
             mla_kv_norm=mla_kv_norm, mla_w_uq=mla_w_uq, mla_w_ukv=mla_w_ukv, diff_lambda=diff_lambda,
             diff_subln=diff_subln, mlstm_conv=mlstm_conv, mlstm_gate_bias=mlstm_gate_bias,
             mlstm_norm=mlstm_norm, na_rpb=na_rpb, ffn_w_gate=ffn_w_gate, ffn_w_up=ffn_w_up,
             ffn_w_down=ffn_w_down, moe_router=moe_router, moe_w_gate=moe_w_gate, moe_w_up=moe_w_up,
             moe_w_down=moe_w_down, norm_final=norm_final)
    depth = norm_mix.shape[0]
    y_prompt, y_sample = _trunk([x_prompt, x_sample], p, depth)
    return (y_prompt, y_sample)
```
